```python
import math
import jax
import jax.numpy as jnp
from jax import lax
import numpy as np

D_MODEL = 1024
BATCH = 8
SEQ = 4096
DEPTH = 1

EPS = 1e-6
A_HEADS = 8
A_LAT = 128
A_VDIM = 64
IDX_HEADS = 4
IDX_DIM = 64
IDX_TOPK_MAX = 256
Q_BLOCK = 128
B_HEADS = 4
B_DIM = 128
CONV_W = 4
CHUNK = 64
PEER_HEADS = 8
PEER_NKEYS = 128
PEER_QDIM = 256
PEER_TOPK = 16
PEER_BLOCK = 128
N_EXPERTS = PEER_NKEYS * PEER_NKEYS

A_WIDTH = A_HEADS * A_VDIM
B_WIDTH = B_HEADS * B_DIM
MIX_WIDTH = A_WIDTH + B_WIDTH
IN_SPLITS = (A_HEADS * A_LAT, A_LAT, IDX_HEADS * IDX_DIM, IDX_DIM, IDX_HEADS,
             B_WIDTH, B_WIDTH, B_WIDTH, B_WIDTH, B_HEADS, B_HEADS)
IN_WIDTH = sum(IN_SPLITS)

kernel_name = 'hybrid_dsa_gdn_peer_adaln_layer'


def rms(x):
    xf = x.astype(jnp.float32)
    return xf * lax.rsqrt(jnp.mean(xf * xf, -1, keepdims=True) + EPS)


def l2n(x):
    return x * lax.rsqrt(jnp.sum(x * x, -1, keepdims=True) + EPS)


def modulate(xn, shift, scale):
    return xn * (1.0 + scale[:, None, :]) + shift[:, None, :]


def causal_dwconv(x, w):
    C = x.shape[-1]
    return lax.conv_general_dilated(
        x, w[:, None, :].astype(x.dtype), window_strides=(1,),
        padding=[(w.shape[0] - 1, 0)], dimension_numbers=('NWC', 'WIO', 'NWC'),
        feature_group_count=C)


def dsa_attention(q_lat, kv, iq, ik, iw, q_gain, k_gain, w_uv):
    B, S = kv.shape[:2]
    topk = min(IDX_TOPK_MAX, S // 4)
    q = rms(q_lat) * q_gain
    k = rms(kv) * k_gain
    slopes = jnp.exp2(-8.0 * jnp.arange(1, A_HEADS + 1, dtype=jnp.float32) / A_HEADS)
    nb = S // Q_BLOCK
    key_pos = jnp.arange(S)
    scale = A_LAT ** -0.5
    idx_scale = (IDX_DIM ** -0.5) * (IDX_HEADS ** -0.5)

    def blocks(a):
        return a.reshape((B, nb, Q_BLOCK) + a.shape[2:]).swapaxes(0, 1)

    def one_block(args):
        qb, iqb, iwb, start = args
        qpos = start + jnp.arange(Q_BLOCK)
        causal = key_pos[None, :] <= qpos[:, None]
        logits = jnp.einsum('bqhd,bsd->bqhs', iqb, ik)
        score = jnp.einsum('bqhs,bqh->bqs', jax.nn.relu(logits), iwb).astype(jnp.float32) * idx_scale
        score = jnp.where(causal[None], score, -jnp.inf)
        _, sel = lax.top_k(score, topk)
        kv_sel = jax.vmap(lambda kb, ib: kb[ib])(k, sel)
        s = jnp.einsum('bqhr,bqkr->bqhk', qb, kv_sel) * scale
        dist = (qpos[None, :, None] - sel).astype(jnp.float32)
        s = s - slopes[None, None, :, None] * dist[:, :, None, :]
        valid = (sel <= qpos[None, :, None])[:, :, None, :]
        p = jax.nn.softmax(jnp.where(valid, s, -jnp.inf), axis=-1)
        return jnp.einsum('bqhk,bqkr->bqhr', p, kv_sel)

    starts = jnp.arange(nb, dtype=jnp.int32) * Q_BLOCK
    o = lax.map(one_block, (blocks(q), blocks(iq), blocks(iw), starts))
    o = o.swapaxes(0, 1).reshape(B, S, A_HEADS, A_LAT)
    return jnp.einsum('bshr,hrv->bshv', o, w_uv).reshape(B, S, A_WIDTH)


def gated_deltanet(q, k, v, z, a, b, conv_w, a_log, dt_bias, norm_gain):
    B, S = q.shape[:2]
    qkv = jax.nn.silu(causal_dwconv(jnp.concatenate([q, k, v], -1), conv_w)).astype(jnp.float32)
    q, k, v = jnp.split(qkv, 3, axis=-1)

    def heads(t):
        return t.reshape(B, S, B_HEADS, B_DIM).transpose(0, 2, 1, 3)

    q = l2n(heads(q)) * (B_DIM ** -0.5)
    k = l2n(heads(k))
    v = heads(v)
    beta = jax.nn.sigmoid(b.astype(jnp.float32)).transpose(0, 2, 1)
    g = (-jnp.exp(a_log.astype(jnp.float32))
         * jax.nn.softplus(a.astype(jnp.float32) + dt_bias.astype(jnp.float32))).transpose(0, 2, 1)
    n = S // CHUNK

    def chunks(t):
        return t.reshape(t.shape[:2] + (n, CHUNK) + t.shape[3:])

    q, k, v, beta, g = (chunks(t) for t in (q, k, v, beta, g))
    G = jnp.cumsum(g, axis=-1)
    tril = jnp.tril(jnp.ones((CHUNK, CHUNK), bool))
    strict = jnp.tril(jnp.ones((CHUNK, CHUNK), bool), -1)
    decay = jnp.exp(jnp.where(tril, G[..., :, None] - G[..., None, :], -jnp.inf))
    kb = k * beta[..., None]
    Lmat = jnp.where(strict, jnp.einsum('bhnid,bhnjd->bhnij', kb, k) * decay, 0.0)
    eye = jnp.eye(CHUNK, dtype=jnp.float32)
    T = lax.linalg.triangular_solve(eye + Lmat, jnp.broadcast_to(eye, Lmat.shape),
                                    left_side=True, lower=True, unit_diagonal=True)
    u = T @ (v * beta[..., None])
    w = T @ (kb * jnp.exp(G)[..., None])
    attn_intra = jnp.einsum('bhnid,bhnjd->bhnij', q, k) * decay
    q_dec = q * jnp.exp(G)[..., None]
    k_dec = k * jnp.exp(G[..., -1:] - G)[..., None]
    g_last = jnp.exp(G[..., -1])

    def step(state, xs):
        u_c, w_c, a_c, qd_c, kd_c, gl_c = xs
        v_new = u_c - jnp.einsum('bhck,bhkv->bhcv', w_c, state)
        o = jnp.einsum('bhck,bhkv->bhcv', qd_c, state) + jnp.einsum('bhij,bhjv->bhiv', a_c, v_new)
        state = state * gl_c[..., None, None] + jnp.einsum('bhck,bhcv->bhkv', kd_c, v_new)
        return state, o

    xs = tuple(jnp.moveaxis(t, 2, 0) for t in (u, w, attn_intra, q_dec, k_dec, g_last))
    state0 = jnp.zeros((B, B_HEADS, B_DIM, B_DIM), jnp.float32)
    _, o = lax.scan(step, state0, xs)
    o = jnp.moveaxis(o, 0, 2).reshape(B, B_HEADS, S, B_DIM).transpose(0, 2, 1, 3)
    zh = z.reshape(B, S, B_HEADS, B_DIM).astype(jnp.float32)
    o = rms(o) * norm_gain * jax.nn.silu(zh)
    return o.reshape(B, S, B_WIDTH)


def peer(h, w_query, sub_keys, u_tab, v_tab):
    B, S, D = h.shape
    hb_all = h.reshape(-1, PEER_BLOCK, D)

    def one_block(hb):
        q = (hb @ w_query).reshape(PEER_BLOCK, PEER_HEADS, 2, PEER_QDIM // 2)
        s = jnp.einsum('thpd,hpnd->thpn', q, sub_keys).astype(jnp.float32)
        s_half, i_half = lax.top_k(s, PEER_TOPK)
        cand = s_half[..., 0, :, None] + s_half[..., 1, None, :]
        cand_s, cand_i = lax.top_k(cand.reshape(PEER_BLOCK, PEER_HEADS, PEER_TOPK * PEER_TOPK), PEER_TOPK)
        i1 = jnp.take_along_axis(i_half[..., 0, :], cand_i // PEER_TOPK, axis=-1)
        i2 = jnp.take_along_axis(i_half[..., 1, :], cand_i % PEER_TOPK, axis=-1)
        expert = i1 * PEER_NKEYS + i2
        gate = jax.nn.softmax(cand_s, axis=-1)
        act = jax.nn.gelu(jnp.einsum('tpkd,td->tpk', u_tab[expert], hb).astype(jnp.float32),
                          approximate=False) * gate
        return jnp.einsum('tpk,tpkd->td', act.astype(v_tab.dtype), v_tab[expert])

    return lax.map(one_block, hb_all).reshape(B, S, D)


def setup_inputs(seed: int = 0) -> dict:
    key = jax.random.key(seed)
    ks = jax.random.split(key, 18)
    f32 = jnp.float32
    L = DEPTH

    def nrm(k, shape, fan_in, gain=1.0):
        return jax.random.normal(k, shape, f32) * (gain * fan_in ** -0.5)

    dt = jnp.exp(jax.random.uniform(ks[10], (L, B_HEADS), f32,
                                    minval=math.log(1e-3), maxval=math.log(1e-1)))
    return {
        'x': jax.random.normal(ks[0], (BATCH, SEQ, D_MODEL), f32),
        'c': jax.random.normal(ks[1], (BATCH, D_MODEL), f32),
        'w_ada': nrm(ks[2], (L, D_MODEL, 6 * D_MODEL), D_MODEL, 0.5),
        'b_ada': 0.02 * jax.random.normal(ks[3], (L, 6 * D_MODEL), f32),
        'w_in': nrm(ks[4], (L, D_MODEL, IN_WIDTH), D_MODEL),
        'q_gain': 1.0 + 0.02 * jax.random.normal(ks[5], (L, A_LAT), f32),
        'k_gain': 1.0 + 0.02 * jax.random.normal(ks[6], (L, A_LAT), f32),
        'w_uv': nrm(ks[7], (L, A_HEADS, A_LAT, A_VDIM), A_LAT),
        'conv_w': nrm(ks[8], (L, CONV_W, 3 * B_WIDTH), CONV_W),
        'a_log': jnp.log(jax.random.uniform(ks[9], (L, B_HEADS), f32, minval=1.0, maxval=16.0)),
        'dt_bias': dt + jnp.log(-jnp.expm1(-dt)),
        'gdn_gain': 1.0 + 0.02 * jax.random.normal(ks[11], (L, B_DIM), f32),
        'w_out': nrm(ks[12], (L, MIX_WIDTH, D_MODEL), MIX_WIDTH),
        'w_query': nrm(ks[13], (L, D_MODEL, PEER_HEADS * PEER_QDIM), D_MODEL),
        'sub_keys': nrm(ks[14], (L, PEER_HEADS, 2, PEER_NKEYS, PEER_QDIM // 2), PEER_QDIM // 2),
        'u_tab': nrm(ks[15], (L, N_EXPERTS, D_MODEL), D_MODEL),
        'v_tab': nrm(ks[16], (L, N_EXPERTS, D_MODEL), D_MODEL),
    }


def reference(x, c, w_ada, b_ada, w_in, q_gain, k_gain, w_uv, conv_w, a_log, dt_bias,
              gdn_gain, w_out, w_query, sub_keys, u_tab, v_tab):
    B, S, _ = x.shape
    offsets = [int(o) for o in np.cumsum(IN_SPLITS)[:-1]]
    for l in range(DEPTH):
        sh1, sc1, g1, sh2, sc2, g2 = jnp.split(jax.nn.silu(c) @ w_ada[l] + b_ada[l], 6, axis=-1)
        h = modulate(rms(x), sh1, sc1).astype(x.dtype)
        proj = h @ w_in[l]
        aq, akv, iq, ik, iw, bq, bk, bv, bz, ba, bb = jnp.split(proj, offsets, axis=-1)
        ya = dsa_attention(aq.reshape(B, S, A_HEADS, A_LAT), akv,
                           iq.reshape(B, S, IDX_HEADS, IDX_DIM), ik, iw,
                           q_gain[l], k_gain[l], w_uv[l])
        yb = gated_deltanet(bq, bk, bv, bz, ba, bb, conv_w[l], a_log[l], dt_bias[l], gdn_gain[l])
        mix = jnp.concatenate([ya, yb], axis=-1).astype(x.dtype) @ w_out[l]
        x = x + (g1[:, None, :] * mix).astype(x.dtype)
        h = modulate(rms(x), sh2, sc2).astype(x.dtype)
        y = peer(h, w_query[l], sub_keys[l], u_tab[l], v_tab[l])
        x = x + (g2[:, None, :] * y).astype(x.dtype)
    return x
```

```python
import math
import jax
import jax.numpy as jnp
from jax import lax
import numpy as np
from jax.experimental import pallas as pl
from jax.experimental.pallas import tpu as pltpu

D_MODEL = 1024
BATCH = 8
SEQ = 4096
DEPTH = 1

EPS = 1e-6
A_HEADS = 8
A_LAT = 128
A_VDIM = 64
IDX_HEADS = 4
IDX_DIM = 64
IDX_TOPK_MAX = 256
Q_BLOCK = 128
B_HEADS = 4
B_DIM = 128
CONV_W = 4
CHUNK = 64
PEER_HEADS = 8
PEER_NKEYS = 128
PEER_QDIM = 256
PEER_TOPK = 16
PEER_BLOCK = 128
N_EXPERTS = PEER_NKEYS * PEER_NKEYS

A_WIDTH = A_HEADS * A_VDIM
B_WIDTH = B_HEADS * B_DIM
MIX_WIDTH = A_WIDTH + B_WIDTH
IN_SPLITS = (A_HEADS * A_LAT, A_LAT, IDX_HEADS * IDX_DIM, IDX_DIM, IDX_HEADS,
             B_WIDTH, B_WIDTH, B_WIDTH, B_WIDTH, B_HEADS, B_HEADS)
IN_WIDTH = sum(IN_SPLITS)


def rms(x):
    xf = x.astype(jnp.float32)
    return xf * lax.rsqrt(jnp.mean(xf * xf, -1, keepdims=True) + EPS)


def l2n(x):
    return x * lax.rsqrt(jnp.sum(x * x, -1, keepdims=True) + EPS)


def modulate(xn, shift, scale):
    return xn * (1.0 + scale[:, None, :]) + shift[:, None, :]


def causal_dwconv(x, w):
    C = x.shape[-1]
    return lax.conv_general_dilated(
        x, w[:, None, :].astype(x.dtype), window_strides=(1,),
        padding=[(w.shape[0] - 1, 0)], dimension_numbers=('NWC', 'WIO', 'NWC'),
        feature_group_count=C)


def dsa_attention(q_lat, kv, iq, ik, iw, q_gain, k_gain, w_uv):
    B, S = kv.shape[:2]
    topk = min(IDX_TOPK_MAX, S // 4)
    q = rms(q_lat) * q_gain
    k = rms(kv) * k_gain
    slopes = jnp.exp2(-8.0 * jnp.arange(1, A_HEADS + 1, dtype=jnp.float32) / A_HEADS)
    nb = S // Q_BLOCK
    key_pos = jnp.arange(S)
    scale = A_LAT ** -0.5
    idx_scale = (IDX_DIM ** -0.5) * (IDX_HEADS ** -0.5)

    def blocks(a):
        return a.reshape((B, nb, Q_BLOCK) + a.shape[2:]).swapaxes(0, 1)

    def one_block(args):
        qb, iqb, iwb, start = args
        qpos = start + jnp.arange(Q_BLOCK)
        causal = key_pos[None, :] <= qpos[:, None]
        logits = jnp.einsum('bqhd,bsd->bqhs', iqb, ik)
        score = jnp.einsum('bqhs,bqh->bqs', jax.nn.relu(logits), iwb).astype(jnp.float32) * idx_scale
        score = jnp.where(causal[None], score, -jnp.inf)
        _, sel = lax.top_k(score, topk)
        kv_sel = jax.vmap(lambda kb, ib: kb[ib])(k, sel)
        s = jnp.einsum('bqhr,bqkr->bqhk', qb, kv_sel) * scale
        dist = (qpos[None, :, None] - sel).astype(jnp.float32)
        s = s - slopes[None, None, :, None] * dist[:, :, None, :]
        valid = (sel <= qpos[None, :, None])[:, :, None, :]
        p = jax.nn.softmax(jnp.where(valid, s, -jnp.inf), axis=-1)
        return jnp.einsum('bqhk,bqkr->bqhr', p, kv_sel)

    starts = jnp.arange(nb, dtype=jnp.int32) * Q_BLOCK
    o = lax.map(one_block, (blocks(q), blocks(iq), blocks(iw), starts))
    o = o.swapaxes(0, 1).reshape(B, S, A_HEADS, A_LAT)
    return jnp.einsum('bshr,hrv->bshv', o, w_uv).reshape(B, S, A_WIDTH)


def gated_deltanet(q, k, v, z, a, b, conv_w, a_log, dt_bias, norm_gain):
    B, S = q.shape[:2]
    qkv = jax.nn.silu(causal_dwconv(jnp.concatenate([q, k, v], -1), conv_w)).astype(jnp.float32)
    q, k, v = jnp.split(qkv, 3, axis=-1)

    def heads(t):
        return t.reshape(B, S, B_HEADS, B_DIM).transpose(0, 2, 1, 3)

    q = l2n(heads(q)) * (B_DIM ** -0.5)
    k = l2n(heads(k))
    v = heads(v)
    beta = jax.nn.sigmoid(b.astype(jnp.float32)).transpose(0, 2, 1)
    g = (-jnp.exp(a_log.astype(jnp.float32))
         * jax.nn.softplus(a.astype(jnp.float32) + dt_bias.astype(jnp.float32))).transpose(0, 2, 1)
    n = S // CHUNK

    def chunks(t):
        return t.reshape(t.shape[:2] + (n, CHUNK) + t.shape[3:])

    q, k, v, beta, g = (chunks(t) for t in (q, k, v, beta, g))
    G = jnp.cumsum(g, axis=-1)
    tril = jnp.tril(jnp.ones((CHUNK, CHUNK), bool))
    strict = jnp.tril(jnp.ones((CHUNK, CHUNK), bool), -1)
    decay = jnp.exp(jnp.where(tril, G[..., :, None] - G[..., None, :], -jnp.inf))
    kb = k * beta[..., None]
    Lmat = jnp.where(strict, jnp.einsum('bhnid,bhnjd->bhnij', kb, k) * decay, 0.0)
    eye = jnp.eye(CHUNK, dtype=jnp.float32)
    T = lax.linalg.triangular_solve(eye + Lmat, jnp.broadcast_to(eye, Lmat.shape),
                                    left_side=True, lower=True, unit_diagonal=True)
    u = T @ (v * beta[..., None])
    w = T @ (kb * jnp.exp(G)[..., None])
    attn_intra = jnp.einsum('bhnid,bhnjd->bhnij', q, k) * decay
    q_dec = q * jnp.exp(G)[..., None]
    k_dec = k * jnp.exp(G[..., -1:] - G)[..., None]
    g_last = jnp.exp(G[..., -1])

    def step(state, xs):
        u_c, w_c, a_c, qd_c, kd_c, gl_c = xs
        v_new = u_c - jnp.einsum('bhck,bhkv->bhcv', w_c, state)
        o = jnp.einsum('bhck,bhkv->bhcv', qd_c, state) + jnp.einsum('bhij,bhjv->bhiv', a_c, v_new)
        state = state * gl_c[..., None, None] + jnp.einsum('bhck,bhcv->bhkv', kd_c, v_new)
        return state, o

    xs = tuple(jnp.moveaxis(t, 2, 0) for t in (u, w, attn_intra, q_dec, k_dec, g_last))
    state0 = jnp.zeros((B, B_HEADS, B_DIM, B_DIM), jnp.float32)
    _, o = lax.scan(step, state0, xs)
    o = jnp.moveaxis(o, 0, 2).reshape(B, B_HEADS, S, B_DIM).transpose(0, 2, 1, 3)
    zh = z.reshape(B, S, B_HEADS, B_DIM).astype(jnp.float32)
    o = rms(o) * norm_gain * jax.nn.silu(zh)
    return o.reshape(B, S, B_WIDTH)


def peer(h, w_query, sub_keys, u_tab, v_tab):
    B, S, D = h.shape
    hb_all = h.reshape(-1, PEER_BLOCK, D)

    def one_block(hb):
        q = (hb @ w_query).reshape(PEER_BLOCK, PEER_HEADS, 2, PEER_QDIM // 2)
        s = jnp.einsum('thpd,hpnd->thpn', q, sub_keys).astype(jnp.float32)
        s_half, i_half = lax.top_k(s, PEER_TOPK)
        cand = s_half[..., 0, :, None] + s_half[..., 1, None, :]
        cand_s, cand_i = lax.top_k(cand.reshape(PEER_BLOCK, PEER_HEADS, PEER_TOPK * PEER_TOPK), PEER_TOPK)
        i1 = jnp.take_along_axis(i_half[..., 0, :], cand_i // PEER_TOPK, axis=-1)
        i2 = jnp.take_along_axis(i_half[..., 1, :], cand_i % PEER_TOPK, axis=-1)
        expert = i1 * PEER_NKEYS + i2
        gate = jax.nn.softmax(cand_s, axis=-1)
        act = jax.nn.gelu(jnp.einsum('tpkd,td->tpk', u_tab[expert], hb).astype(jnp.float32),
                          approximate=False) * gate
        return jnp.einsum('tpk,tpkd->td', act.astype(v_tab.dtype), v_tab[expert])

    return lax.map(one_block, hb_all).reshape(B, S, D)


def _mm_kernel(a_ref, b_ref, o_ref):
    o_ref[...] = jnp.dot(a_ref[...].astype(jnp.bfloat16), b_ref[...].astype(jnp.bfloat16),
                         preferred_element_type=jnp.float32)


def pallas_matmul(a, b, tm=512, tn=512):
    M, K = a.shape
    N = b.shape[1]
    Np = -(-N // tn) * tn
    if Np != N:
        b = jnp.pad(b, ((0, 0), (0, Np - N)))
    out = pl.pallas_call(
        _mm_kernel,
        grid=(M // tm, Np // tn),
        in_specs=[pl.BlockSpec((tm, K), lambda i, j: (i, 0)),
                  pl.BlockSpec((K, tn), lambda i, j: (0, j))],
        out_specs=pl.BlockSpec((tm, tn), lambda i, j: (i, j)),
        out_shape=jax.ShapeDtypeStruct((M, Np), jnp.float32),
    )(a, b)
    return out[:, :N]


def kernel(x, c, w_ada, b_ada, w_in, q_gain, k_gain, w_uv, conv_w, a_log, dt_bias,
           gdn_gain, w_out, w_query, sub_keys, u_tab, v_tab):
    B, S, _ = x.shape
    offsets = [int(o) for o in np.cumsum(IN_SPLITS)[:-1]]
    for l in range(DEPTH):
        sh1, sc1, g1, sh2, sc2, g2 = jnp.split(jax.nn.silu(c) @ w_ada[l] + b_ada[l], 6, axis=-1)
        h = modulate(rms(x), sh1, sc1).astype(x.dtype)
        proj = pallas_matmul(h.reshape(B * S, -1), w_in[l]).reshape(B, S, -1)
        aq, akv, iq, ik, iw, bq, bk, bv, bz, ba, bb = jnp.split(proj, offsets, axis=-1)
        ya = dsa_attention(aq.reshape(B, S, A_HEADS, A_LAT), akv,
                           iq.reshape(B, S, IDX_HEADS, IDX_DIM), ik, iw,
                           q_gain[l], k_gain[l], w_uv[l])
        yb = gated_deltanet(bq, bk, bv, bz, ba, bb, conv_w[l], a_log[l], dt_bias[l], gdn_gain[l])
        mix = pallas_matmul(jnp.concatenate([ya, yb], axis=-1).reshape(B * S, -1), w_out[l]).reshape(B, S, -1)
        x = x + (g1[:, None, :] * mix).astype(x.dtype)
        h = modulate(rms(x), sh2, sc2).astype(x.dtype)
        y = peer(h, w_query[l], sub_keys[l], u_tab[l], v_tab[l])
        x = x + (g2[:, None, :] * y).astype(x.dtype)
    return x
```

```python
import functools
import math
import jax
import jax.numpy as jnp
from jax import lax
import numpy as np
from jax.experimental import pallas as pl
from jax.experimental.pallas import tpu as pltpu

D_MODEL = 1024
BATCH = 8
SEQ = 4096
DEPTH = 1

EPS = 1e-6
A_HEADS = 8
A_LAT = 128
A_VDIM = 64
IDX_HEADS = 4
IDX_DIM = 64
IDX_TOPK_MAX = 256
Q_BLOCK = 128
B_HEADS = 4
B_DIM = 128
CONV_W = 4
CHUNK = 64
PEER_HEADS = 8
PEER_NKEYS = 128
PEER_QDIM = 256
PEER_TOPK = 16
PEER_BLOCK = 128
N_EXPERTS = PEER_NKEYS * PEER_NKEYS

A_WIDTH = A_HEADS * A_VDIM
B_WIDTH = B_HEADS * B_DIM
MIX_WIDTH = A_WIDTH + B_WIDTH
IN_SPLITS = (A_HEADS * A_LAT, A_LAT, IDX_HEADS * IDX_DIM, IDX_DIM, IDX_HEADS,
             B_WIDTH, B_WIDTH, B_WIDTH, B_WIDTH, B_HEADS, B_HEADS)
IN_WIDTH = sum(IN_SPLITS)


def rms(x):
    xf = x.astype(jnp.float32)
    return xf * lax.rsqrt(jnp.mean(xf * xf, -1, keepdims=True) + EPS)


def l2n(x):
    return x * lax.rsqrt(jnp.sum(x * x, -1, keepdims=True) + EPS)


def modulate(xn, shift, scale):
    return xn * (1.0 + scale[:, None, :]) + shift[:, None, :]


def causal_dwconv(x, w):
    C = x.shape[-1]
    return lax.conv_general_dilated(
        x, w[:, None, :].astype(x.dtype), window_strides=(1,),
        padding=[(w.shape[0] - 1, 0)], dimension_numbers=('NWC', 'WIO', 'NWC'),
        feature_group_count=C)


f32 = jnp.float32
bf16 = jnp.bfloat16
INT_MIN = -2 ** 31


def _dsa_kernel(q_ref, kv_ref, iq_ref, ik_ref, iw_ref, qg_ref, kg_ref, wbd_ref, tri_ref, o_ref,
                kaug_scr, ikb_scr, key_scr, qaug_scr, m_scr, l_scr, acc_scr, *, S, TQ, TK, topk):
    qi = pl.program_id(1)
    scale = A_LAT ** -0.5
    idx_scale = (IDX_DIM ** -0.5) * (IDX_HEADS ** -0.5)

    @pl.when(qi == 0)
    def _prep_keys():
        kv = kv_ref[0]
        kn = kv * lax.rsqrt(jnp.mean(kv * kv, -1, keepdims=True) + EPS) * kg_ref[...]
        pos = lax.broadcasted_iota(jnp.int32, (S, A_LAT), 0)
        lane = lax.broadcasted_iota(jnp.int32, (S, A_LAT), 1)
        hi = (pos >> 6).astype(f32)
        lo = (pos & 63).astype(f32)
        extra = jnp.where(lane == 0, hi, jnp.where(lane == 1, lo, 0.0))
        kaug_scr[:, :A_LAT] = kn.astype(bf16)
        kaug_scr[:, A_LAT:] = extra.astype(bf16)
        ikb_scr[...] = ik_ref[0].astype(bf16)

    q = q_ref[0]
    lane = lax.broadcasted_iota(jnp.int32, (TQ, A_LAT), 1)
    for h in range(A_HEADS):
        qh = q[:, h * A_LAT:(h + 1) * A_LAT]
        qn = qh * lax.rsqrt(jnp.mean(qh * qh, -1, keepdims=True) + EPS) * qg_ref[...] * scale
        slope = 2.0 ** (-8.0 * (h + 1) / A_HEADS)
        extra = jnp.where(lane == 0, slope * 64.0, jnp.where(lane == 1, slope, 0.0))
        qaug_scr[h, :, :A_LAT] = qn.astype(bf16)
        qaug_scr[h, :, A_LAT:] = extra.astype(bf16)

    nchunks = (qi * TQ + TQ + TK - 1) // TK
    qpos = qi * TQ + lax.broadcasted_iota(jnp.int32, (TQ, 1), 0)
    colb = lax.broadcasted_iota(jnp.int32, (TQ, TK), 1)
    iq = iq_ref[0]
    iqb = [iq[:, h * IDX_DIM:(h + 1) * IDX_DIM].astype(bf16) for h in range(IDX_HEADS)]
    iw = iw_ref[0]

    def score_chunk(c, carry):
        off = pl.multiple_of(c * TK, TK)
        ikc = ikb_scr[pl.ds(off, TK), :]
        acc = jnp.zeros((TQ, TK), f32)
        for h in range(IDX_HEADS):
            lg = lax.dot_general(iqb[h], ikc, (((1,), (1,)), ((), ())), preferred_element_type=f32)
            acc = acc + jnp.maximum(lg, 0.0) * iw[:, h:h + 1]
        sc = acc * idx_scale
        sc = jnp.where(colb + off <= qpos, sc, -jnp.inf)
        bits = pltpu.bitcast(sc, jnp.int32)
        key_scr[:, pl.ds(off, TK)] = jnp.where(bits < 0, bits ^ jnp.int32(0x7FFFFFFF), bits)
        return carry

    lax.fori_loop(0, nchunks, score_chunk, 0)

    def count(pred_fn):
        def body(c, cnt):
            off = pl.multiple_of(c * TK, TK)
            hit = pred_fn(key_scr[:, pl.ds(off, TK)]).astype(jnp.int32)
            part = hit[:, 0:128]
            for j in range(1, TK // 128):
                part = part + hit[:, j * 128:(j + 1) * 128]
            return cnt + part
        cnt = lax.fori_loop(0, nchunks, body, jnp.zeros((TQ, 128), jnp.int32))
        return jnp.sum(cnt, axis=1, keepdims=True)

    def bit_step(i, T):
        cand = T + lax.shift_left(jnp.int32(1), jnp.int32(31) - i)
        cnt = count(lambda k: k >= cand)
        return jnp.where(cnt >= topk, cand, T)

    T = lax.fori_loop(0, 32, bit_step, jnp.full((TQ, 1), INT_MIN, jnp.int32))
    n_gt = count(lambda k: k > T)
    room = (topk - n_gt).astype(f32)

    m_scr[...] = jnp.full(m_scr.shape, -jnp.inf, f32)
    l_scr[...] = jnp.zeros(l_scr.shape, f32)
    acc_scr[...] = jnp.zeros(acc_scr.shape, f32)

    def attend_chunk(c, ties_before):
        off = pl.multiple_of(c * TK, TK)
        keyc = key_scr[:, pl.ds(off, TK)]
        eq = keyc == T
        eqf = jnp.where(eq, 1.0, 0.0)
        pref = jnp.dot(eqf.astype(bf16), tri_ref[...], preferred_element_type=f32)
        sel = (keyc > T) | (eq & (pref + ties_before < room))
        sel = sel & (colb + off <= qpos)
        kc = kaug_scr[pl.ds(off, TK), :]
        vc = kc[:, :A_LAT]
        for h in range(A_HEADS):
            s = lax.dot_general(qaug_scr[h], kc, (((1,), (1,)), ((), ())), preferred_element_type=f32)
            s = jnp.where(sel, s, -jnp.inf)
            m_old = m_scr[h]
            m_new = jnp.maximum(m_old, jnp.max(s, axis=1, keepdims=True))
            m_safe = jnp.where(m_new == -jnp.inf, 0.0, m_new)
            p = jnp.exp(s - m_safe)
            alpha = jnp.exp(m_old - m_safe)
            l_scr[h] = alpha * l_scr[h] + jnp.sum(p, axis=1, keepdims=True)
            acc_scr[h] = alpha * acc_scr[h] + jnp.dot(p.astype(bf16), vc, preferred_element_type=f32)
            m_scr[h] = m_new
        return ties_before + jnp.sum(eqf, axis=1, keepdims=True)

    lax.fori_loop(0, nchunks, attend_chunk, jnp.zeros((TQ, 1), f32))

    o = jnp.concatenate([(acc_scr[h] / l_scr[h]).astype(bf16) for h in range(A_HEADS)], axis=1)
    o_ref[0] = jnp.dot(o, wbd_ref[...], preferred_element_type=f32)


def dsa_attention(q_lat, kv, iq, ik, iw, q_gain, k_gain, w_uv, *, TQ=128, TK=512):
    B, S, _ = q_lat.shape
    topk = min(IDX_TOPK_MAX, S // 4)
    TK = min(TK, S)
    wbd = jnp.zeros((A_HEADS * A_LAT, A_WIDTH), f32)
    for h in range(A_HEADS):
        wbd = wbd.at[h * A_LAT:(h + 1) * A_LAT, h * A_VDIM:(h + 1) * A_VDIM].set(w_uv[h])
    tri = (jnp.arange(TK)[:, None] < jnp.arange(TK)[None, :]).astype(bf16)
    kern = functools.partial(_dsa_kernel, S=S, TQ=TQ, TK=TK, topk=topk)
    return pl.pallas_call(
        kern,
        grid=(B, S // TQ),
        in_specs=[
            pl.BlockSpec((1, TQ, A_HEADS * A_LAT), lambda b, i: (b, i, 0)),
            pl.BlockSpec((1, S, A_LAT), lambda b, i: (b, 0, 0)),
            pl.BlockSpec((1, TQ, IDX_HEADS * IDX_DIM), lambda b, i: (b, i, 0)),
            pl.BlockSpec((1, S, IDX_DIM), lambda b, i: (b, 0, 0)),
            pl.BlockSpec((1, TQ, IDX_HEADS), lambda b, i: (b, i, 0)),
            pl.BlockSpec((1, A_LAT), lambda b, i: (0, 0)),
            pl.BlockSpec((1, A_LAT), lambda b, i: (0, 0)),
            pl.BlockSpec((A_HEADS * A_LAT, A_WIDTH), lambda b, i: (0, 0)),
            pl.BlockSpec((TK, TK), lambda b, i: (0, 0)),
        ],
        out_specs=pl.BlockSpec((1, TQ, A_WIDTH), lambda b, i: (b, i, 0)),
        out_shape=jax.ShapeDtypeStruct((B, S, A_WIDTH), f32),
        scratch_shapes=[
            pltpu.VMEM((S, 2 * A_LAT), bf16),
            pltpu.VMEM((S, IDX_DIM), bf16),
            pltpu.VMEM((TQ, S), jnp.int32),
            pltpu.VMEM((A_HEADS, TQ, 2 * A_LAT), bf16),
            pltpu.VMEM((A_HEADS, TQ, 1), f32),
            pltpu.VMEM((A_HEADS, TQ, 1), f32),
            pltpu.VMEM((A_HEADS, TQ, A_LAT), f32),
        ],
        compiler_params=pltpu.CompilerParams(dimension_semantics=("arbitrary", "arbitrary")),
        name="dsa_attention",
    )(q_lat, kv, iq, ik, iw, q_gain.reshape(1, -1), k_gain.reshape(1, -1), wbd.astype(bf16), tri)


def gated_deltanet(q, k, v, z, a, b, conv_w, a_log, dt_bias, norm_gain):
    B, S = q.shape[:2]
    qkv = jax.nn.silu(causal_dwconv(jnp.concatenate([q, k, v], -1), conv_w)).astype(jnp.float32)
    q, k, v = jnp.split(qkv, 3, axis=-1)

    def heads(t):
        return t.reshape(B, S, B_HEADS, B_DIM).transpose(0, 2, 1, 3)

    q = l2n(heads(q)) * (B_DIM ** -0.5)
    k = l2n(heads(k))
    v = heads(v)
    beta = jax.nn.sigmoid(b.astype(jnp.float32)).transpose(0, 2, 1)
    g = (-jnp.exp(a_log.astype(jnp.float32))
         * jax.nn.softplus(a.astype(jnp.float32) + dt_bias.astype(jnp.float32))).transpose(0, 2, 1)
    n = S // CHUNK

    def chunks(t):
        return t.reshape(t.shape[:2] + (n, CHUNK) + t.shape[3:])

    q, k, v, beta, g = (chunks(t) for t in (q, k, v, beta, g))
    G = jnp.cumsum(g, axis=-1)
    tril = jnp.tril(jnp.ones((CHUNK, CHUNK), bool))
    strict = jnp.tril(jnp.ones((CHUNK, CHUNK), bool), -1)
    decay = jnp.exp(jnp.where(tril, G[..., :, None] - G[..., None, :], -jnp.inf))
    kb = k * beta[..., None]
    Lmat = jnp.where(strict, jnp.einsum('bhnid,bhnjd->bhnij', kb, k) * decay, 0.0)
    eye = jnp.eye(CHUNK, dtype=jnp.float32)
    T = lax.linalg.triangular_solve(eye + Lmat, jnp.broadcast_to(eye, Lmat.shape),
                                    left_side=True, lower=True, unit_diagonal=True)
    u = T @ (v * beta[..., None])
    w = T @ (kb * jnp.exp(G)[..., None])
    attn_intra = jnp.einsum('bhnid,bhnjd->bhnij', q, k) * decay
    q_dec = q * jnp.exp(G)[..., None]
    k_dec = k * jnp.exp(G[..., -1:] - G)[..., None]
    g_last = jnp.exp(G[..., -1])

    def step(state, xs):
        u_c, w_c, a_c, qd_c, kd_c, gl_c = xs
        v_new = u_c - jnp.einsum('bhck,bhkv->bhcv', w_c, state)
        o = jnp.einsum('bhck,bhkv->bhcv', qd_c, state) + jnp.einsum('bhij,bhjv->bhiv', a_c, v_new)
        state = state * gl_c[..., None, None] + jnp.einsum('bhck,bhcv->bhkv', kd_c, v_new)
        return state, o

    xs = tuple(jnp.moveaxis(t, 2, 0) for t in (u, w, attn_intra, q_dec, k_dec, g_last))
    state0 = jnp.zeros((B, B_HEADS, B_DIM, B_DIM), jnp.float32)
    _, o = lax.scan(step, state0, xs)
    o = jnp.moveaxis(o, 0, 2).reshape(B, B_HEADS, S, B_DIM).transpose(0, 2, 1, 3)
    zh = z.reshape(B, S, B_HEADS, B_DIM).astype(jnp.float32)
    o = rms(o) * norm_gain * jax.nn.silu(zh)
    return o.reshape(B, S, B_WIDTH)


def peer(h, w_query, sub_keys, u_tab, v_tab):
    B, S, D = h.shape
    hb_all = h.reshape(-1, PEER_BLOCK, D)

    def one_block(hb):
        q = (hb @ w_query).reshape(PEER_BLOCK, PEER_HEADS, 2, PEER_QDIM // 2)
        s = jnp.einsum('thpd,hpnd->thpn', q, sub_keys).astype(jnp.float32)
        s_half, i_half = lax.top_k(s, PEER_TOPK)
        cand = s_half[..., 0, :, None] + s_half[..., 1, None, :]
        cand_s, cand_i = lax.top_k(cand.reshape(PEER_BLOCK, PEER_HEADS, PEER_TOPK * PEER_TOPK), PEER_TOPK)
        i1 = jnp.take_along_axis(i_half[..., 0, :], cand_i // PEER_TOPK, axis=-1)
        i2 = jnp.take_along_axis(i_half[..., 1, :], cand_i % PEER_TOPK, axis=-1)
        expert = i1 * PEER_NKEYS + i2
        gate = jax.nn.softmax(cand_s, axis=-1)
        act = jax.nn.gelu(jnp.einsum('tpkd,td->tpk', u_tab[expert], hb).astype(jnp.float32),
                          approximate=False) * gate
        return jnp.einsum('tpk,tpkd->td', act.astype(v_tab.dtype), v_tab[expert])

    return lax.map(one_block, hb_all).reshape(B, S, D)


def _mm_kernel(a_ref, b_ref, o_ref):
    o_ref[...] = jnp.dot(a_ref[...].astype(jnp.bfloat16), b_ref[...].astype(jnp.bfloat16),
                         preferred_element_type=jnp.float32)


def pallas_matmul(a, b, tm=512, tn=512):
    M, K = a.shape
    N = b.shape[1]
    Np = -(-N // tn) * tn
    if Np != N:
        b = jnp.pad(b, ((0, 0), (0, Np - N)))
    out = pl.pallas_call(
        _mm_kernel,
        grid=(M // tm, Np // tn),
        in_specs=[pl.BlockSpec((tm, K), lambda i, j: (i, 0)),
                  pl.BlockSpec((K, tn), lambda i, j: (0, j))],
        out_specs=pl.BlockSpec((tm, tn), lambda i, j: (i, j)),
        out_shape=jax.ShapeDtypeStruct((M, Np), jnp.float32),
    )(a, b)
    return out[:, :N]


def kernel(x, c, w_ada, b_ada, w_in, q_gain, k_gain, w_uv, conv_w, a_log, dt_bias,
           gdn_gain, w_out, w_query, sub_keys, u_tab, v_tab):
    B, S, _ = x.shape
    offsets = [int(o) for o in np.cumsum(IN_SPLITS)[:-1]]
    for l in range(DEPTH):
        sh1, sc1, g1, sh2, sc2, g2 = jnp.split(jax.nn.silu(c) @ w_ada[l] + b_ada[l], 6, axis=-1)
        h = modulate(rms(x), sh1, sc1).astype(x.dtype)
        proj = pallas_matmul(h.reshape(B * S, -1), w_in[l]).reshape(B, S, -1)
        aq, akv, iq, ik, iw, bq, bk, bv, bz, ba, bb = jnp.split(proj, offsets, axis=-1)
        ya = dsa_attention(aq, akv, iq, ik, iw, q_gain[l], k_gain[l], w_uv[l])
        yb = gated_deltanet(bq, bk, bv, bz, ba, bb, conv_w[l], a_log[l], dt_bias[l], gdn_gain[l])
        mix = pallas_matmul(jnp.concatenate([ya, yb], axis=-1).reshape(B * S, -1), w_out[l]).reshape(B, S, -1)
        x = x + (g1[:, None, :] * mix).astype(x.dtype)
        h = modulate(rms(x), sh2, sc2).astype(x.dtype)
        y = peer(h, w_query[l], sub_keys[l], u_tab[l], v_tab[l])
        x = x + (g2[:, None, :] * y).astype(x.dtype)
    return x
```

```python
import functools
import math
import jax
import jax.numpy as jnp
from jax import lax
import numpy as np
from jax.experimental import pallas as pl
from jax.experimental.pallas import tpu as pltpu

D_MODEL = 1024
BATCH = 8
SEQ = 4096
DEPTH = 1

EPS = 1e-6
A_HEADS = 8
A_LAT = 128
A_VDIM = 64
IDX_HEADS = 4
IDX_DIM = 64
IDX_TOPK_MAX = 256
Q_BLOCK = 128
B_HEADS = 4
B_DIM = 128
CONV_W = 4
CHUNK = 64
PEER_HEADS = 8
PEER_NKEYS = 128
PEER_QDIM = 256
PEER_TOPK = 16
PEER_BLOCK = 128
N_EXPERTS = PEER_NKEYS * PEER_NKEYS

A_WIDTH = A_HEADS * A_VDIM
B_WIDTH = B_HEADS * B_DIM
MIX_WIDTH = A_WIDTH + B_WIDTH
IN_SPLITS = (A_HEADS * A_LAT, A_LAT, IDX_HEADS * IDX_DIM, IDX_DIM, IDX_HEADS,
             B_WIDTH, B_WIDTH, B_WIDTH, B_WIDTH, B_HEADS, B_HEADS)
IN_WIDTH = sum(IN_SPLITS)


def rms(x):
    xf = x.astype(jnp.float32)
    return xf * lax.rsqrt(jnp.mean(xf * xf, -1, keepdims=True) + EPS)


def l2n(x):
    return x * lax.rsqrt(jnp.sum(x * x, -1, keepdims=True) + EPS)


def modulate(xn, shift, scale):
    return xn * (1.0 + scale[:, None, :]) + shift[:, None, :]


def causal_dwconv(x, w):
    C = x.shape[-1]
    return lax.conv_general_dilated(
        x, w[:, None, :].astype(x.dtype), window_strides=(1,),
        padding=[(w.shape[0] - 1, 0)], dimension_numbers=('NWC', 'WIO', 'NWC'),
        feature_group_count=C)


f32 = jnp.float32
bf16 = jnp.bfloat16
INT_MIN = -2 ** 31


def _dsa_kernel(q_ref, kv_ref, iq_ref, ik_ref, iw_ref, qg_ref, kg_ref, wbd_ref, tri_ref, o_ref,
                kaug_scr, ikb_scr, key_scr, qaug_scr, m_scr, l_scr, acc_scr, *, S, TQ, TK, topk):
    qi = pl.program_id(1)
    scale = A_LAT ** -0.5
    idx_scale = (IDX_DIM ** -0.5) * (IDX_HEADS ** -0.5)

    @pl.when(qi == 0)
    def _prep_keys():
        kv = kv_ref[0]
        kn = kv * lax.rsqrt(jnp.mean(kv * kv, -1, keepdims=True) + EPS) * kg_ref[...]
        pos = lax.broadcasted_iota(jnp.int32, (S, A_LAT), 0)
        lane = lax.broadcasted_iota(jnp.int32, (S, A_LAT), 1)
        hi = (pos >> 6).astype(f32)
        lo = (pos & 63).astype(f32)
        extra = jnp.where(lane == 0, hi, jnp.where(lane == 1, lo, 0.0))
        kaug_scr[:, :A_LAT] = kn.astype(bf16)
        kaug_scr[:, A_LAT:] = extra.astype(bf16)
        ikb_scr[...] = ik_ref[0].astype(bf16)

    q = q_ref[0]
    lane = lax.broadcasted_iota(jnp.int32, (TQ, A_LAT), 1)
    for h in range(A_HEADS):
        qh = q[:, h * A_LAT:(h + 1) * A_LAT]
        qn = qh * lax.rsqrt(jnp.mean(qh * qh, -1, keepdims=True) + EPS) * qg_ref[...] * scale
        slope = 2.0 ** (-8.0 * (h + 1) / A_HEADS)
        extra = jnp.where(lane == 0, slope * 64.0, jnp.where(lane == 1, slope, 0.0))
        qaug_scr[h, :, :A_LAT] = qn.astype(bf16)
        qaug_scr[h, :, A_LAT:] = extra.astype(bf16)

    nchunks = (qi * TQ + TQ + TK - 1) // TK
    qpos = qi * TQ + lax.broadcasted_iota(jnp.int32, (TQ, 1), 0)
    colb = lax.broadcasted_iota(jnp.int32, (TQ, TK), 1)
    iq = iq_ref[0]
    iqb = [iq[:, h * IDX_DIM:(h + 1) * IDX_DIM].astype(bf16) for h in range(IDX_HEADS)]
    iw = iw_ref[0]

    def score_chunk(c, carry):
        off = pl.multiple_of(c * TK, TK)
        ikc = ikb_scr[pl.ds(off, TK), :]
        acc = jnp.zeros((TQ, TK), f32)
        for h in range(IDX_HEADS):
            lg = lax.dot_general(iqb[h], ikc, (((1,), (1,)), ((), ())), preferred_element_type=f32)
            acc = acc + jnp.maximum(lg, 0.0) * iw[:, h:h + 1]
        sc = acc * idx_scale
        sc = jnp.where(colb + off <= qpos, sc, -jnp.inf)
        bits = pltpu.bitcast(sc, jnp.int32)
        key_scr[:, pl.ds(off, TK)] = jnp.where(bits < 0, bits ^ jnp.int32(0x7FFFFFFF), bits)
        return carry

    lax.fori_loop(0, nchunks, score_chunk, 0)

    def count(pred_fn):
        def body(c, cnt):
            off = pl.multiple_of(c * TK, TK)
            hit = pred_fn(key_scr[:, pl.ds(off, TK)]).astype(jnp.int32)
            part = hit[:, 0:128]
            for j in range(1, TK // 128):
                part = part + hit[:, j * 128:(j + 1) * 128]
            return cnt + part
        cnt = lax.fori_loop(0, nchunks, body, jnp.zeros((TQ, 128), jnp.int32))
        return jnp.sum(cnt, axis=1, keepdims=True)

    def bit_step(i, T):
        cand = T + lax.shift_left(jnp.int32(1), jnp.int32(31) - i)
        cnt = count(lambda k: k >= cand)
        return jnp.where(cnt >= topk, cand, T)

    T = lax.fori_loop(0, 32, bit_step, jnp.full((TQ, 1), INT_MIN, jnp.int32))
    n_gt = count(lambda k: k > T)
    room = (topk - n_gt).astype(f32)

    m_scr[...] = jnp.full(m_scr.shape, -jnp.inf, f32)
    l_scr[...] = jnp.zeros(l_scr.shape, f32)
    acc_scr[...] = jnp.zeros(acc_scr.shape, f32)

    def attend_chunk(c, ties_before):
        off = pl.multiple_of(c * TK, TK)
        keyc = key_scr[:, pl.ds(off, TK)]
        eq = keyc == T
        eqf = jnp.where(eq, 1.0, 0.0)
        pref = jnp.dot(eqf.astype(bf16), tri_ref[...], preferred_element_type=f32)
        sel = (keyc > T) | (eq & (pref + ties_before < room))
        sel = sel & (colb + off <= qpos)
        kc = kaug_scr[pl.ds(off, TK), :]
        vc = kc[:, :A_LAT]
        for h in range(A_HEADS):
            s = lax.dot_general(qaug_scr[h], kc, (((1,), (1,)), ((), ())), preferred_element_type=f32)
            s = jnp.where(sel, s, -jnp.inf)
            m_old = m_scr[h]
            m_new = jnp.maximum(m_old, jnp.max(s, axis=1, keepdims=True))
            m_safe = jnp.where(m_new == -jnp.inf, 0.0, m_new)
            p = jnp.exp(s - m_safe)
            alpha = jnp.exp(m_old - m_safe)
            l_scr[h] = alpha * l_scr[h] + jnp.sum(p, axis=1, keepdims=True)
            acc_scr[h] = alpha * acc_scr[h] + jnp.dot(p.astype(bf16), vc, preferred_element_type=f32)
            m_scr[h] = m_new
        return ties_before + jnp.sum(eqf, axis=1, keepdims=True)

    lax.fori_loop(0, nchunks, attend_chunk, jnp.zeros((TQ, 1), f32))

    o = jnp.concatenate([(acc_scr[h] / l_scr[h]).astype(bf16) for h in range(A_HEADS)], axis=1)
    o_ref[0] = jnp.dot(o, wbd_ref[...], preferred_element_type=f32)


def dsa_attention(q_lat, kv, iq, ik, iw, q_gain, k_gain, w_uv, *, TQ=128, TK=512):
    B, S, _ = q_lat.shape
    topk = min(IDX_TOPK_MAX, S // 4)
    TK = min(TK, S)
    wbd = jnp.zeros((A_HEADS * A_LAT, A_WIDTH), f32)
    for h in range(A_HEADS):
        wbd = wbd.at[h * A_LAT:(h + 1) * A_LAT, h * A_VDIM:(h + 1) * A_VDIM].set(w_uv[h])
    tri = (jnp.arange(TK)[:, None] < jnp.arange(TK)[None, :]).astype(bf16)
    kern = functools.partial(_dsa_kernel, S=S, TQ=TQ, TK=TK, topk=topk)
    return pl.pallas_call(
        kern,
        grid=(B, S // TQ),
        in_specs=[
            pl.BlockSpec((1, TQ, A_HEADS * A_LAT), lambda b, i: (b, i, 0)),
            pl.BlockSpec((1, S, A_LAT), lambda b, i: (b, 0, 0)),
            pl.BlockSpec((1, TQ, IDX_HEADS * IDX_DIM), lambda b, i: (b, i, 0)),
            pl.BlockSpec((1, S, IDX_DIM), lambda b, i: (b, 0, 0)),
            pl.BlockSpec((1, TQ, IDX_HEADS), lambda b, i: (b, i, 0)),
            pl.BlockSpec((1, A_LAT), lambda b, i: (0, 0)),
            pl.BlockSpec((1, A_LAT), lambda b, i: (0, 0)),
            pl.BlockSpec((A_HEADS * A_LAT, A_WIDTH), lambda b, i: (0, 0)),
            pl.BlockSpec((TK, TK), lambda b, i: (0, 0)),
        ],
        out_specs=pl.BlockSpec((1, TQ, A_WIDTH), lambda b, i: (b, i, 0)),
        out_shape=jax.ShapeDtypeStruct((B, S, A_WIDTH), f32),
        scratch_shapes=[
            pltpu.VMEM((S, 2 * A_LAT), bf16),
            pltpu.VMEM((S, IDX_DIM), bf16),
            pltpu.VMEM((TQ, S), jnp.int32),
            pltpu.VMEM((A_HEADS, TQ, 2 * A_LAT), bf16),
            pltpu.VMEM((A_HEADS, TQ, 1), f32),
            pltpu.VMEM((A_HEADS, TQ, 1), f32),
            pltpu.VMEM((A_HEADS, TQ, A_LAT), f32),
        ],
        compiler_params=pltpu.CompilerParams(dimension_semantics=("arbitrary", "arbitrary")),
        name="dsa_attention",
    )(q_lat, kv, iq, ik, iw, q_gain.reshape(1, -1), k_gain.reshape(1, -1), wbd.astype(bf16), tri)


def gated_deltanet(q, k, v, z, a, b, conv_w, a_log, dt_bias, norm_gain):
    B, S = q.shape[:2]
    qkv = jax.nn.silu(causal_dwconv(jnp.concatenate([q, k, v], -1), conv_w)).astype(jnp.float32)
    q, k, v = jnp.split(qkv, 3, axis=-1)

    def heads(t):
        return t.reshape(B, S, B_HEADS, B_DIM).transpose(0, 2, 1, 3)

    q = l2n(heads(q)) * (B_DIM ** -0.5)
    k = l2n(heads(k))
    v = heads(v)
    beta = jax.nn.sigmoid(b.astype(jnp.float32)).transpose(0, 2, 1)
    g = (-jnp.exp(a_log.astype(jnp.float32))
         * jax.nn.softplus(a.astype(jnp.float32) + dt_bias.astype(jnp.float32))).transpose(0, 2, 1)
    n = S // CHUNK

    def chunks(t):
        return t.reshape(t.shape[:2] + (n, CHUNK) + t.shape[3:])

    q, k, v, beta, g = (chunks(t) for t in (q, k, v, beta, g))
    G = jnp.cumsum(g, axis=-1)
    tril = jnp.tril(jnp.ones((CHUNK, CHUNK), bool))
    strict = jnp.tril(jnp.ones((CHUNK, CHUNK), bool), -1)
    decay = jnp.exp(jnp.where(tril, G[..., :, None] - G[..., None, :], -jnp.inf))
    kb = k * beta[..., None]
    Lmat = jnp.where(strict, jnp.einsum('bhnid,bhnjd->bhnij', kb, k) * decay, 0.0)
    eye = jnp.eye(CHUNK, dtype=jnp.float32)
    T = lax.linalg.triangular_solve(eye + Lmat, jnp.broadcast_to(eye, Lmat.shape),
                                    left_side=True, lower=True, unit_diagonal=True)
    u = T @ (v * beta[..., None])
    w = T @ (kb * jnp.exp(G)[..., None])
    attn_intra = jnp.einsum('bhnid,bhnjd->bhnij', q, k) * decay
    q_dec = q * jnp.exp(G)[..., None]
    k_dec = k * jnp.exp(G[..., -1:] - G)[..., None]
    g_last = jnp.exp(G[..., -1])

    def step(state, xs):
        u_c, w_c, a_c, qd_c, kd_c, gl_c = xs
        v_new = u_c - jnp.einsum('bhck,bhkv->bhcv', w_c, state)
        o = jnp.einsum('bhck,bhkv->bhcv', qd_c, state) + jnp.einsum('bhij,bhjv->bhiv', a_c, v_new)
        state = state * gl_c[..., None, None] + jnp.einsum('bhck,bhcv->bhkv', kd_c, v_new)
        return state, o

    xs = tuple(jnp.moveaxis(t, 2, 0) for t in (u, w, attn_intra, q_dec, k_dec, g_last))
    state0 = jnp.zeros((B, B_HEADS, B_DIM, B_DIM), jnp.float32)
    _, o = lax.scan(step, state0, xs)
    o = jnp.moveaxis(o, 0, 2).reshape(B, B_HEADS, S, B_DIM).transpose(0, 2, 1, 3)
    zh = z.reshape(B, S, B_HEADS, B_DIM).astype(jnp.float32)
    o = rms(o) * norm_gain * jax.nn.silu(zh)
    return o.reshape(B, S, B_WIDTH)


PEER_SLOTS = PEER_HEADS * PEER_TOPK
PEER_HALF = PEER_QDIM // 2
ROW_WORDS = D_MODEL // 2
ROW_SUBL = ROW_WORDS // 128
STAGE_STRIDE = PEER_SLOTS + 8


def _top16_rows(s, iota_rows, fill, vals_scr, idx_scr, payload=None, pay_scr=None):
    for r in range(PEER_TOPK):
        m = jnp.max(s, axis=0, keepdims=True)
        first = jnp.min(jnp.where(s == m, iota_rows, fill), axis=0, keepdims=True)
        taken = iota_rows == first
        vals_scr[r:r + 1, :] = m
        if payload is None:
            idx_scr[r:r + 1, :] = first
        else:
            pay_scr[r:r + 1, :] = jnp.max(jnp.where(taken, payload, -1), axis=0, keepdims=True)
        s = jnp.where(taken, -jnp.inf, s)


def _peer_route_kernel(h_ref, wq_ref, sk_ref, eid_ref, gate_ref,
                       v1_scr, i1_scr, v2_scr, i2_scr, cv_scr, ce_scr, eid_scr, gate_scr, *, T):
    q = jnp.dot(h_ref[...].astype(bf16), wq_ref[...], preferred_element_type=f32).astype(bf16)
    kiota = lax.broadcasted_iota(jnp.int32, (PEER_NKEYS, T), 0)
    sub8 = lax.broadcasted_iota(jnp.int32, (8, T), 0)
    for p in range(PEER_HEADS):
        for half, (vs, is_) in enumerate(((v1_scr, i1_scr), (v2_scr, i2_scr))):
            g = 2 * p + half
            s = lax.dot_general(sk_ref[g], q[:, g * PEER_HALF:(g + 1) * PEER_HALF],
                                (((1,), (1,)), ((), ())), preferred_element_type=f32)
            _top16_rows(s, kiota, PEER_NKEYS, vs, is_)
        v1 = v1_scr[...]
        v2 = v2_scr[...]
        e1 = i1_scr[...] * PEER_NKEYS
        e2 = i2_scr[...]
        cv = [v1[0:1] + v2[0:8], v1[0:1] + v2[8:16]]
        ce = [e1[0:1] + e2[0:8], e1[0:1] + e2[8:16]]
        cf = [sub8, sub8 + 8]
        for a in range(1, 8):
            cv.append(v1[a:a + 1] + v2[0:8])
            ce.append(e1[a:a + 1] + e2[0:8])
            cf.append(sub8 + a * PEER_TOPK)
        cv.append(v1[8:16] + v2[0:1])
        ce.append(e1[8:16] + e2[0:1])
        cf.append((sub8 + 8) * PEER_TOPK)
        cand = jnp.concatenate(cv, axis=0)
        cexp = jnp.concatenate(ce, axis=0)
        cflat = jnp.concatenate(cf, axis=0)
        _top16_rows(cand, cflat, PEER_TOPK * PEER_TOPK, cv_scr, None, payload=cexp, pay_scr=ce_scr)
        top = cv_scr[...]
        ex = jnp.exp(top - top[0:1])
        gate_scr[p * PEER_TOPK:(p + 1) * PEER_TOPK, :] = ex / jnp.sum(ex, axis=0, keepdims=True)
        eid_scr[p * PEER_TOPK:(p + 1) * PEER_TOPK, :] = ce_scr[...]
    eid_ref[...] = eid_scr[...].T
    gate_ref[...] = gate_scr[...].T


def peer_route(h, w_query, sub_keys, *, T=128):
    N, D = h.shape
    sk = sub_keys.reshape(PEER_HEADS * 2, PEER_NKEYS, PEER_HALF).astype(bf16)
    kern = functools.partial(_peer_route_kernel, T=T)
    return pl.pallas_call(
        kern,
        grid=(N // T,),
        in_specs=[
            pl.BlockSpec((T, D), lambda i: (i, 0)),
            pl.BlockSpec((D, PEER_HEADS * PEER_QDIM), lambda i: (0, 0)),
            pl.BlockSpec((PEER_HEADS * 2, PEER_NKEYS, PEER_HALF), lambda i: (0, 0, 0)),
        ],
        out_specs=[pl.BlockSpec((T, PEER_SLOTS), lambda i: (i, 0)),
                   pl.BlockSpec((T, PEER_SLOTS), lambda i: (i, 0))],
        out_shape=[jax.ShapeDtypeStruct((N, PEER_SLOTS), jnp.int32),
                   jax.ShapeDtypeStruct((N, PEER_SLOTS), f32)],
        scratch_shapes=[
            pltpu.VMEM((PEER_TOPK, T), f32), pltpu.VMEM((PEER_TOPK, T), jnp.int32),
            pltpu.VMEM((PEER_TOPK, T), f32), pltpu.VMEM((PEER_TOPK, T), jnp.int32),
            pltpu.VMEM((PEER_TOPK, T), f32), pltpu.VMEM((PEER_TOPK, T), jnp.int32),
            pltpu.VMEM((PEER_SLOTS, T), jnp.int32), pltpu.VMEM((PEER_SLOTS, T), f32),
        ],
        compiler_params=pltpu.CompilerParams(dimension_semantics=("arbitrary",)),
        name="peer_route",
    )(h, w_query.astype(bf16), sk)


def pack_table(tab):
    bits = lax.bitcast_convert_type(tab.astype(bf16), jnp.uint16).astype(jnp.uint32)
    words = (bits[:, :ROW_WORDS] << 16) | bits[:, ROW_WORDS:]
    return words.reshape(tab.shape[0] * ROW_SUBL, 128)


def _gather_rows(eid_ref, tok, tab_ref, st):
    for k in range(PEER_SLOTS):
        off = pl.multiple_of(eid_ref[tok, k] * ROW_SUBL, ROW_SUBL)
        slab = tab_ref[pl.ds(off, ROW_SUBL), :]
        hi = lax.bitcast_convert_type(slab & jnp.uint32(0xFFFF0000), f32)
        lo = lax.bitcast_convert_type(slab << 16, f32)
        st[pl.ds(k, ROW_SUBL, stride=STAGE_STRIDE), :] = hi
        st[pl.ds(ROW_SUBL * STAGE_STRIDE + k, ROW_SUBL, stride=STAGE_STRIDE), :] = lo
    chunks = [st[c * STAGE_STRIDE:c * STAGE_STRIDE + PEER_SLOTS, :] for c in range(2 * ROW_SUBL)]
    return jnp.concatenate(chunks, axis=1).astype(bf16)


def _peer_u_kernel(eid_ref, h_ref, gate_ref, tab_ref, w_ref, stage_scr, *, T):
    row8 = lax.broadcasted_iota(jnp.int32, (8, PEER_SLOTS), 0)

    def group(g, carry):
        t0 = pl.multiple_of(g * 8, 8)
        h8 = h_ref[pl.ds(t0, 8), :].astype(bf16)
        acc = jnp.zeros((8, PEER_SLOTS), f32)
        for i in range(8):
            rows = _gather_rows(eid_ref, t0 + i, tab_ref, stage_scr.at[i % 2])
            res = lax.dot_general(h8, rows, (((1,), (1,)), ((), ())), preferred_element_type=f32)
            acc = jnp.where(row8 == i, res, acc)
        gelu = 0.5 * acc * (1.0 + lax.erf(acc * (2.0 ** -0.5)))
        w_ref[pl.ds(t0, 8), :] = gelu * gate_ref[pl.ds(t0, 8), :]
        return carry

    lax.fori_loop(0, T // 8, group, 0)


def _peer_v_kernel(eid_ref, w_ref, x_ref, g2_ref, tab_ref, o_ref, stage_scr, *, T):
    row8 = lax.broadcasted_iota(jnp.int32, (8, D_MODEL), 0)

    def group(g, carry):
        t0 = pl.multiple_of(g * 8, 8)
        w8 = w_ref[pl.ds(t0, 8), :].astype(bf16)
        acc = jnp.zeros((8, D_MODEL), f32)
        for i in range(8):
            rows = _gather_rows(eid_ref, t0 + i, tab_ref, stage_scr.at[i % 2])
            res = jnp.dot(w8, rows, preferred_element_type=f32)
            acc = jnp.where(row8 == i, res, acc)
        o_ref[pl.ds(t0, 8), :] = x_ref[pl.ds(t0, 8), :] + g2_ref[0] * acc
        return carry

    lax.fori_loop(0, T // 8, group, 0)


_TABLE_VMEM_LIMIT = 56 * 1024 * 1024


def _table_spec():
    return pl.BlockSpec((N_EXPERTS * ROW_SUBL, 128), lambda i: (0, 0), pipeline_mode=pl.Buffered(1))


def _stage_scratch():
    return pltpu.VMEM((2, 2 * ROW_SUBL * STAGE_STRIDE, 128), f32)


def peer_u(eid, h, gate, u_pk, *, T=128):
    N, D = h.shape
    return pl.pallas_call(
        functools.partial(_peer_u_kernel, T=T),
        grid=(N // T,),
        in_specs=[
            pl.BlockSpec((T, PEER_SLOTS), lambda i: (i, 0), memory_space=pltpu.SMEM),
            pl.BlockSpec((T, D), lambda i: (i, 0)),
            pl.BlockSpec((T, PEER_SLOTS), lambda i: (i, 0)),
            _table_spec(),
        ],
        out_specs=pl.BlockSpec((T, PEER_SLOTS), lambda i: (i, 0)),
        out_shape=jax.ShapeDtypeStruct((N, PEER_SLOTS), f32),
        scratch_shapes=[_stage_scratch()],
        compiler_params=pltpu.CompilerParams(dimension_semantics=("arbitrary",),
                                             vmem_limit_bytes=_TABLE_VMEM_LIMIT),
        name="peer_u",
    )(eid, h, gate, u_pk)


def peer_v(eid, w, x, g2, v_pk, *, T=128):
    N, D = x.shape
    per_batch = N // g2.shape[0]
    return pl.pallas_call(
        functools.partial(_peer_v_kernel, T=T),
        grid=(N // T,),
        in_specs=[
            pl.BlockSpec((T, PEER_SLOTS), lambda i: (i, 0), memory_space=pltpu.SMEM),
            pl.BlockSpec((T, PEER_SLOTS), lambda i: (i, 0)),
            pl.BlockSpec((T, D), lambda i: (i, 0)),
            pl.BlockSpec((1, 1, D), lambda i: (i * T // per_batch, 0, 0)),
            _table_spec(),
        ],
        out_specs=pl.BlockSpec((T, D), lambda i: (i, 0)),
        out_shape=jax.ShapeDtypeStruct((N, D), f32),
        scratch_shapes=[_stage_scratch()],
        compiler_params=pltpu.CompilerParams(dimension_semantics=("arbitrary",),
                                             vmem_limit_bytes=_TABLE_VMEM_LIMIT),
        name="peer_v",
    )(eid, w, x, g2.reshape(g2.shape[0], 1, D), v_pk)


def _mm_kernel(a_ref, b_ref, o_ref):
    o_ref[...] = jnp.dot(a_ref[...].astype(jnp.bfloat16), b_ref[...].astype(jnp.bfloat16),
                         preferred_element_type=jnp.float32)


def pallas_matmul(a, b, tm=512, tn=512):
    M, K = a.shape
    N = b.shape[1]
    Np = -(-N // tn) * tn
    if Np != N:
        b = jnp.pad(b, ((0, 0), (0, Np - N)))
    out = pl.pallas_call(
        _mm_kernel,
        grid=(M // tm, Np // tn),
        in_specs=[pl.BlockSpec((tm, K), lambda i, j: (i, 0)),
                  pl.BlockSpec((K, tn), lambda i, j: (0, j))],
        out_specs=pl.BlockSpec((tm, tn), lambda i, j: (i, j)),
        out_shape=jax.ShapeDtypeStruct((M, Np), jnp.float32),
    )(a, b)
    return out[:, :N]


def kernel(x, c, w_ada, b_ada, w_in, q_gain, k_gain, w_uv, conv_w, a_log, dt_bias,
           gdn_gain, w_out, w_query, sub_keys, u_tab, v_tab):
    B, S, _ = x.shape
    offsets = [int(o) for o in np.cumsum(IN_SPLITS)[:-1]]
    for l in range(DEPTH):
        sh1, sc1, g1, sh2, sc2, g2 = jnp.split(jax.nn.silu(c) @ w_ada[l] + b_ada[l], 6, axis=-1)
        h = modulate(rms(x), sh1, sc1).astype(x.dtype)
        proj = pallas_matmul(h.reshape(B * S, -1), w_in[l]).reshape(B, S, -1)
        aq, akv, iq, ik, iw, bq, bk, bv, bz, ba, bb = jnp.split(proj, offsets, axis=-1)
        ya = dsa_attention(aq, akv, iq, ik, iw, q_gain[l], k_gain[l], w_uv[l])
        yb = gated_deltanet(bq, bk, bv, bz, ba, bb, conv_w[l], a_log[l], dt_bias[l], gdn_gain[l])
        mix = pallas_matmul(jnp.concatenate([ya, yb], axis=-1).reshape(B * S, -1), w_out[l]).reshape(B, S, -1)
        x = x + (g1[:, None, :] * mix).astype(x.dtype)
        h = modulate(rms(x), sh2, sc2).astype(x.dtype)
        hf = h.reshape(B * S, D_MODEL)
        eid, gate = peer_route(hf, w_query[l], sub_keys[l])
        w = peer_u(eid, hf, gate, pack_table(u_tab[l]))
        x = peer_v(eid, w, x.reshape(B * S, D_MODEL), g2, pack_table(v_tab[l])).reshape(B, S, D_MODEL)
    return x
```

```python
import functools
import math
import jax
import jax.numpy as jnp
from jax import lax
import numpy as np
from jax.experimental import pallas as pl
from jax.experimental.pallas import tpu as pltpu

D_MODEL = 1024
BATCH = 8
SEQ = 4096
DEPTH = 1

EPS = 1e-6
A_HEADS = 8
A_LAT = 128
A_VDIM = 64
IDX_HEADS = 4
IDX_DIM = 64
IDX_TOPK_MAX = 256
Q_BLOCK = 128
B_HEADS = 4
B_DIM = 128
CONV_W = 4
CHUNK = 64
PEER_HEADS = 8
PEER_NKEYS = 128
PEER_QDIM = 256
PEER_TOPK = 16
PEER_BLOCK = 128
N_EXPERTS = PEER_NKEYS * PEER_NKEYS

A_WIDTH = A_HEADS * A_VDIM
B_WIDTH = B_HEADS * B_DIM
MIX_WIDTH = A_WIDTH + B_WIDTH
IN_SPLITS = (A_HEADS * A_LAT, A_LAT, IDX_HEADS * IDX_DIM, IDX_DIM, IDX_HEADS,
             B_WIDTH, B_WIDTH, B_WIDTH, B_WIDTH, B_HEADS, B_HEADS)
IN_WIDTH = sum(IN_SPLITS)


f32 = jnp.float32
bf16 = jnp.bfloat16
INT_MIN = -2 ** 31
_PROJ_VMEM_LIMIT = 48 * 1024 * 1024


def _rms_modulate(x, shift, scale):
    xn = x * lax.rsqrt(jnp.mean(x * x, -1, keepdims=True) + EPS)
    return xn * (1.0 + scale) + shift


def _adaln_kernel(c_ref, w_ref, b_ref, o_ref):
    c = c_ref[...]
    s = (c * jax.nn.sigmoid(c)).astype(bf16)
    o_ref[...] = jnp.dot(s, w_ref[...].astype(bf16), preferred_element_type=f32) + b_ref[...]


def adaln(c, w, b, *, tn=512):
    B, D = c.shape
    N = w.shape[1]
    return pl.pallas_call(
        _adaln_kernel,
        grid=(N // tn,),
        in_specs=[pl.BlockSpec((B, D), lambda j: (0, 0)),
                  pl.BlockSpec((D, tn), lambda j: (0, j)),
                  pl.BlockSpec((1, tn), lambda j: (0, j))],
        out_specs=pl.BlockSpec((B, tn), lambda j: (0, j)),
        out_shape=jax.ShapeDtypeStruct((B, N), f32),
        name="adaln",
    )(c, w, b.reshape(1, N))


SMALL_IK = (0, IDX_DIM)
SMALL_IW = (IDX_DIM, IDX_DIM + IDX_HEADS)
SMALL_BA = (IDX_DIM + IDX_HEADS, IDX_DIM + IDX_HEADS + B_HEADS)
SMALL_BB = (IDX_DIM + IDX_HEADS + B_HEADS, IDX_DIM + IDX_HEADS + 2 * B_HEADS)


def _inproj_kernel(x_ref, sh_ref, sc_ref, wq_ref, wkv_ref, wiq_ref, wsm_ref, wb_ref,
                   oq_ref, okv_ref, oiq_ref, osm_ref, ob_ref):
    h = _rms_modulate(x_ref[...], sh_ref[0], sc_ref[0]).astype(bf16)
    for w_ref, o_ref in ((wq_ref, oq_ref), (wkv_ref, okv_ref), (wiq_ref, oiq_ref),
                         (wsm_ref, osm_ref), (wb_ref, ob_ref)):
        o_ref[...] = jnp.dot(h, w_ref[...], preferred_element_type=f32)


def in_projection(x, shift, scale, w_in, *, tm=512):
    N, D = x.shape
    per_batch = N // shift.shape[0]
    o = [int(v) for v in np.cumsum((0,) + IN_SPLITS)]
    wb = w_in.astype(bf16)
    w_q, w_kv, w_iq = wb[:, o[0]:o[1]], wb[:, o[1]:o[2]], wb[:, o[2]:o[3]]
    w_small = jnp.concatenate([wb[:, o[3]:o[5]], wb[:, o[9]:o[11]],
                               jnp.zeros((D, 128 - SMALL_BB[1]), bf16)], axis=1)
    w_b = wb[:, o[5]:o[9]]
    weights = (w_q, w_kv, w_iq, w_small, w_b)
    mod = pl.BlockSpec((1, 1, D), lambda i: (i * tm // per_batch, 0, 0))
    return pl.pallas_call(
        _inproj_kernel,
        grid=(N // tm,),
        in_specs=[pl.BlockSpec((tm, D), lambda i: (i, 0)), mod, mod]
                 + [pl.BlockSpec(w.shape, lambda i: (0, 0), pipeline_mode=pl.Buffered(1)) for w in weights],
        out_specs=[pl.BlockSpec((tm, w.shape[1]), lambda i: (i, 0)) for w in weights],
        out_shape=[jax.ShapeDtypeStruct((N, w.shape[1]), f32) for w in weights],
        compiler_params=pltpu.CompilerParams(dimension_semantics=("arbitrary",),
                                             vmem_limit_bytes=_PROJ_VMEM_LIMIT),
        name="in_projection",
    )(x, shift[:, None, :], scale[:, None, :], *weights)


def _outproj_kernel(ya_ref, yb_ref, x_ref, g1_ref, sh_ref, sc_ref, wa_ref, wb_ref, x1_ref, h2_ref):
    mix = (jnp.dot(ya_ref[...].astype(bf16), wa_ref[...], preferred_element_type=f32)
           + jnp.dot(yb_ref[...].astype(bf16), wb_ref[...], preferred_element_type=f32))
    x1 = x_ref[...] + g1_ref[0] * mix
    x1_ref[...] = x1
    h2_ref[...] = _rms_modulate(x1, sh_ref[0], sc_ref[0])


def out_projection(ya, yb, x, gate, shift, scale, w_out, *, tm=512):
    N, D = x.shape
    per_batch = N // gate.shape[0]
    wb = w_out.astype(bf16)
    w_a, w_b = wb[:A_WIDTH], wb[A_WIDTH:]
    mod = pl.BlockSpec((1, 1, D), lambda i: (i * tm // per_batch, 0, 0))
    row = pl.BlockSpec((tm, D), lambda i: (i, 0))
    half = pl.BlockSpec((tm, A_WIDTH), lambda i: (i, 0))
    return pl.pallas_call(
        _outproj_kernel,
        grid=(N // tm,),
        in_specs=[half, half, row, mod, mod, mod,
                  pl.BlockSpec(w_a.shape, lambda i: (0, 0)), pl.BlockSpec(w_b.shape, lambda i: (0, 0))],
        out_specs=[row, row],
        out_shape=[jax.ShapeDtypeStruct((N, D), f32), jax.ShapeDtypeStruct((N, D), f32)],
        compiler_params=pltpu.CompilerParams(dimension_semantics=("arbitrary",),
                                             vmem_limit_bytes=_PROJ_VMEM_LIMIT),
        name="out_projection",
    )(ya, yb, x, gate[:, None, :], shift[:, None, :], scale[:, None, :], w_a, w_b)


def _dsa_kernel(q_ref, kv_ref, iq_ref, ik_ref, iw_ref, qg_ref, kg_ref, wbd_ref, tri_ref, o_ref,
                kaug_scr, ikb_scr, key_scr, qaug_scr, m_scr, l_scr, acc_scr, *, S, TQ, TK, topk):
    qi = pl.program_id(1)
    scale = A_LAT ** -0.5
    idx_scale = (IDX_DIM ** -0.5) * (IDX_HEADS ** -0.5)

    @pl.when(qi == 0)
    def _prep_keys():
        kv = kv_ref[0]
        kn = kv * lax.rsqrt(jnp.mean(kv * kv, -1, keepdims=True) + EPS) * kg_ref[...]
        pos = lax.broadcasted_iota(jnp.int32, (S, A_LAT), 0)
        lane = lax.broadcasted_iota(jnp.int32, (S, A_LAT), 1)
        hi = (pos >> 6).astype(f32)
        lo = (pos & 63).astype(f32)
        extra = jnp.where(lane == 0, hi, jnp.where(lane == 1, lo, 0.0))
        kaug_scr[:, :A_LAT] = kn.astype(bf16)
        kaug_scr[:, A_LAT:] = extra.astype(bf16)
        ikb_scr[...] = ik_ref[0].astype(bf16)

    q = q_ref[0]
    lane = lax.broadcasted_iota(jnp.int32, (TQ, A_LAT), 1)
    for h in range(A_HEADS):
        qh = q[:, h * A_LAT:(h + 1) * A_LAT]
        qn = qh * lax.rsqrt(jnp.mean(qh * qh, -1, keepdims=True) + EPS) * qg_ref[...] * scale
        slope = 2.0 ** (-8.0 * (h + 1) / A_HEADS)
        extra = jnp.where(lane == 0, slope * 64.0, jnp.where(lane == 1, slope, 0.0))
        qaug_scr[h, :, :A_LAT] = qn.astype(bf16)
        qaug_scr[h, :, A_LAT:] = extra.astype(bf16)

    nchunks = (qi * TQ + TQ + TK - 1) // TK
    qpos = qi * TQ + lax.broadcasted_iota(jnp.int32, (TQ, 1), 0)
    colb = lax.broadcasted_iota(jnp.int32, (TQ, TK), 1)
    iq = iq_ref[0]
    iqb = [iq[:, h * IDX_DIM:(h + 1) * IDX_DIM].astype(bf16) for h in range(IDX_HEADS)]
    iw = iw_ref[0]

    def score_chunk(c, carry):
        off = pl.multiple_of(c * TK, TK)
        ikc = ikb_scr[pl.ds(off, TK), :]
        acc = jnp.zeros((TQ, TK), f32)
        for h in range(IDX_HEADS):
            lg = lax.dot_general(iqb[h], ikc, (((1,), (1,)), ((), ())), preferred_element_type=f32)
            acc = acc + jnp.maximum(lg, 0.0) * iw[:, h:h + 1]
        sc = acc * idx_scale
        sc = jnp.where(colb + off <= qpos, sc, -jnp.inf)
        bits = pltpu.bitcast(sc, jnp.int32)
        key_scr[:, pl.ds(off, TK)] = jnp.where(bits < 0, bits ^ jnp.int32(0x7FFFFFFF), bits)
        return carry

    lax.fori_loop(0, nchunks, score_chunk, 0)

    def count(pred_fn):
        def body(c, cnt):
            off = pl.multiple_of(c * TK, TK)
            hit = pred_fn(key_scr[:, pl.ds(off, TK)]).astype(jnp.int32)
            part = hit[:, 0:128]
            for j in range(1, TK // 128):
                part = part + hit[:, j * 128:(j + 1) * 128]
            return cnt + part
        cnt = lax.fori_loop(0, nchunks, body, jnp.zeros((TQ, 128), jnp.int32))
        return jnp.sum(cnt, axis=1, keepdims=True)

    def bit_step(i, T):
        cand = T + lax.shift_left(jnp.int32(1), jnp.int32(31) - i)
        cnt = count(lambda k: k >= cand)
        return jnp.where(cnt >= topk, cand, T)

    T = lax.fori_loop(0, 32, bit_step, jnp.full((TQ, 1), INT_MIN, jnp.int32))
    n_gt = count(lambda k: k > T)
    room = (topk - n_gt).astype(f32)

    m_scr[...] = jnp.full(m_scr.shape, -jnp.inf, f32)
    l_scr[...] = jnp.zeros(l_scr.shape, f32)
    acc_scr[...] = jnp.zeros(acc_scr.shape, f32)

    def attend_chunk(c, ties_before):
        off = pl.multiple_of(c * TK, TK)
        keyc = key_scr[:, pl.ds(off, TK)]
        eq = keyc == T
        eqf = jnp.where(eq, 1.0, 0.0)
        pref = jnp.dot(eqf.astype(bf16), tri_ref[...], preferred_element_type=f32)
        sel = (keyc > T) | (eq & (pref + ties_before < room))
        sel = sel & (colb + off <= qpos)
        kc = kaug_scr[pl.ds(off, TK), :]
        vc = kc[:, :A_LAT]
        for h in range(A_HEADS):
            s = lax.dot_general(qaug_scr[h], kc, (((1,), (1,)), ((), ())), preferred_element_type=f32)
            s = jnp.where(sel, s, -jnp.inf)
            m_old = m_scr[h]
            m_new = jnp.maximum(m_old, jnp.max(s, axis=1, keepdims=True))
            m_safe = jnp.where(m_new == -jnp.inf, 0.0, m_new)
            p = jnp.exp(s - m_safe)
            alpha = jnp.exp(m_old - m_safe)
            l_scr[h] = alpha * l_scr[h] + jnp.sum(p, axis=1, keepdims=True)
            acc_scr[h] = alpha * acc_scr[h] + jnp.dot(p.astype(bf16), vc, preferred_element_type=f32)
            m_scr[h] = m_new
        return ties_before + jnp.sum(eqf, axis=1, keepdims=True)

    lax.fori_loop(0, nchunks, attend_chunk, jnp.zeros((TQ, 1), f32))

    o = jnp.concatenate([(acc_scr[h] / l_scr[h]).astype(bf16) for h in range(A_HEADS)], axis=1)
    o_ref[0] = jnp.dot(o, wbd_ref[...], preferred_element_type=f32)


def dsa_attention(q_lat, kv, iq, ik, iw, q_gain, k_gain, w_uv, *, TQ=128, TK=512):
    B, S, _ = q_lat.shape
    topk = min(IDX_TOPK_MAX, S // 4)
    TK = min(TK, S)
    wbd = jnp.zeros((A_HEADS * A_LAT, A_WIDTH), f32)
    for h in range(A_HEADS):
        wbd = wbd.at[h * A_LAT:(h + 1) * A_LAT, h * A_VDIM:(h + 1) * A_VDIM].set(w_uv[h])
    tri = (jnp.arange(TK)[:, None] < jnp.arange(TK)[None, :]).astype(bf16)
    kern = functools.partial(_dsa_kernel, S=S, TQ=TQ, TK=TK, topk=topk)
    return pl.pallas_call(
        kern,
        grid=(B, S // TQ),
        in_specs=[
            pl.BlockSpec((1, TQ, A_HEADS * A_LAT), lambda b, i: (b, i, 0)),
            pl.BlockSpec((1, S, A_LAT), lambda b, i: (b, 0, 0)),
            pl.BlockSpec((1, TQ, IDX_HEADS * IDX_DIM), lambda b, i: (b, i, 0)),
            pl.BlockSpec((1, S, IDX_DIM), lambda b, i: (b, 0, 0)),
            pl.BlockSpec((1, TQ, IDX_HEADS), lambda b, i: (b, i, 0)),
            pl.BlockSpec((1, A_LAT), lambda b, i: (0, 0)),
            pl.BlockSpec((1, A_LAT), lambda b, i: (0, 0)),
            pl.BlockSpec((A_HEADS * A_LAT, A_WIDTH), lambda b, i: (0, 0)),
            pl.BlockSpec((TK, TK), lambda b, i: (0, 0)),
        ],
        out_specs=pl.BlockSpec((1, TQ, A_WIDTH), lambda b, i: (b, i, 0)),
        out_shape=jax.ShapeDtypeStruct((B, S, A_WIDTH), f32),
        scratch_shapes=[
            pltpu.VMEM((S, 2 * A_LAT), bf16),
            pltpu.VMEM((S, IDX_DIM), bf16),
            pltpu.VMEM((TQ, S), jnp.int32),
            pltpu.VMEM((A_HEADS, TQ, 2 * A_LAT), bf16),
            pltpu.VMEM((A_HEADS, TQ, 1), f32),
            pltpu.VMEM((A_HEADS, TQ, 1), f32),
            pltpu.VMEM((A_HEADS, TQ, A_LAT), f32),
        ],
        compiler_params=pltpu.CompilerParams(dimension_semantics=("arbitrary", "arbitrary")),
        name="dsa_attention",
    )(q_lat, kv, iq, ik, iw, q_gain.reshape(1, -1), k_gain.reshape(1, -1), wbd.astype(bf16), tri)


CONV_HALO = 8
_HI = lax.Precision.HIGHEST


def _softplus(x):
    return jnp.maximum(x, 0.0) + jnp.log1p(jnp.exp(-jnp.abs(x)))


def _gdn_kernel(q_ref, k_ref, v_ref, z_ref, a_ref, b_ref, at_ref, bt_ref, cw_ref, alog_ref, dtb_ref,
                gain_ref, o_ref, tail_scr, act_scr, state_scr, *, SB):
    sj = pl.program_id(1)

    @pl.when(sj == 0)
    def _reset():
        tail_scr[...] = jnp.zeros(tail_scr.shape, f32)
        state_scr[...] = jnp.zeros(state_scr.shape, f32)

    cw = cw_ref[...]
    for idx, x_ref in enumerate((q_ref, k_ref, v_ref)):
        x = x_ref[0]
        xc = jnp.concatenate([tail_scr[idx], x], axis=0)
        y = jnp.zeros((SB, B_WIDTH), f32)
        for j in range(CONV_W):
            lo = CONV_HALO - (CONV_W - 1) + j
            y = y + cw[j:j + 1, idx * B_WIDTH:(idx + 1) * B_WIDTH] * xc[lo:lo + SB]
        tail_scr[idx] = x[SB - CONV_HALO:SB]
        act_scr[idx] = y * jax.nn.sigmoid(y)

    ri = lax.broadcasted_iota(jnp.int32, (CHUNK, CHUNK), 0)
    ci = lax.broadcasted_iota(jnp.int32, (CHUNK, CHUNK), 1)
    tril = ri >= ci
    strict = ri > ci
    tril_f = jnp.where(tril, 1.0, 0.0)
    triu_f = jnp.where(ri <= ci, 1.0, 0.0)
    eye = jnp.where(ri == ci, 1.0, 0.0)
    nt = (((1,), (1,)), ((), ()))

    def chunk(c, carry):
        rows = pl.ds(pl.multiple_of(c * CHUNK, CHUNK), CHUNK)
        for h in range(B_HEADS):
            lanes = slice(h * B_DIM, (h + 1) * B_DIM)
            q = act_scr[0, rows, lanes]
            k = act_scr[1, rows, lanes]
            v = act_scr[2, rows, lanes]
            q = q * lax.rsqrt(jnp.sum(q * q, -1, keepdims=True) + EPS) * (B_DIM ** -0.5)
            k = k * lax.rsqrt(jnp.sum(k * k, -1, keepdims=True) + EPS)
            neg_rate = -jnp.exp(alog_ref[0:1, h:h + 1])
            dtb = dtb_ref[0:1, h:h + 1]
            g_col = neg_rate * _softplus(a_ref[0, rows, h:h + 1] + dtb)
            g_row = neg_rate * _softplus(at_ref[0, h, pl.ds(c, 1), :] + dtb)
            beta = jax.nn.sigmoid(b_ref[0, rows, h:h + 1])
            Gc = jnp.dot(tril_f, jnp.broadcast_to(g_col, (CHUNK, CHUNK)), precision=_HI,
                         preferred_element_type=f32)
            Gr = jnp.dot(jnp.broadcast_to(g_row, (CHUNK, CHUNK)), triu_f, precision=_HI,
                         preferred_element_type=f32)
            G = Gc[:, 0:1]
            G_last = Gc[CHUNK - 1:CHUNK, 0:1]
            decay = jnp.exp(jnp.where(tril, Gc - Gr, -jnp.inf))
            kb = k * beta
            kbf = k.astype(bf16)
            L = jnp.where(strict, lax.dot_general(kb.astype(bf16), kbf, nt, preferred_element_type=f32) * decay, 0.0)
            P = eye - L
            M = jnp.dot(L, L, precision=_HI, preferred_element_type=f32)
            for lvl in range(5):
                P = P + jnp.dot(P, M, precision=_HI, preferred_element_type=f32)
                if lvl < 4:
                    M = jnp.dot(M, M, precision=_HI, preferred_element_type=f32)
            Tb = P.astype(bf16)
            eG = jnp.exp(G)
            u = jnp.dot(Tb, (v * beta).astype(bf16), preferred_element_type=f32)
            w = jnp.dot(Tb, (kb * eG).astype(bf16), preferred_element_type=f32)
            attn = lax.dot_general(q.astype(bf16), kbf, nt, preferred_element_type=f32) * decay
            q_dec = q * eG
            k_dec = k * jnp.exp(G_last - G)
            st = state_scr[h]
            stb = st.astype(bf16)
            v_new = u - jnp.dot(w.astype(bf16), stb, preferred_element_type=f32)
            vnb = v_new.astype(bf16)
            o = (jnp.dot(q_dec.astype(bf16), stb, preferred_element_type=f32)
                 + jnp.dot(attn.astype(bf16), vnb, preferred_element_type=f32))
            state_scr[h] = st * jnp.exp(G_last) + lax.dot_general(
                k_dec.astype(bf16), vnb, (((0,), (0,)), ((), ())), preferred_element_type=f32)
            zh = z_ref[0, rows, lanes]
            o = o * lax.rsqrt(jnp.mean(o * o, -1, keepdims=True) + EPS) * gain_ref[...]
            o_ref[0, rows, lanes] = o * (zh * jax.nn.sigmoid(zh))
        return carry

    lax.fori_loop(0, SB // CHUNK, chunk, 0)


def gated_deltanet(qkvz, a, b, conv_w, a_log, dt_bias, norm_gain, *, SB=512):
    B, S, _ = qkvz.shape
    SB = min(SB, S)
    nch = S // CHUNK
    at = a.transpose(0, 2, 1).reshape(B, B_HEADS, nch, CHUNK)
    bt = b.transpose(0, 2, 1).reshape(B, B_HEADS, nch, CHUNK)
    wide = pl.BlockSpec((1, SB, B_WIDTH), lambda bi, j: (bi, j, 0))
    part = [pl.BlockSpec((1, SB, B_WIDTH), functools.partial(lambda bi, j, n: (bi, j, n), n=n))
            for n in range(4)]
    narrow = pl.BlockSpec((1, SB, B_HEADS), lambda bi, j: (bi, j, 0))
    rowwise = pl.BlockSpec((1, B_HEADS, SB // CHUNK, CHUNK), lambda bi, j: (bi, 0, j, 0))
    return pl.pallas_call(
        functools.partial(_gdn_kernel, SB=SB),
        grid=(B, S // SB),
        in_specs=[*part, narrow, narrow, rowwise, rowwise,
                  pl.BlockSpec((CONV_W, 3 * B_WIDTH), lambda bi, j: (0, 0)),
                  pl.BlockSpec((1, B_HEADS), lambda bi, j: (0, 0)),
                  pl.BlockSpec((1, B_HEADS), lambda bi, j: (0, 0)),
                  pl.BlockSpec((1, B_DIM), lambda bi, j: (0, 0))],
        out_specs=wide,
        out_shape=jax.ShapeDtypeStruct((B, S, B_WIDTH), f32),
        scratch_shapes=[pltpu.VMEM((3, CONV_HALO, B_WIDTH), f32),
                        pltpu.VMEM((3, SB, B_WIDTH), f32),
                        pltpu.VMEM((B_HEADS, B_DIM, B_DIM), f32)],
        compiler_params=pltpu.CompilerParams(dimension_semantics=("arbitrary", "arbitrary")),
        name="gated_deltanet",
    )(qkvz, qkvz, qkvz, qkvz, a, b, at, bt, conv_w,
      a_log.reshape(1, -1), dt_bias.reshape(1, -1), norm_gain.reshape(1, -1))


PEER_SLOTS = PEER_HEADS * PEER_TOPK
PEER_HALF = PEER_QDIM // 2
ROW_WORDS = D_MODEL // 2
ROW_SUBL = ROW_WORDS // 128
STAGE_STRIDE = PEER_SLOTS + 8


def _top16_rows(s, iota_rows, fill, vals_scr, idx_scr, payload=None, pay_scr=None):
    for r in range(PEER_TOPK):
        m = jnp.max(s, axis=0, keepdims=True)
        first = jnp.min(jnp.where(s == m, iota_rows, fill), axis=0, keepdims=True)
        taken = iota_rows == first
        vals_scr[r:r + 1, :] = m
        if payload is None:
            idx_scr[r:r + 1, :] = first
        else:
            pay_scr[r:r + 1, :] = jnp.max(jnp.where(taken, payload, -1), axis=0, keepdims=True)
        s = jnp.where(taken, -jnp.inf, s)


def _peer_route_kernel(h_ref, wq_ref, sk_ref, eid_ref, gate_ref,
                       v1_scr, i1_scr, v2_scr, i2_scr, cv_scr, ce_scr, eid_scr, gate_scr, *, T):
    q = jnp.dot(h_ref[...].astype(bf16), wq_ref[...], preferred_element_type=f32).astype(bf16)
    kiota = lax.broadcasted_iota(jnp.int32, (PEER_NKEYS, T), 0)
    sub8 = lax.broadcasted_iota(jnp.int32, (8, T), 0)
    for p in range(PEER_HEADS):
        for half, (vs, is_) in enumerate(((v1_scr, i1_scr), (v2_scr, i2_scr))):
            g = 2 * p + half
            s = lax.dot_general(sk_ref[g], q[:, g * PEER_HALF:(g + 1) * PEER_HALF],
                                (((1,), (1,)), ((), ())), preferred_element_type=f32)
            _top16_rows(s, kiota, PEER_NKEYS, vs, is_)
        v1 = v1_scr[...]
        v2 = v2_scr[...]
        e1 = i1_scr[...] * PEER_NKEYS
        e2 = i2_scr[...]
        cv = [v1[0:1] + v2[0:8], v1[0:1] + v2[8:16]]
        ce = [e1[0:1] + e2[0:8], e1[0:1] + e2[8:16]]
        cf = [sub8, sub8 + 8]
        for a in range(1, 8):
            cv.append(v1[a:a + 1] + v2[0:8])
            ce.append(e1[a:a + 1] + e2[0:8])
            cf.append(sub8 + a * PEER_TOPK)
        cv.append(v1[8:16] + v2[0:1])
        ce.append(e1[8:16] + e2[0:1])
        cf.append((sub8 + 8) * PEER_TOPK)
        cand = jnp.concatenate(cv, axis=0)
        cexp = jnp.concatenate(ce, axis=0)
        cflat = jnp.concatenate(cf, axis=0)
        _top16_rows(cand, cflat, PEER_TOPK * PEER_TOPK, cv_scr, None, payload=cexp, pay_scr=ce_scr)
        top = cv_scr[...]
        ex = jnp.exp(top - top[0:1])
        gate_scr[p * PEER_TOPK:(p + 1) * PEER_TOPK, :] = ex / jnp.sum(ex, axis=0, keepdims=True)
        eid_scr[p * PEER_TOPK:(p + 1) * PEER_TOPK, :] = ce_scr[...]
    eid_ref[...] = eid_scr[...].T
    gate_ref[...] = gate_scr[...].T


def peer_route(h, w_query, sub_keys, *, T=128):
    N, D = h.shape
    sk = sub_keys.reshape(PEER_HEADS * 2, PEER_NKEYS, PEER_HALF).astype(bf16)
    kern = functools.partial(_peer_route_kernel, T=T)
    return pl.pallas_call(
        kern,
        grid=(N // T,),
        in_specs=[
            pl.BlockSpec((T, D), lambda i: (i, 0)),
            pl.BlockSpec((D, PEER_HEADS * PEER_QDIM), lambda i: (0, 0)),
            pl.BlockSpec((PEER_HEADS * 2, PEER_NKEYS, PEER_HALF), lambda i: (0, 0, 0)),
        ],
        out_specs=[pl.BlockSpec((T, PEER_SLOTS), lambda i: (i, 0)),
                   pl.BlockSpec((T, PEER_SLOTS), lambda i: (i, 0))],
        out_shape=[jax.ShapeDtypeStruct((N, PEER_SLOTS), jnp.int32),
                   jax.ShapeDtypeStruct((N, PEER_SLOTS), f32)],
        scratch_shapes=[
            pltpu.VMEM((PEER_TOPK, T), f32), pltpu.VMEM((PEER_TOPK, T), jnp.int32),
            pltpu.VMEM((PEER_TOPK, T), f32), pltpu.VMEM((PEER_TOPK, T), jnp.int32),
            pltpu.VMEM((PEER_TOPK, T), f32), pltpu.VMEM((PEER_TOPK, T), jnp.int32),
            pltpu.VMEM((PEER_SLOTS, T), jnp.int32), pltpu.VMEM((PEER_SLOTS, T), f32),
        ],
        compiler_params=pltpu.CompilerParams(dimension_semantics=("arbitrary",)),
        name="peer_route",
    )(h, w_query.astype(bf16), sk)


def pack_table(tab):
    bits = lax.bitcast_convert_type(tab.astype(bf16), jnp.uint16).astype(jnp.uint32)
    words = (bits[:, :ROW_WORDS] << 16) | bits[:, ROW_WORDS:]
    return words.reshape(tab.shape[0] * ROW_SUBL, 128)


def _gather_rows(eid_ref, tok, tab_ref, st):
    for k in range(PEER_SLOTS):
        off = pl.multiple_of(eid_ref[tok, k] * ROW_SUBL, ROW_SUBL)
        slab = tab_ref[pl.ds(off, ROW_SUBL), :]
        hi = lax.bitcast_convert_type(slab & jnp.uint32(0xFFFF0000), f32)
        lo = lax.bitcast_convert_type(slab << 16, f32)
        st[pl.ds(k, ROW_SUBL, stride=STAGE_STRIDE), :] = hi
        st[pl.ds(ROW_SUBL * STAGE_STRIDE + k, ROW_SUBL, stride=STAGE_STRIDE), :] = lo
    chunks = [st[c * STAGE_STRIDE:c * STAGE_STRIDE + PEER_SLOTS, :] for c in range(2 * ROW_SUBL)]
    return jnp.concatenate(chunks, axis=1).astype(bf16)


def _peer_u_kernel(eid_ref, h_ref, gate_ref, tab_ref, w_ref, stage_scr, *, T):
    row8 = lax.broadcasted_iota(jnp.int32, (8, PEER_SLOTS), 0)

    def group(g, carry):
        t0 = pl.multiple_of(g * 8, 8)
        h8 = h_ref[pl.ds(t0, 8), :].astype(bf16)
        acc = jnp.zeros((8, PEER_SLOTS), f32)
        for i in range(8):
            rows = _gather_rows(eid_ref, t0 + i, tab_ref, stage_scr.at[i % 2])
            res = lax.dot_general(h8, rows, (((1,), (1,)), ((), ())), preferred_element_type=f32)
            acc = jnp.where(row8 == i, res, acc)
        gelu = 0.5 * acc * (1.0 + lax.erf(acc * (2.0 ** -0.5)))
        w_ref[pl.ds(t0, 8), :] = gelu * gate_ref[pl.ds(t0, 8), :]
        return carry

    lax.fori_loop(0, T // 8, group, 0)


def _peer_v_kernel(eid_ref, w_ref, x_ref, g2_ref, tab_ref, o_ref, stage_scr, *, T):
    row8 = lax.broadcasted_iota(jnp.int32, (8, D_MODEL), 0)

    def group(g, carry):
        t0 = pl.multiple_of(g * 8, 8)
        w8 = w_ref[pl.ds(t0, 8), :].astype(bf16)
        acc = jnp.zeros((8, D_MODEL), f32)
        for i in range(8):
            rows = _gather_rows(eid_ref, t0 + i, tab_ref, stage_scr.at[i % 2])
            res = jnp.dot(w8, rows, preferred_element_type=f32)
            acc = jnp.where(row8 == i, res, acc)
        o_ref[pl.ds(t0, 8), :] = x_ref[pl.ds(t0, 8), :] + g2_ref[0] * acc
        return carry

    lax.fori_loop(0, T // 8, group, 0)


_TABLE_VMEM_LIMIT = 56 * 1024 * 1024


def _table_spec():
    return pl.BlockSpec((N_EXPERTS * ROW_SUBL, 128), lambda i: (0, 0), pipeline_mode=pl.Buffered(1))


def _stage_scratch():
    return pltpu.VMEM((2, 2 * ROW_SUBL * STAGE_STRIDE, 128), f32)


def peer_u(eid, h, gate, u_pk, *, T=128):
    N, D = h.shape
    return pl.pallas_call(
        functools.partial(_peer_u_kernel, T=T),
        grid=(N // T,),
        in_specs=[
            pl.BlockSpec((T, PEER_SLOTS), lambda i: (i, 0), memory_space=pltpu.SMEM),
            pl.BlockSpec((T, D), lambda i: (i, 0)),
            pl.BlockSpec((T, PEER_SLOTS), lambda i: (i, 0)),
            _table_spec(),
        ],
        out_specs=pl.BlockSpec((T, PEER_SLOTS), lambda i: (i, 0)),
        out_shape=jax.ShapeDtypeStruct((N, PEER_SLOTS), f32),
        scratch_shapes=[_stage_scratch()],
        compiler_params=pltpu.CompilerParams(dimension_semantics=("arbitrary",),
                                             vmem_limit_bytes=_TABLE_VMEM_LIMIT),
        name="peer_u",
    )(eid, h, gate, u_pk)


def peer_v(eid, w, x, g2, v_pk, *, T=128):
    N, D = x.shape
    per_batch = N // g2.shape[0]
    return pl.pallas_call(
        functools.partial(_peer_v_kernel, T=T),
        grid=(N // T,),
        in_specs=[
            pl.BlockSpec((T, PEER_SLOTS), lambda i: (i, 0), memory_space=pltpu.SMEM),
            pl.BlockSpec((T, PEER_SLOTS), lambda i: (i, 0)),
            pl.BlockSpec((T, D), lambda i: (i, 0)),
            pl.BlockSpec((1, 1, D), lambda i: (i * T // per_batch, 0, 0)),
            _table_spec(),
        ],
        out_specs=pl.BlockSpec((T, D), lambda i: (i, 0)),
        out_shape=jax.ShapeDtypeStruct((N, D), f32),
        scratch_shapes=[_stage_scratch()],
        compiler_params=pltpu.CompilerParams(dimension_semantics=("arbitrary",),
                                             vmem_limit_bytes=_TABLE_VMEM_LIMIT),
        name="peer_v",
    )(eid, w, x, g2.reshape(g2.shape[0], 1, D), v_pk)


def kernel(x, c, w_ada, b_ada, w_in, q_gain, k_gain, w_uv, conv_w, a_log, dt_bias,
           gdn_gain, w_out, w_query, sub_keys, u_tab, v_tab):
    B, S, D = x.shape
    xf = x.reshape(B * S, D)
    for l in range(DEPTH):
        sh1, sc1, g1, sh2, sc2, g2 = jnp.split(adaln(c, w_ada[l], b_ada[l]), 6, axis=-1)
        aq, akv, iq, small, qkvz = in_projection(xf, sh1, sc1, w_in[l])
        small = small.reshape(B, S, -1)
        ik, iw, ba, bb = (small[..., lo:hi] for lo, hi in (SMALL_IK, SMALL_IW, SMALL_BA, SMALL_BB))
        ya = dsa_attention(aq.reshape(B, S, -1), akv.reshape(B, S, -1), iq.reshape(B, S, -1), ik, iw,
                           q_gain[l], k_gain[l], w_uv[l])
        yb = gated_deltanet(qkvz.reshape(B, S, -1), ba, bb, conv_w[l], a_log[l], dt_bias[l], gdn_gain[l])
        x1, h2 = out_projection(ya.reshape(B * S, -1), yb.reshape(B * S, -1), xf, g1, sh2, sc2, w_out[l])
        eid, gate = peer_route(h2, w_query[l], sub_keys[l])
        w = peer_u(eid, h2, gate, pack_table(u_tab[l]))
        xf = peer_v(eid, w, x1, g2, pack_table(v_tab[l]))
    return xf.reshape(B, S, D)
```

```python
import functools
import math
import jax
import jax.numpy as jnp
from jax import lax
import numpy as np
from jax.experimental import pallas as pl
from jax.experimental.pallas import tpu as pltpu

D_MODEL = 1024
BATCH = 8
SEQ = 4096
DEPTH = 1

EPS = 1e-6
A_HEADS = 8
A_LAT = 128
A_VDIM = 64
IDX_HEADS = 4
IDX_DIM = 64
IDX_TOPK_MAX = 256
Q_BLOCK = 128
B_HEADS = 4
B_DIM = 128
CONV_W = 4
CHUNK = 64
PEER_HEADS = 8
PEER_NKEYS = 128
PEER_QDIM = 256
PEER_TOPK = 16
PEER_BLOCK = 128
N_EXPERTS = PEER_NKEYS * PEER_NKEYS

A_WIDTH = A_HEADS * A_VDIM
B_WIDTH = B_HEADS * B_DIM
MIX_WIDTH = A_WIDTH + B_WIDTH
IN_SPLITS = (A_HEADS * A_LAT, A_LAT, IDX_HEADS * IDX_DIM, IDX_DIM, IDX_HEADS,
             B_WIDTH, B_WIDTH, B_WIDTH, B_WIDTH, B_HEADS, B_HEADS)
IN_WIDTH = sum(IN_SPLITS)


f32 = jnp.float32
bf16 = jnp.bfloat16
INT_MIN = -2 ** 31
SUM_LANE = 2
_PROJ_VMEM_LIMIT = 48 * 1024 * 1024


def _rms_modulate(x, shift, scale):
    xn = x * lax.rsqrt(jnp.mean(x * x, -1, keepdims=True) + EPS)
    return xn * (1.0 + scale) + shift


def _adaln_kernel(c_ref, w_ref, b_ref, o_ref):
    c = c_ref[...]
    s = (c * jax.nn.sigmoid(c)).astype(bf16)
    o_ref[...] = jnp.dot(s, w_ref[...].astype(bf16), preferred_element_type=f32) + b_ref[...]


def adaln(c, w, b, *, tn=512):
    B, D = c.shape
    N = w.shape[1]
    return pl.pallas_call(
        _adaln_kernel,
        grid=(N // tn,),
        in_specs=[pl.BlockSpec((B, D), lambda j: (0, 0)),
                  pl.BlockSpec((D, tn), lambda j: (0, j)),
                  pl.BlockSpec((1, tn), lambda j: (0, j))],
        out_specs=pl.BlockSpec((B, tn), lambda j: (0, j)),
        out_shape=jax.ShapeDtypeStruct((B, N), f32),
        name="adaln",
    )(c, w, b.reshape(1, N))


SMALL_IK = (0, IDX_DIM)
SMALL_IW = (IDX_DIM, IDX_DIM + IDX_HEADS)
SMALL_BA = (IDX_DIM + IDX_HEADS, IDX_DIM + IDX_HEADS + B_HEADS)
SMALL_BB = (IDX_DIM + IDX_HEADS + B_HEADS, IDX_DIM + IDX_HEADS + 2 * B_HEADS)


def _inproj_kernel(x_ref, sh_ref, sc_ref, wq_ref, wkv_ref, wiq_ref, wsm_ref, wb_ref,
                   oq_ref, okv_ref, oiq_ref, osm_ref, ob_ref):
    h = _rms_modulate(x_ref[...], sh_ref[0], sc_ref[0]).astype(bf16)
    for w_ref, o_ref in ((wq_ref, oq_ref), (wkv_ref, okv_ref), (wiq_ref, oiq_ref),
                         (wsm_ref, osm_ref), (wb_ref, ob_ref)):
        o_ref[...] = jnp.dot(h, w_ref[...], preferred_element_type=f32)


def in_projection(x, shift, scale, w_in, *, tm=512):
    N, D = x.shape
    per_batch = N // shift.shape[0]
    o = [int(v) for v in np.cumsum((0,) + IN_SPLITS)]
    wb = w_in.astype(bf16)
    w_q, w_kv, w_iq = wb[:, o[0]:o[1]], wb[:, o[1]:o[2]], wb[:, o[2]:o[3]]
    w_small = jnp.concatenate([wb[:, o[3]:o[5]], wb[:, o[9]:o[11]],
                               jnp.zeros((D, 128 - SMALL_BB[1]), bf16)], axis=1)
    w_b = wb[:, o[5]:o[9]]
    weights = (w_q, w_kv, w_iq, w_small, w_b)
    mod = pl.BlockSpec((1, 1, D), lambda i: (i * tm // per_batch, 0, 0))
    return pl.pallas_call(
        _inproj_kernel,
        grid=(N // tm,),
        in_specs=[pl.BlockSpec((tm, D), lambda i: (i, 0)), mod, mod]
                 + [pl.BlockSpec(w.shape, lambda i: (0, 0), pipeline_mode=pl.Buffered(1)) for w in weights],
        out_specs=[pl.BlockSpec((tm, w.shape[1]), lambda i: (i, 0)) for w in weights],
        out_shape=[jax.ShapeDtypeStruct((N, w.shape[1]), f32) for w in weights],
        compiler_params=pltpu.CompilerParams(dimension_semantics=("arbitrary",),
                                             vmem_limit_bytes=_PROJ_VMEM_LIMIT),
        name="in_projection",
    )(x, shift[:, None, :], scale[:, None, :], *weights)


def _outproj_kernel(ya_ref, yb_ref, x_ref, g1_ref, sh_ref, sc_ref, wa_ref, wb_ref, x1_ref, h2_ref):
    mix = (jnp.dot(ya_ref[...].astype(bf16), wa_ref[...], preferred_element_type=f32)
           + jnp.dot(yb_ref[...].astype(bf16), wb_ref[...], preferred_element_type=f32))
    x1 = x_ref[...] + g1_ref[0] * mix
    x1_ref[...] = x1
    h2_ref[...] = _rms_modulate(x1, sh_ref[0], sc_ref[0])


def out_projection(ya, yb, x, gate, shift, scale, w_out, *, tm=512):
    N, D = x.shape
    per_batch = N // gate.shape[0]
    wb = w_out.astype(bf16)
    w_a, w_b = wb[:A_WIDTH], wb[A_WIDTH:]
    mod = pl.BlockSpec((1, 1, D), lambda i: (i * tm // per_batch, 0, 0))
    row = pl.BlockSpec((tm, D), lambda i: (i, 0))
    half = pl.BlockSpec((tm, A_WIDTH), lambda i: (i, 0))
    return pl.pallas_call(
        _outproj_kernel,
        grid=(N // tm,),
        in_specs=[half, half, row, mod, mod, mod,
                  pl.BlockSpec(w_a.shape, lambda i: (0, 0)), pl.BlockSpec(w_b.shape, lambda i: (0, 0))],
        out_specs=[row, row],
        out_shape=[jax.ShapeDtypeStruct((N, D), f32), jax.ShapeDtypeStruct((N, D), f32)],
        compiler_params=pltpu.CompilerParams(dimension_semantics=("arbitrary",),
                                             vmem_limit_bytes=_PROJ_VMEM_LIMIT),
        name="out_projection",
    )(ya, yb, x, gate[:, None, :], shift[:, None, :], scale[:, None, :], w_a, w_b)


def _dsa_kernel(q_ref, kv_ref, iq_ref, ik_ref, iw_ref, qg_ref, kg_ref, wbd_ref, tri_ref, o_ref,
                kaug_scr, ikb_scr, key_scr, qaug_scr, *head_scr, S, TQ, TK, topk):
    qi = pl.program_id(1)
    scale = A_LAT ** -0.5
    idx_scale = (IDX_DIM ** -0.5) * (IDX_HEADS ** -0.5)

    @pl.when(qi == 0)
    def _prep_keys():
        kv = kv_ref[0]
        kn = kv * lax.rsqrt(jnp.mean(kv * kv, -1, keepdims=True) + EPS) * kg_ref[...]
        pos = lax.broadcasted_iota(jnp.int32, (S, A_LAT), 0)
        lane = lax.broadcasted_iota(jnp.int32, (S, A_LAT), 1)
        hi = (pos >> 6).astype(f32)
        lo = (pos & 63).astype(f32)
        extra = jnp.where(lane == 0, hi, jnp.where(lane == 1, lo, jnp.where(lane == SUM_LANE, 1.0, 0.0)))
        kaug_scr[:, :A_LAT] = kn.astype(bf16)
        kaug_scr[:, A_LAT:] = extra.astype(bf16)
        ikb_scr[...] = ik_ref[0].astype(bf16)

    q = q_ref[0]
    lane = lax.broadcasted_iota(jnp.int32, (TQ, A_LAT), 1)
    for h in range(A_HEADS):
        qh = q[:, h * A_LAT:(h + 1) * A_LAT]
        qn = qh * lax.rsqrt(jnp.mean(qh * qh, -1, keepdims=True) + EPS) * qg_ref[...] * scale
        slope = 2.0 ** (-8.0 * (h + 1) / A_HEADS)
        extra = jnp.where(lane == 0, slope * 64.0, jnp.where(lane == 1, slope, 0.0))
        qaug_scr[h, :, :A_LAT] = qn.astype(bf16)
        qaug_scr[h, :, A_LAT:] = extra.astype(bf16)

    nchunks = (qi * TQ + TQ + TK - 1) // TK
    qpos = qi * TQ + lax.broadcasted_iota(jnp.int32, (TQ, 1), 0)
    colb = lax.broadcasted_iota(jnp.int32, (TQ, TK), 1)
    iq = iq_ref[0]
    iqb = [iq[:, h * IDX_DIM:(h + 1) * IDX_DIM].astype(bf16) for h in range(IDX_HEADS)]
    iw = iw_ref[0]

    def score_chunk(c, carry):
        off = pl.multiple_of(c * TK, TK)
        ikc = ikb_scr[pl.ds(off, TK), :]
        acc = jnp.zeros((TQ, TK), f32)
        for h in range(IDX_HEADS):
            lg = lax.dot_general(iqb[h], ikc, (((1,), (1,)), ((), ())), preferred_element_type=f32)
            acc = acc + jnp.maximum(lg, 0.0) * iw[:, h:h + 1]
        sc = acc * idx_scale
        sc = jnp.where(colb + off <= qpos, sc, -jnp.inf)
        bits = pltpu.bitcast(sc, jnp.int32)
        key_scr[:, pl.ds(off, TK)] = jnp.where(bits < 0, bits ^ jnp.int32(0x7FFFFFFF), bits)
        return carry

    lax.fori_loop(0, nchunks, score_chunk, 0)

    def count(pred_fn):
        def body(c, cnt):
            off = pl.multiple_of(c * TK, TK)
            hit = pred_fn(key_scr[:, pl.ds(off, TK)]).astype(jnp.int32)
            part = hit[:, 0:128]
            for j in range(1, TK // 128):
                part = part + hit[:, j * 128:(j + 1) * 128]
            return cnt + part
        cnt = lax.fori_loop(0, nchunks, body, jnp.zeros((TQ, 128), jnp.int32))
        return jnp.sum(cnt, axis=1, keepdims=True)

    def bit_step(i, T):
        cand = T + lax.shift_left(jnp.int32(1), jnp.int32(31) - i)
        cnt = count(lambda k: k >= cand)
        return jnp.where(cnt >= topk, cand, T)

    T = lax.fori_loop(0, 32, bit_step, jnp.full((TQ, 1), INT_MIN, jnp.int32))
    n_gt = count(lambda k: k > T)
    room = (topk - n_gt).astype(f32)

    m_scr, acc_scr = head_scr[:A_HEADS], head_scr[A_HEADS:]
    for h in range(A_HEADS):
        m_scr[h][...] = jnp.full((TQ, 1), -jnp.inf, f32)
        acc_scr[h][...] = jnp.zeros((TQ, 2 * A_LAT), f32)

    def attend_chunk(c, ties_before):
        off = pl.multiple_of(c * TK, TK)
        keyc = key_scr[:, pl.ds(off, TK)]
        eq = keyc == T
        eqf = jnp.where(eq, 1.0, 0.0)
        pref = jnp.dot(eqf.astype(bf16), tri_ref[...], preferred_element_type=f32)
        sel = (keyc > T) | (eq & (pref + ties_before < room))
        sel = sel & (colb + off <= qpos)
        kc = kaug_scr[pl.ds(off, TK), :]
        for h in range(A_HEADS):
            s = lax.dot_general(qaug_scr[h], kc, (((1,), (1,)), ((), ())), preferred_element_type=f32)
            s = jnp.where(sel, s, -jnp.inf)
            m_old = m_scr[h][...]
            m_new = jnp.maximum(m_old, jnp.max(s, axis=1, keepdims=True))
            m_safe = jnp.where(m_new == -jnp.inf, 0.0, m_new)
            p = jnp.exp(s - m_safe)
            alpha = jnp.exp(m_old - m_safe)
            acc_scr[h][...] = alpha * acc_scr[h][...] + jnp.dot(p.astype(bf16), kc, preferred_element_type=f32)
            m_scr[h][...] = m_new
        return ties_before + jnp.sum(eqf, axis=1, keepdims=True)

    lax.fori_loop(0, nchunks, attend_chunk, jnp.zeros((TQ, 1), f32))

    heads = []
    for h in range(A_HEADS):
        acc = acc_scr[h][...]
        row_sum = acc[:, A_LAT + SUM_LANE:A_LAT + SUM_LANE + 1]
        heads.append((acc[:, :A_LAT] / row_sum).astype(bf16))
    o = jnp.concatenate(heads, axis=1)
    o_ref[0] = jnp.dot(o, wbd_ref[...], preferred_element_type=f32)


def dsa_attention(q_lat, kv, iq, ik, iw, q_gain, k_gain, w_uv, *, TQ=128, TK=512):
    B, S, _ = q_lat.shape
    topk = min(IDX_TOPK_MAX, S // 4)
    TK = min(TK, S)
    wbd = jnp.zeros((A_HEADS * A_LAT, A_WIDTH), f32)
    for h in range(A_HEADS):
        wbd = wbd.at[h * A_LAT:(h + 1) * A_LAT, h * A_VDIM:(h + 1) * A_VDIM].set(w_uv[h])
    tri = (jnp.arange(TK)[:, None] < jnp.arange(TK)[None, :]).astype(bf16)
    kern = functools.partial(_dsa_kernel, S=S, TQ=TQ, TK=TK, topk=topk)
    return pl.pallas_call(
        kern,
        grid=(B, S // TQ),
        in_specs=[
            pl.BlockSpec((1, TQ, A_HEADS * A_LAT), lambda b, i: (b, i, 0)),
            pl.BlockSpec((1, S, A_LAT), lambda b, i: (b, 0, 0)),
            pl.BlockSpec((1, TQ, IDX_HEADS * IDX_DIM), lambda b, i: (b, i, 0)),
            pl.BlockSpec((1, S, IDX_DIM), lambda b, i: (b, 0, 0)),
            pl.BlockSpec((1, TQ, IDX_HEADS), lambda b, i: (b, i, 0)),
            pl.BlockSpec((1, A_LAT), lambda b, i: (0, 0)),
            pl.BlockSpec((1, A_LAT), lambda b, i: (0, 0)),
            pl.BlockSpec((A_HEADS * A_LAT, A_WIDTH), lambda b, i: (0, 0)),
            pl.BlockSpec((TK, TK), lambda b, i: (0, 0)),
        ],
        out_specs=pl.BlockSpec((1, TQ, A_WIDTH), lambda b, i: (b, i, 0)),
        out_shape=jax.ShapeDtypeStruct((B, S, A_WIDTH), f32),
        scratch_shapes=[
            pltpu.VMEM((S, 2 * A_LAT), bf16),
            pltpu.VMEM((S, IDX_DIM), bf16),
            pltpu.VMEM((TQ, S), jnp.int32),
            pltpu.VMEM((A_HEADS, TQ, 2 * A_LAT), bf16),
            *[pltpu.VMEM((TQ, 1), f32) for _ in range(A_HEADS)],
            *[pltpu.VMEM((TQ, 2 * A_LAT), f32) for _ in range(A_HEADS)],
        ],
        compiler_params=pltpu.CompilerParams(dimension_semantics=("arbitrary", "arbitrary")),
        name="dsa_attention",
    )(q_lat, kv, iq, ik, iw, q_gain.reshape(1, -1), k_gain.reshape(1, -1), wbd.astype(bf16), tri)


CONV_HALO = 8
_HI = lax.Precision.HIGHEST


def _softplus(x):
    return jnp.maximum(x, 0.0) + jnp.log1p(jnp.exp(-jnp.abs(x)))


def _gdn_kernel(q_ref, k_ref, v_ref, z_ref, a_ref, b_ref, at_ref, bt_ref, cw_ref, alog_ref, dtb_ref,
                gain_ref, o_ref, tail_scr, act_scr, *state_scr, SB):
    sj = pl.program_id(1)

    @pl.when(sj == 0)
    def _reset():
        tail_scr[...] = jnp.zeros(tail_scr.shape, f32)
        for st_ref in state_scr:
            st_ref[...] = jnp.zeros((B_DIM, B_DIM), f32)

    cw = cw_ref[...]
    for idx, x_ref in enumerate((q_ref, k_ref, v_ref)):
        x = x_ref[0]
        xc = jnp.concatenate([tail_scr[idx], x], axis=0)
        y = jnp.zeros((SB, B_WIDTH), f32)
        for j in range(CONV_W):
            lo = CONV_HALO - (CONV_W - 1) + j
            y = y + cw[j:j + 1, idx * B_WIDTH:(idx + 1) * B_WIDTH] * xc[lo:lo + SB]
        tail_scr[idx] = x[SB - CONV_HALO:SB]
        act_scr[idx] = y * jax.nn.sigmoid(y)

    ri = lax.broadcasted_iota(jnp.int32, (CHUNK, CHUNK), 0)
    ci = lax.broadcasted_iota(jnp.int32, (CHUNK, CHUNK), 1)
    tril = ri >= ci
    strict = ri > ci
    tril_f = jnp.where(tril, 1.0, 0.0)
    triu_f = jnp.where(ri <= ci, 1.0, 0.0)
    eye = jnp.where(ri == ci, 1.0, 0.0)
    nt = (((1,), (1,)), ((), ()))

    def chunk(c, carry):
        rows = pl.ds(pl.multiple_of(c * CHUNK, CHUNK), CHUNK)
        for h in range(B_HEADS):
            lanes = slice(h * B_DIM, (h + 1) * B_DIM)
            q = act_scr[0, rows, lanes]
            k = act_scr[1, rows, lanes]
            v = act_scr[2, rows, lanes]
            q = q * lax.rsqrt(jnp.sum(q * q, -1, keepdims=True) + EPS) * (B_DIM ** -0.5)
            k = k * lax.rsqrt(jnp.sum(k * k, -1, keepdims=True) + EPS)
            neg_rate = -jnp.exp(alog_ref[0:1, h:h + 1])
            dtb = dtb_ref[0:1, h:h + 1]
            g_col = neg_rate * _softplus(a_ref[0, rows, h:h + 1] + dtb)
            g_row = neg_rate * _softplus(at_ref[0, h, pl.ds(c, 1), :] + dtb)
            beta = jax.nn.sigmoid(b_ref[0, rows, h:h + 1])
            Gc = jnp.dot(tril_f, jnp.broadcast_to(g_col, (CHUNK, CHUNK)), precision=_HI,
                         preferred_element_type=f32)
            Gr = jnp.dot(jnp.broadcast_to(g_row, (CHUNK, CHUNK)), triu_f, precision=_HI,
                         preferred_element_type=f32)
            G = Gc[:, 0:1]
            G_last = Gc[CHUNK - 1:CHUNK, 0:1]
            decay = jnp.exp(jnp.where(tril, Gc - Gr, -jnp.inf))
            kb = k * beta
            kbf = k.astype(bf16)
            L = jnp.where(strict, lax.dot_general(kb.astype(bf16), kbf, nt, preferred_element_type=f32) * decay, 0.0)
            P = eye - L
            M = jnp.dot(L, L, precision=_HI, preferred_element_type=f32)
            for lvl in range(5):
                P = P + jnp.dot(P, M, precision=_HI, preferred_element_type=f32)
                if lvl < 4:
                    M = jnp.dot(M, M, precision=_HI, preferred_element_type=f32)
            Tb = P.astype(bf16)
            eG = jnp.exp(G)
            u = jnp.dot(Tb, (v * beta).astype(bf16), preferred_element_type=f32)
            w = jnp.dot(Tb, (kb * eG).astype(bf16), preferred_element_type=f32)
            attn = lax.dot_general(q.astype(bf16), kbf, nt, preferred_element_type=f32) * decay
            q_dec = q * eG
            k_dec = k * jnp.exp(G_last - G)
            st = state_scr[h][...]
            stb = st.astype(bf16)
            v_new = u - jnp.dot(w.astype(bf16), stb, preferred_element_type=f32)
            vnb = v_new.astype(bf16)
            o = (jnp.dot(q_dec.astype(bf16), stb, preferred_element_type=f32)
                 + jnp.dot(attn.astype(bf16), vnb, preferred_element_type=f32))
            state_scr[h][...] = st * jnp.exp(G_last) + lax.dot_general(
                k_dec.astype(bf16), vnb, (((0,), (0,)), ((), ())), preferred_element_type=f32)
            zh = z_ref[0, rows, lanes]
            o = o * lax.rsqrt(jnp.mean(o * o, -1, keepdims=True) + EPS) * gain_ref[...]
            o_ref[0, rows, lanes] = o * (zh * jax.nn.sigmoid(zh))
        return carry

    lax.fori_loop(0, SB // CHUNK, chunk, 0)


def gated_deltanet(qkvz, a, b, conv_w, a_log, dt_bias, norm_gain, *, SB=512):
    B, S, _ = qkvz.shape
    SB = min(SB, S)
    nch = S // CHUNK
    at = a.transpose(0, 2, 1).reshape(B, B_HEADS, nch, CHUNK)
    bt = b.transpose(0, 2, 1).reshape(B, B_HEADS, nch, CHUNK)
    wide = pl.BlockSpec((1, SB, B_WIDTH), lambda bi, j: (bi, j, 0))
    part = [pl.BlockSpec((1, SB, B_WIDTH), functools.partial(lambda bi, j, n: (bi, j, n), n=n))
            for n in range(4)]
    narrow = pl.BlockSpec((1, SB, B_HEADS), lambda bi, j: (bi, j, 0))
    rowwise = pl.BlockSpec((1, B_HEADS, SB // CHUNK, CHUNK), lambda bi, j: (bi, 0, j, 0))
    return pl.pallas_call(
        functools.partial(_gdn_kernel, SB=SB),
        grid=(B, S // SB),
        in_specs=[*part, narrow, narrow, rowwise, rowwise,
                  pl.BlockSpec((CONV_W, 3 * B_WIDTH), lambda bi, j: (0, 0)),
                  pl.BlockSpec((1, B_HEADS), lambda bi, j: (0, 0)),
                  pl.BlockSpec((1, B_HEADS), lambda bi, j: (0, 0)),
                  pl.BlockSpec((1, B_DIM), lambda bi, j: (0, 0))],
        out_specs=wide,
        out_shape=jax.ShapeDtypeStruct((B, S, B_WIDTH), f32),
        scratch_shapes=[pltpu.VMEM((3, CONV_HALO, B_WIDTH), f32),
                        pltpu.VMEM((3, SB, B_WIDTH), f32),
                        *[pltpu.VMEM((B_DIM, B_DIM), f32) for _ in range(B_HEADS)]],
        compiler_params=pltpu.CompilerParams(dimension_semantics=("arbitrary", "arbitrary")),
        name="gated_deltanet",
    )(qkvz, qkvz, qkvz, qkvz, a, b, at, bt, conv_w,
      a_log.reshape(1, -1), dt_bias.reshape(1, -1), norm_gain.reshape(1, -1))


PEER_SLOTS = PEER_HEADS * PEER_TOPK
PEER_HALF = PEER_QDIM // 2
ROW_WORDS = D_MODEL // 2
ROW_SUBL = ROW_WORDS // 128
STAGE_STRIDE = PEER_SLOTS + 8


def _top16_rows(s, iota_rows, fill, vals_scr, idx_scr, payload=None, pay_scr=None):
    for r in range(PEER_TOPK):
        m = jnp.max(s, axis=0, keepdims=True)
        first = jnp.min(jnp.where(s == m, iota_rows, fill), axis=0, keepdims=True)
        taken = iota_rows == first
        vals_scr[r:r + 1, :] = m
        if payload is None:
            idx_scr[r:r + 1, :] = first
        else:
            pay_scr[r:r + 1, :] = jnp.max(jnp.where(taken, payload, -1), axis=0, keepdims=True)
        s = jnp.where(taken, -jnp.inf, s)


def _peer_route_kernel(h_ref, wq_ref, sk_ref, off_ref, gate_ref,
                       v1_scr, i1_scr, v2_scr, i2_scr, cv_scr, ce_scr, eid_scr, gate_scr, *, T):
    q = jnp.dot(h_ref[...].astype(bf16), wq_ref[...], preferred_element_type=f32).astype(bf16)
    gate_scr[...] = jnp.zeros(gate_scr.shape, f32)
    kiota = lax.broadcasted_iota(jnp.int32, (PEER_NKEYS, T), 0)
    sub8 = lax.broadcasted_iota(jnp.int32, (8, T), 0)
    for p in range(PEER_HEADS):
        for half, (vs, is_) in enumerate(((v1_scr, i1_scr), (v2_scr, i2_scr))):
            g = 2 * p + half
            s = lax.dot_general(sk_ref[g], q[:, g * PEER_HALF:(g + 1) * PEER_HALF],
                                (((1,), (1,)), ((), ())), preferred_element_type=f32)
            _top16_rows(s, kiota, PEER_NKEYS, vs, is_)
        v1 = v1_scr[...]
        v2 = v2_scr[...]
        e1 = i1_scr[...] * PEER_NKEYS
        e2 = i2_scr[...]
        cv = [v1[0:1] + v2[0:8], v1[0:1] + v2[8:16]]
        ce = [e1[0:1] + e2[0:8], e1[0:1] + e2[8:16]]
        cf = [sub8, sub8 + 8]
        for a in range(1, 8):
            cv.append(v1[a:a + 1] + v2[0:8])
            ce.append(e1[a:a + 1] + e2[0:8])
            cf.append(sub8 + a * PEER_TOPK)
        cv.append(v1[8:16] + v2[0:1])
        ce.append(e1[8:16] + e2[0:1])
        cf.append((sub8 + 8) * PEER_TOPK)
        cand = jnp.concatenate(cv, axis=0)
        cexp = jnp.concatenate(ce, axis=0)
        cflat = jnp.concatenate(cf, axis=0)
        _top16_rows(cand, cflat, PEER_TOPK * PEER_TOPK, cv_scr, None, payload=cexp, pay_scr=ce_scr)
        top = cv_scr[...]
        ex = jnp.exp(top - top[0:1])
        gate_scr[pl.ds(2 * p * PEER_TOPK + 1, PEER_TOPK, stride=2), :] = ex / jnp.sum(ex, axis=0, keepdims=True)
        eid_scr[p * PEER_TOPK:(p + 1) * PEER_TOPK, :] = ce_scr[...] * ROW_SUBL
    off_ref[...] = eid_scr[...].T
    gate_ref[...] = gate_scr[...].T


def peer_route(h, w_query, sub_keys, *, T=128):
    N, D = h.shape
    sk = sub_keys.reshape(PEER_HEADS * 2, PEER_NKEYS, PEER_HALF).astype(bf16)
    kern = functools.partial(_peer_route_kernel, T=T)
    return pl.pallas_call(
        kern,
        grid=(N // T,),
        in_specs=[
            pl.BlockSpec((T, D), lambda i: (i, 0)),
            pl.BlockSpec((D, PEER_HEADS * PEER_QDIM), lambda i: (0, 0)),
            pl.BlockSpec((PEER_HEADS * 2, PEER_NKEYS, PEER_HALF), lambda i: (0, 0, 0)),
        ],
        out_specs=[pl.BlockSpec((T, PEER_SLOTS), lambda i: (i, 0)),
                   pl.BlockSpec((T, 2 * PEER_SLOTS), lambda i: (i, 0))],
        out_shape=[jax.ShapeDtypeStruct((N, PEER_SLOTS), jnp.int32),
                   jax.ShapeDtypeStruct((N, 2 * PEER_SLOTS), f32)],
        scratch_shapes=[
            pltpu.VMEM((PEER_TOPK, T), f32), pltpu.VMEM((PEER_TOPK, T), jnp.int32),
            pltpu.VMEM((PEER_TOPK, T), f32), pltpu.VMEM((PEER_TOPK, T), jnp.int32),
            pltpu.VMEM((PEER_TOPK, T), f32), pltpu.VMEM((PEER_TOPK, T), jnp.int32),
            pltpu.VMEM((PEER_SLOTS, T), jnp.int32), pltpu.VMEM((2 * PEER_SLOTS, T), f32),
        ],
        compiler_params=pltpu.CompilerParams(dimension_semantics=("arbitrary",)),
        name="peer_route",
    )(h, w_query.astype(bf16), sk)


def pack_table(tab):
    bits = lax.bitcast_convert_type(tab.astype(bf16), jnp.uint16).astype(jnp.uint32)
    words = (bits[:, :ROW_WORDS] << 16) | bits[:, ROW_WORDS:]
    return lax.bitcast_convert_type(words, jnp.int32).reshape(tab.shape[0] * ROW_SUBL, 128)


def _gather_rows(off_ref, tok, tab_ref, st):
    for k in range(PEER_SLOTS):
        off = pl.multiple_of(off_ref[tok, k], ROW_SUBL)
        st[pl.ds(k, ROW_SUBL, stride=STAGE_STRIDE), :] = tab_ref[pl.ds(off, ROW_SUBL), :]
    chunks = [st[c * STAGE_STRIDE:c * STAGE_STRIDE + PEER_SLOTS, :] for c in range(ROW_SUBL)]
    return pltpu.bitcast(jnp.concatenate(chunks, axis=1), bf16)


def _peer_u_kernel(off_ref, h_ref, gate_ref, tab_ref, w_ref, stage_scr, *, T):
    row16 = lax.broadcasted_iota(jnp.int32, (16, 2 * PEER_SLOTS), 0) & 7
    odd = (lax.broadcasted_iota(jnp.int32, (8, 2 * PEER_SLOTS), 1) & 1) == 1

    def group(g, carry):
        t0 = pl.multiple_of(g * 8, 8)
        h8 = h_ref[pl.ds(t0, 8), :]
        h16 = jnp.concatenate([h8[:, :ROW_WORDS], h8[:, ROW_WORDS:]], axis=0).astype(bf16)
        acc = jnp.zeros((16, 2 * PEER_SLOTS), f32)
        for i in range(8):
            rows = _gather_rows(off_ref, t0 + i, tab_ref, stage_scr.at[i % 2])
            res = lax.dot_general(h16, rows, (((1,), (1,)), ((), ())), preferred_element_type=f32)
            acc = jnp.where(row16 == i, res, acc)
        dots = acc[0:8] + pltpu.roll(acc[8:16], 1, axis=1)
        gelu = 0.5 * dots * (1.0 + lax.erf(dots * (2.0 ** -0.5)))
        w_ref[pl.ds(t0, 8), :] = jnp.where(odd, gelu * gate_ref[pl.ds(t0, 8), :], 0.0)
        return carry

    lax.fori_loop(0, T // 8, group, 0)


def _peer_v_kernel(off_ref, w_ref, x_ref, g2_ref, tab_ref, o_ref, stage_scr, *, T):
    row16 = lax.broadcasted_iota(jnp.int32, (16, ROW_WORDS), 0) & 7

    def group(g, carry):
        t0 = pl.multiple_of(g * 8, 8)
        w8 = w_ref[pl.ds(t0, 8), :]
        w16 = jnp.concatenate([w8, pltpu.roll(w8, 2 * PEER_SLOTS - 1, axis=1)], axis=0).astype(bf16)
        acc = jnp.zeros((16, ROW_WORDS), f32)
        for i in range(8):
            rows = _gather_rows(off_ref, t0 + i, tab_ref, stage_scr.at[i % 2])
            res = jnp.dot(w16, rows, preferred_element_type=f32)
            acc = jnp.where(row16 == i, res, acc)
        y = jnp.concatenate([acc[0:8], acc[8:16]], axis=1)
        o_ref[pl.ds(t0, 8), :] = x_ref[pl.ds(t0, 8), :] + g2_ref[0] * y
        return carry

    lax.fori_loop(0, T // 8, group, 0)


_TABLE_VMEM_LIMIT = 56 * 1024 * 1024


def _table_spec():
    return pl.BlockSpec((N_EXPERTS * ROW_SUBL, 128), lambda i: (0, 0), pipeline_mode=pl.Buffered(1))


def _stage_scratch():
    return pltpu.VMEM((2, ROW_SUBL * STAGE_STRIDE, 128), jnp.int32)


def peer_u(off, h, gate, u_pk, *, T=128):
    N, D = h.shape
    return pl.pallas_call(
        functools.partial(_peer_u_kernel, T=T),
        grid=(N // T,),
        in_specs=[
            pl.BlockSpec((T, PEER_SLOTS), lambda i: (i, 0), memory_space=pltpu.SMEM),
            pl.BlockSpec((T, D), lambda i: (i, 0)),
            pl.BlockSpec((T, 2 * PEER_SLOTS), lambda i: (i, 0)),
            _table_spec(),
        ],
        out_specs=pl.BlockSpec((T, 2 * PEER_SLOTS), lambda i: (i, 0)),
        out_shape=jax.ShapeDtypeStruct((N, 2 * PEER_SLOTS), f32),
        scratch_shapes=[_stage_scratch()],
        compiler_params=pltpu.CompilerParams(dimension_semantics=("arbitrary",),
                                             vmem_limit_bytes=_TABLE_VMEM_LIMIT),
        name="peer_u",
    )(off, h, gate, u_pk)


def peer_v(off, w, x, g2, v_pk, *, T=128):
    N, D = x.shape
    per_batch = N // g2.shape[0]
    return pl.pallas_call(
        functools.partial(_peer_v_kernel, T=T),
        grid=(N // T,),
        in_specs=[
            pl.BlockSpec((T, PEER_SLOTS), lambda i: (i, 0), memory_space=pltpu.SMEM),
            pl.BlockSpec((T, 2 * PEER_SLOTS), lambda i: (i, 0)),
            pl.BlockSpec((T, D), lambda i: (i, 0)),
            pl.BlockSpec((1, 1, D), lambda i: (i * T // per_batch, 0, 0)),
            _table_spec(),
        ],
        out_specs=pl.BlockSpec((T, D), lambda i: (i, 0)),
        out_shape=jax.ShapeDtypeStruct((N, D), f32),
        scratch_shapes=[_stage_scratch()],
        compiler_params=pltpu.CompilerParams(dimension_semantics=("arbitrary",),
                                             vmem_limit_bytes=_TABLE_VMEM_LIMIT),
        name="peer_v",
    )(off, w, x, g2.reshape(g2.shape[0], 1, D), v_pk)


def kernel(x, c, w_ada, b_ada, w_in, q_gain, k_gain, w_uv, conv_w, a_log, dt_bias,
           gdn_gain, w_out, w_query, sub_keys, u_tab, v_tab):
    B, S, D = x.shape
    xf = x.reshape(B * S, D)
    for l in range(DEPTH):
        sh1, sc1, g1, sh2, sc2, g2 = jnp.split(adaln(c, w_ada[l], b_ada[l]), 6, axis=-1)
        aq, akv, iq, small, qkvz = in_projection(xf, sh1, sc1, w_in[l])
        small = small.reshape(B, S, -1)
        ik, iw, ba, bb = (small[..., lo:hi] for lo, hi in (SMALL_IK, SMALL_IW, SMALL_BA, SMALL_BB))
        ya = dsa_attention(aq.reshape(B, S, -1), akv.reshape(B, S, -1), iq.reshape(B, S, -1), ik, iw,
                           q_gain[l], k_gain[l], w_uv[l])
        yb = gated_deltanet(qkvz.reshape(B, S, -1), ba, bb, conv_w[l], a_log[l], dt_bias[l], gdn_gain[l])
        x1, h2 = out_projection(ya.reshape(B * S, -1), yb.reshape(B * S, -1), xf, g1, sh2, sc2, w_out[l])
        off, gate = peer_route(h2, w_query[l], sub_keys[l])
        w = peer_u(off, h2, gate, pack_table(u_tab[l]))
        xf = peer_v(off, w, x1, g2, pack_table(v_tab[l]))
    return xf.reshape(B, S, D)
```

```python
import functools
import math
import jax
import jax.numpy as jnp
from jax import lax
import numpy as np
from jax.experimental import pallas as pl
from jax.experimental.pallas import tpu as pltpu

D_MODEL = 1024
BATCH = 8
SEQ = 4096
DEPTH = 1

EPS = 1e-6
A_HEADS = 8
A_LAT = 128
A_VDIM = 64
IDX_HEADS = 4
IDX_DIM = 64
IDX_TOPK_MAX = 256
Q_BLOCK = 128
B_HEADS = 4
B_DIM = 128
CONV_W = 4
CHUNK = 64
PEER_HEADS = 8
PEER_NKEYS = 128
PEER_QDIM = 256
PEER_TOPK = 16
PEER_BLOCK = 128
N_EXPERTS = PEER_NKEYS * PEER_NKEYS

A_WIDTH = A_HEADS * A_VDIM
B_WIDTH = B_HEADS * B_DIM
MIX_WIDTH = A_WIDTH + B_WIDTH
IN_SPLITS = (A_HEADS * A_LAT, A_LAT, IDX_HEADS * IDX_DIM, IDX_DIM, IDX_HEADS,
             B_WIDTH, B_WIDTH, B_WIDTH, B_WIDTH, B_HEADS, B_HEADS)
IN_WIDTH = sum(IN_SPLITS)


f32 = jnp.float32
bf16 = jnp.bfloat16
INT_MIN = -2 ** 31
SUM_LANE = 2
_PROJ_VMEM_LIMIT = 48 * 1024 * 1024


def _rms_modulate(x, shift, scale):
    xn = x * lax.rsqrt(jnp.mean(x * x, -1, keepdims=True) + EPS)
    return xn * (1.0 + scale) + shift


def _adaln_kernel(c_ref, w_ref, b_ref, o_ref):
    c = c_ref[...]
    s = (c * jax.nn.sigmoid(c)).astype(bf16)
    o_ref[...] = jnp.dot(s, w_ref[...].astype(bf16), preferred_element_type=f32) + b_ref[...]


def adaln(c, w, b, *, tn=512):
    B, D = c.shape
    N = w.shape[1]
    return pl.pallas_call(
        _adaln_kernel,
        grid=(N // tn,),
        in_specs=[pl.BlockSpec((B, D), lambda j: (0, 0)),
                  pl.BlockSpec((D, tn), lambda j: (0, j)),
                  pl.BlockSpec((1, tn), lambda j: (0, j))],
        out_specs=pl.BlockSpec((B, tn), lambda j: (0, j)),
        out_shape=jax.ShapeDtypeStruct((B, N), f32),
        name="adaln",
    )(c, w, b.reshape(1, N))


SMALL_IK = (0, IDX_DIM)
SMALL_IW = (IDX_DIM, IDX_DIM + IDX_HEADS)
SMALL_BA = (IDX_DIM + IDX_HEADS, IDX_DIM + IDX_HEADS + B_HEADS)
SMALL_BB = (IDX_DIM + IDX_HEADS + B_HEADS, IDX_DIM + IDX_HEADS + 2 * B_HEADS)


def _inproj_kernel(x_ref, sh_ref, sc_ref, wq_ref, wkv_ref, wiq_ref, wsm_ref, wb_ref,
                   oq_ref, okv_ref, oiq_ref, osm_ref, ob_ref):
    h = _rms_modulate(x_ref[...], sh_ref[0], sc_ref[0]).astype(bf16)
    for w_ref, o_ref in ((wq_ref, oq_ref), (wkv_ref, okv_ref), (wiq_ref, oiq_ref),
                         (wsm_ref, osm_ref), (wb_ref, ob_ref)):
        o_ref[...] = jnp.dot(h, w_ref[...], preferred_element_type=f32)


def in_projection(x, shift, scale, w_in, *, tm=512):
    N, D = x.shape
    per_batch = N // shift.shape[0]
    o = [int(v) for v in np.cumsum((0,) + IN_SPLITS)]
    wb = w_in.astype(bf16)
    w_q, w_kv, w_iq = wb[:, o[0]:o[1]], wb[:, o[1]:o[2]], wb[:, o[2]:o[3]]
    w_small = jnp.concatenate([wb[:, o[3]:o[5]], wb[:, o[9]:o[11]],
                               jnp.zeros((D, 128 - SMALL_BB[1]), bf16)], axis=1)
    w_b = wb[:, o[5]:o[9]]
    weights = (w_q, w_kv, w_iq, w_small, w_b)
    mod = pl.BlockSpec((1, 1, D), lambda i: (i * tm // per_batch, 0, 0))
    return pl.pallas_call(
        _inproj_kernel,
        grid=(N // tm,),
        in_specs=[pl.BlockSpec((tm, D), lambda i: (i, 0)), mod, mod]
                 + [pl.BlockSpec(w.shape, lambda i: (0, 0), pipeline_mode=pl.Buffered(1)) for w in weights],
        out_specs=[pl.BlockSpec((tm, w.shape[1]), lambda i: (i, 0)) for w in weights],
        out_shape=[jax.ShapeDtypeStruct((N, w.shape[1]), f32) for w in weights],
        compiler_params=pltpu.CompilerParams(dimension_semantics=("arbitrary",),
                                             vmem_limit_bytes=_PROJ_VMEM_LIMIT),
        name="in_projection",
    )(x, shift[:, None, :], scale[:, None, :], *weights)


def _outproj_kernel(ya_ref, yb_ref, x_ref, g1_ref, sh_ref, sc_ref, wa_ref, wb_ref, x1_ref, h2_ref):
    mix = (jnp.dot(ya_ref[...].astype(bf16), wa_ref[...], preferred_element_type=f32)
           + jnp.dot(yb_ref[...].astype(bf16), wb_ref[...], preferred_element_type=f32))
    x1 = x_ref[...] + g1_ref[0] * mix
    x1_ref[...] = x1
    h2_ref[...] = _rms_modulate(x1, sh_ref[0], sc_ref[0])


def out_projection(ya, yb, x, gate, shift, scale, w_out, *, tm=512):
    N, D = x.shape
    per_batch = N // gate.shape[0]
    wb = w_out.astype(bf16)
    w_a, w_b = wb[:A_WIDTH], wb[A_WIDTH:]
    mod = pl.BlockSpec((1, 1, D), lambda i: (i * tm // per_batch, 0, 0))
    row = pl.BlockSpec((tm, D), lambda i: (i, 0))
    half = pl.BlockSpec((tm, A_WIDTH), lambda i: (i, 0))
    return pl.pallas_call(
        _outproj_kernel,
        grid=(N // tm,),
        in_specs=[half, half, row, mod, mod, mod,
                  pl.BlockSpec(w_a.shape, lambda i: (0, 0)), pl.BlockSpec(w_b.shape, lambda i: (0, 0))],
        out_specs=[row, row],
        out_shape=[jax.ShapeDtypeStruct((N, D), f32), jax.ShapeDtypeStruct((N, D), f32)],
        compiler_params=pltpu.CompilerParams(dimension_semantics=("arbitrary",),
                                             vmem_limit_bytes=_PROJ_VMEM_LIMIT),
        name="out_projection",
    )(ya, yb, x, gate[:, None, :], shift[:, None, :], scale[:, None, :], w_a, w_b)


def _dsa_kernel(q_ref, kv_ref, iq_ref, ik_ref, iw_ref, qg_ref, kg_ref, wbd_ref, tri_ref, o_ref,
                kaug_scr, ikb_scr, key_scr, qaug_scr, *head_scr, S, TQ, TK, topk):
    qi = pl.program_id(1)
    scale = A_LAT ** -0.5
    idx_scale = (IDX_DIM ** -0.5) * (IDX_HEADS ** -0.5)

    @pl.when(qi == 0)
    def _prep_keys():
        kv = kv_ref[0]
        kn = kv * lax.rsqrt(jnp.mean(kv * kv, -1, keepdims=True) + EPS) * kg_ref[...]
        pos = lax.broadcasted_iota(jnp.int32, (S, A_LAT), 0)
        lane = lax.broadcasted_iota(jnp.int32, (S, A_LAT), 1)
        hi = (pos >> 6).astype(f32)
        lo = (pos & 63).astype(f32)
        extra = jnp.where(lane == 0, hi, jnp.where(lane == 1, lo, jnp.where(lane == SUM_LANE, 1.0, 0.0)))
        kaug_scr[:, :A_LAT] = kn.astype(bf16)
        kaug_scr[:, A_LAT:] = extra.astype(bf16)
        ikb_scr[...] = ik_ref[0].astype(bf16)

    q = q_ref[0]
    lane = lax.broadcasted_iota(jnp.int32, (TQ, A_LAT), 1)
    for h in range(A_HEADS):
        qh = q[:, h * A_LAT:(h + 1) * A_LAT]
        qn = qh * lax.rsqrt(jnp.mean(qh * qh, -1, keepdims=True) + EPS) * qg_ref[...] * scale
        slope = 2.0 ** (-8.0 * (h + 1) / A_HEADS)
        extra = jnp.where(lane == 0, slope * 64.0, jnp.where(lane == 1, slope, 0.0))
        qaug_scr[h * TQ:(h + 1) * TQ, :A_LAT] = qn.astype(bf16)
        qaug_scr[h * TQ:(h + 1) * TQ, A_LAT:] = extra.astype(bf16)

    nchunks = (qi * TQ + TQ + TK - 1) // TK
    qpos = qi * TQ + lax.broadcasted_iota(jnp.int32, (TQ, 1), 0)
    colb = lax.broadcasted_iota(jnp.int32, (TQ, TK), 1)
    iq = iq_ref[0]
    iqb = [iq[:, h * IDX_DIM:(h + 1) * IDX_DIM].astype(bf16) for h in range(IDX_HEADS)]
    iw = iw_ref[0]

    def score_chunk(c, carry):
        off = pl.multiple_of(c * TK, TK)
        ikc = ikb_scr[pl.ds(off, TK), :]
        acc = jnp.zeros((TQ, TK), f32)
        for h in range(IDX_HEADS):
            lg = lax.dot_general(iqb[h], ikc, (((1,), (1,)), ((), ())), preferred_element_type=f32)
            acc = acc + jnp.maximum(lg, 0.0) * iw[:, h:h + 1]
        sc = acc * idx_scale
        sc = jnp.where(colb + off <= qpos, sc, -jnp.inf)
        bits = pltpu.bitcast(sc, jnp.int32)
        key_scr[:, pl.ds(off, TK)] = jnp.where(bits < 0, bits ^ jnp.int32(0x7FFFFFFF), bits)
        return carry

    lax.fori_loop(0, nchunks, score_chunk, 0)

    def count(pred_fn):
        def body(c, cnt):
            off = pl.multiple_of(c * TK, TK)
            hit = pred_fn(key_scr[:, pl.ds(off, TK)]).astype(jnp.int32)
            part = hit[:, 0:128]
            for j in range(1, TK // 128):
                part = part + hit[:, j * 128:(j + 1) * 128]
            return cnt + part
        cnt = lax.fori_loop(0, nchunks, body, jnp.zeros((TQ, 128), jnp.int32))
        return jnp.sum(cnt, axis=1, keepdims=True)

    def bit_step(i, T):
        cand = T + lax.shift_left(jnp.int32(1), jnp.int32(31) - i)
        cnt = count(lambda k: k >= cand)
        return jnp.where(cnt >= topk, cand, T)

    T = lax.fori_loop(0, 32, bit_step, jnp.full((TQ, 1), INT_MIN, jnp.int32))
    n_gt = count(lambda k: k > T)
    room = (topk - n_gt).astype(f32)

    m_scr, acc_scr = head_scr[:A_HEADS], head_scr[A_HEADS:]
    for h in range(A_HEADS):
        m_scr[h][...] = jnp.full((TQ, 1), -jnp.inf, f32)
        acc_scr[h][...] = jnp.zeros((TQ, 2 * A_LAT), f32)

    def attend_chunk(c, ties_before):
        off = pl.multiple_of(c * TK, TK)
        keyc = key_scr[:, pl.ds(off, TK)]
        eq = keyc == T
        eqf = jnp.where(eq, 1.0, 0.0)
        pref = jnp.dot(eqf.astype(bf16), tri_ref[...], preferred_element_type=f32)
        sel = (keyc > T) | (eq & (pref + ties_before < room))
        sel = sel & (colb + off <= qpos)
        kc = kaug_scr[pl.ds(off, TK), :]
        scores = lax.dot_general(qaug_scr[...], kc, (((1,), (1,)), ((), ())), preferred_element_type=f32)
        probs, alphas = [], []
        for h in range(A_HEADS):
            s = jnp.where(sel, scores[h * TQ:(h + 1) * TQ], -jnp.inf)
            m_old = m_scr[h][...]
            m_new = jnp.maximum(m_old, jnp.max(s, axis=1, keepdims=True))
            m_safe = jnp.where(m_new == -jnp.inf, 0.0, m_new)
            probs.append(jnp.exp(s - m_safe).astype(bf16))
            alphas.append(jnp.exp(m_old - m_safe))
            m_scr[h][...] = m_new
        for h in range(A_HEADS):
            acc_scr[h][...] = alphas[h] * acc_scr[h][...] + jnp.dot(probs[h], kc, preferred_element_type=f32)
        return ties_before + jnp.sum(eqf, axis=1, keepdims=True)

    lax.fori_loop(0, nchunks, attend_chunk, jnp.zeros((TQ, 1), f32))

    heads = []
    for h in range(A_HEADS):
        acc = acc_scr[h][...]
        row_sum = acc[:, A_LAT + SUM_LANE:A_LAT + SUM_LANE + 1]
        heads.append((acc[:, :A_LAT] / row_sum).astype(bf16))
    o = jnp.concatenate(heads, axis=1)
    o_ref[0] = jnp.dot(o, wbd_ref[...], preferred_element_type=f32)


def dsa_attention(q_lat, kv, iq, ik, iw, q_gain, k_gain, w_uv, *, TQ=128, TK=512):
    B, S, _ = q_lat.shape
    topk = min(IDX_TOPK_MAX, S // 4)
    TK = min(TK, S)
    wbd = jnp.zeros((A_HEADS * A_LAT, A_WIDTH), f32)
    for h in range(A_HEADS):
        wbd = wbd.at[h * A_LAT:(h + 1) * A_LAT, h * A_VDIM:(h + 1) * A_VDIM].set(w_uv[h])
    tri = (jnp.arange(TK)[:, None] < jnp.arange(TK)[None, :]).astype(bf16)
    kern = functools.partial(_dsa_kernel, S=S, TQ=TQ, TK=TK, topk=topk)
    return pl.pallas_call(
        kern,
        grid=(B, S // TQ),
        in_specs=[
            pl.BlockSpec((1, TQ, A_HEADS * A_LAT), lambda b, i: (b, i, 0)),
            pl.BlockSpec((1, S, A_LAT), lambda b, i: (b, 0, 0)),
            pl.BlockSpec((1, TQ, IDX_HEADS * IDX_DIM), lambda b, i: (b, i, 0)),
            pl.BlockSpec((1, S, IDX_DIM), lambda b, i: (b, 0, 0)),
            pl.BlockSpec((1, TQ, IDX_HEADS), lambda b, i: (b, i, 0)),
            pl.BlockSpec((1, A_LAT), lambda b, i: (0, 0)),
            pl.BlockSpec((1, A_LAT), lambda b, i: (0, 0)),
            pl.BlockSpec((A_HEADS * A_LAT, A_WIDTH), lambda b, i: (0, 0)),
            pl.BlockSpec((TK, TK), lambda b, i: (0, 0)),
        ],
        out_specs=pl.BlockSpec((1, TQ, A_WIDTH), lambda b, i: (b, i, 0)),
        out_shape=jax.ShapeDtypeStruct((B, S, A_WIDTH), f32),
        scratch_shapes=[
            pltpu.VMEM((S, 2 * A_LAT), bf16),
            pltpu.VMEM((S, IDX_DIM), bf16),
            pltpu.VMEM((TQ, S), jnp.int32),
            pltpu.VMEM((A_HEADS * TQ, 2 * A_LAT), bf16),
            *[pltpu.VMEM((TQ, 1), f32) for _ in range(A_HEADS)],
            *[pltpu.VMEM((TQ, 2 * A_LAT), f32) for _ in range(A_HEADS)],
        ],
        compiler_params=pltpu.CompilerParams(dimension_semantics=("arbitrary", "arbitrary")),
        name="dsa_attention",
    )(q_lat, kv, iq, ik, iw, q_gain.reshape(1, -1), k_gain.reshape(1, -1), wbd.astype(bf16), tri)


CONV_HALO = 8
_HI = lax.Precision.HIGHEST


def _softplus(x):
    return jnp.maximum(x, 0.0) + jnp.log1p(jnp.exp(-jnp.abs(x)))


def _gdn_kernel(q_ref, k_ref, v_ref, z_ref, a_ref, b_ref, at_ref, bt_ref, cw_ref, alog_ref, dtb_ref,
                gain_ref, o_ref, tail_scr, act_scr, *state_scr, SB):
    sj = pl.program_id(1)

    @pl.when(sj == 0)
    def _reset():
        tail_scr[...] = jnp.zeros(tail_scr.shape, f32)
        for st_ref in state_scr:
            st_ref[...] = jnp.zeros((B_DIM, B_DIM), f32)

    cw = cw_ref[...]
    for idx, x_ref in enumerate((q_ref, k_ref, v_ref)):
        x = x_ref[0]
        xc = jnp.concatenate([tail_scr[idx], x], axis=0)
        y = jnp.zeros((SB, B_WIDTH), f32)
        for j in range(CONV_W):
            lo = CONV_HALO - (CONV_W - 1) + j
            y = y + cw[j:j + 1, idx * B_WIDTH:(idx + 1) * B_WIDTH] * xc[lo:lo + SB]
        tail_scr[idx] = x[SB - CONV_HALO:SB]
        act_scr[idx] = y * jax.nn.sigmoid(y)

    ri = lax.broadcasted_iota(jnp.int32, (CHUNK, CHUNK), 0)
    ci = lax.broadcasted_iota(jnp.int32, (CHUNK, CHUNK), 1)
    tril = ri >= ci
    strict = ri > ci
    tril_f = jnp.where(tril, 1.0, 0.0)
    triu_f = jnp.where(ri <= ci, 1.0, 0.0)
    eye = jnp.where(ri == ci, 1.0, 0.0)
    nt = (((1,), (1,)), ((), ()))

    def chunk(c, carry):
        rows = pl.ds(pl.multiple_of(c * CHUNK, CHUNK), CHUNK)
        H = range(B_HEADS)
        lanes = [slice(h * B_DIM, (h + 1) * B_DIM) for h in H]
        hp = dict(precision=_HI, preferred_element_type=f32)
        lp = dict(preferred_element_type=f32)
        q = [act_scr[0, rows, lanes[h]] for h in H]
        k = [act_scr[1, rows, lanes[h]] for h in H]
        v = [act_scr[2, rows, lanes[h]] for h in H]
        q = [x * lax.rsqrt(jnp.sum(x * x, -1, keepdims=True) + EPS) * (B_DIM ** -0.5) for x in q]
        k = [x * lax.rsqrt(jnp.sum(x * x, -1, keepdims=True) + EPS) for x in k]
        neg_rate = [-jnp.exp(alog_ref[0:1, h:h + 1]) for h in H]
        dtb = [dtb_ref[0:1, h:h + 1] for h in H]
        g_col = [neg_rate[h] * _softplus(a_ref[0, rows, h:h + 1] + dtb[h]) for h in H]
        g_row = [neg_rate[h] * _softplus(at_ref[0, h, pl.ds(c, 1), :] + dtb[h]) for h in H]
        beta = [jax.nn.sigmoid(b_ref[0, rows, h:h + 1]) for h in H]
        Gc = [jnp.dot(tril_f, jnp.broadcast_to(g_col[h], (CHUNK, CHUNK)), **hp) for h in H]
        Gr = [jnp.dot(jnp.broadcast_to(g_row[h], (CHUNK, CHUNK)), triu_f, **hp) for h in H]
        G = [x[:, 0:1] for x in Gc]
        G_last = [x[CHUNK - 1:CHUNK, 0:1] for x in Gc]
        decay = [jnp.exp(jnp.where(tril, Gc[h] - Gr[h], -jnp.inf)) for h in H]
        kb = [k[h] * beta[h] for h in H]
        kbf = [x.astype(bf16) for x in k]
        L = [jnp.where(strict, lax.dot_general(kb[h].astype(bf16), kbf[h], nt, **lp) * decay[h], 0.0) for h in H]
        P = [eye - x for x in L]
        M = [jnp.dot(x, x, **hp) for x in L]
        for lvl in range(5):
            P = [P[h] + jnp.dot(P[h], M[h], **hp) for h in H]
            if lvl < 4:
                M = [jnp.dot(x, x, **hp) for x in M]
        Tb = [x.astype(bf16) for x in P]
        eG = [jnp.exp(x) for x in G]
        u = [jnp.dot(Tb[h], (v[h] * beta[h]).astype(bf16), **lp) for h in H]
        w = [jnp.dot(Tb[h], (kb[h] * eG[h]).astype(bf16), **lp) for h in H]
        attn = [lax.dot_general(q[h].astype(bf16), kbf[h], nt, **lp) * decay[h] for h in H]
        q_dec = [q[h] * eG[h] for h in H]
        k_dec = [k[h] * jnp.exp(G_last[h] - G[h]) for h in H]
        st = [state_scr[h][...] for h in H]
        stb = [x.astype(bf16) for x in st]
        v_new = [u[h] - jnp.dot(w[h].astype(bf16), stb[h], **lp) for h in H]
        vnb = [x.astype(bf16) for x in v_new]
        o = [jnp.dot(q_dec[h].astype(bf16), stb[h], **lp) + jnp.dot(attn[h].astype(bf16), vnb[h], **lp) for h in H]
        for h in H:
            state_scr[h][...] = st[h] * jnp.exp(G_last[h]) + lax.dot_general(
                k_dec[h].astype(bf16), vnb[h], (((0,), (0,)), ((), ())), **lp)
        for h in H:
            zh = z_ref[0, rows, lanes[h]]
            on = o[h] * lax.rsqrt(jnp.mean(o[h] * o[h], -1, keepdims=True) + EPS) * gain_ref[...]
            o_ref[0, rows, lanes[h]] = on * (zh * jax.nn.sigmoid(zh))
        return carry

    lax.fori_loop(0, SB // CHUNK, chunk, 0)


def gated_deltanet(qkvz, a, b, conv_w, a_log, dt_bias, norm_gain, *, SB=512):
    B, S, _ = qkvz.shape
    SB = min(SB, S)
    nch = S // CHUNK
    at = a.transpose(0, 2, 1).reshape(B, B_HEADS, nch, CHUNK)
    bt = b.transpose(0, 2, 1).reshape(B, B_HEADS, nch, CHUNK)
    wide = pl.BlockSpec((1, SB, B_WIDTH), lambda bi, j: (bi, j, 0))
    part = [pl.BlockSpec((1, SB, B_WIDTH), functools.partial(lambda bi, j, n: (bi, j, n), n=n))
            for n in range(4)]
    narrow = pl.BlockSpec((1, SB, B_HEADS), lambda bi, j: (bi, j, 0))
    rowwise = pl.BlockSpec((1, B_HEADS, SB // CHUNK, CHUNK), lambda bi, j: (bi, 0, j, 0))
    return pl.pallas_call(
        functools.partial(_gdn_kernel, SB=SB),
        grid=(B, S // SB),
        in_specs=[*part, narrow, narrow, rowwise, rowwise,
                  pl.BlockSpec((CONV_W, 3 * B_WIDTH), lambda bi, j: (0, 0)),
                  pl.BlockSpec((1, B_HEADS), lambda bi, j: (0, 0)),
                  pl.BlockSpec((1, B_HEADS), lambda bi, j: (0, 0)),
                  pl.BlockSpec((1, B_DIM), lambda bi, j: (0, 0))],
        out_specs=wide,
        out_shape=jax.ShapeDtypeStruct((B, S, B_WIDTH), f32),
        scratch_shapes=[pltpu.VMEM((3, CONV_HALO, B_WIDTH), f32),
                        pltpu.VMEM((3, SB, B_WIDTH), f32),
                        *[pltpu.VMEM((B_DIM, B_DIM), f32) for _ in range(B_HEADS)]],
        compiler_params=pltpu.CompilerParams(dimension_semantics=("arbitrary", "arbitrary")),
        name="gated_deltanet",
    )(qkvz, qkvz, qkvz, qkvz, a, b, at, bt, conv_w,
      a_log.reshape(1, -1), dt_bias.reshape(1, -1), norm_gain.reshape(1, -1))


PEER_SLOTS = PEER_HEADS * PEER_TOPK
PEER_HALF = PEER_QDIM // 2
ROW_WORDS = D_MODEL // 2
ROW_SUBL = ROW_WORDS // 128
STAGE_STRIDE = PEER_SLOTS + 8


def _top16_rows(s, iota_rows, fill, vals_scr, idx_scr, payload=None, pay_scr=None):
    for r in range(PEER_TOPK):
        m = jnp.max(s, axis=0, keepdims=True)
        first = jnp.min(jnp.where(s == m, iota_rows, fill), axis=0, keepdims=True)
        taken = iota_rows == first
        vals_scr[r:r + 1, :] = m
        if payload is None:
            idx_scr[r:r + 1, :] = first
        else:
            pay_scr[r:r + 1, :] = jnp.max(jnp.where(taken, payload, -1), axis=0, keepdims=True)
        s = jnp.where(taken, -jnp.inf, s)


def _peer_route_kernel(h_ref, wq_ref, sk_ref, off_ref, gate_ref,
                       v1_scr, i1_scr, v2_scr, i2_scr, cv_scr, ce_scr, eid_scr, gate_scr, *, T):
    q = jnp.dot(h_ref[...].astype(bf16), wq_ref[...], preferred_element_type=f32).astype(bf16)
    gate_scr[...] = jnp.zeros(gate_scr.shape, f32)
    kiota = lax.broadcasted_iota(jnp.int32, (PEER_NKEYS, T), 0)
    sub8 = lax.broadcasted_iota(jnp.int32, (8, T), 0)
    for p in range(PEER_HEADS):
        for half, (vs, is_) in enumerate(((v1_scr, i1_scr), (v2_scr, i2_scr))):
            g = 2 * p + half
            s = lax.dot_general(sk_ref[g], q[:, g * PEER_HALF:(g + 1) * PEER_HALF],
                                (((1,), (1,)), ((), ())), preferred_element_type=f32)
            _top16_rows(s, kiota, PEER_NKEYS, vs, is_)
        v1 = v1_scr[...]
        v2 = v2_scr[...]
        e1 = i1_scr[...] * PEER_NKEYS
        e2 = i2_scr[...]
        cv = [v1[0:1] + v2[0:8], v1[0:1] + v2[8:16]]
        ce = [e1[0:1] + e2[0:8], e1[0:1] + e2[8:16]]
        cf = [sub8, sub8 + 8]
        for a in range(1, 8):
            cv.append(v1[a:a + 1] + v2[0:8])
            ce.append(e1[a:a + 1] + e2[0:8])
            cf.append(sub8 + a * PEER_TOPK)
        cv.append(v1[8:16] + v2[0:1])
        ce.append(e1[8:16] + e2[0:1])
        cf.append((sub8 + 8) * PEER_TOPK)
        cand = jnp.concatenate(cv, axis=0)
        cexp = jnp.concatenate(ce, axis=0)
        cflat = jnp.concatenate(cf, axis=0)
        _top16_rows(cand, cflat, PEER_TOPK * PEER_TOPK, cv_scr, None, payload=cexp, pay_scr=ce_scr)
        top = cv_scr[...]
        ex = jnp.exp(top - top[0:1])
        gate_scr[pl.ds(2 * p * PEER_TOPK + 1, PEER_TOPK, stride=2), :] = ex / jnp.sum(ex, axis=0, keepdims=True)
        eid_scr[p * PEER_TOPK:(p + 1) * PEER_TOPK, :] = ce_scr[...] * ROW_SUBL
    off_ref[...] = eid_scr[...].T
    gate_ref[...] = gate_scr[...].T


def peer_route(h, w_query, sub_keys, *, T=128):
    N, D = h.shape
    sk = sub_keys.reshape(PEER_HEADS * 2, PEER_NKEYS, PEER_HALF).astype(bf16)
    kern = functools.partial(_peer_route_kernel, T=T)
    return pl.pallas_call(
        kern,
        grid=(N // T,),
        in_specs=[
            pl.BlockSpec((T, D), lambda i: (i, 0)),
            pl.BlockSpec((D, PEER_HEADS * PEER_QDIM), lambda i: (0, 0)),
            pl.BlockSpec((PEER_HEADS * 2, PEER_NKEYS, PEER_HALF), lambda i: (0, 0, 0)),
        ],
        out_specs=[pl.BlockSpec((T, PEER_SLOTS), lambda i: (i, 0)),
                   pl.BlockSpec((T, 2 * PEER_SLOTS), lambda i: (i, 0))],
        out_shape=[jax.ShapeDtypeStruct((N, PEER_SLOTS), jnp.int32),
                   jax.ShapeDtypeStruct((N, 2 * PEER_SLOTS), f32)],
        scratch_shapes=[
            pltpu.VMEM((PEER_TOPK, T), f32), pltpu.VMEM((PEER_TOPK, T), jnp.int32),
            pltpu.VMEM((PEER_TOPK, T), f32), pltpu.VMEM((PEER_TOPK, T), jnp.int32),
            pltpu.VMEM((PEER_TOPK, T), f32), pltpu.VMEM((PEER_TOPK, T), jnp.int32),
            pltpu.VMEM((PEER_SLOTS, T), jnp.int32), pltpu.VMEM((2 * PEER_SLOTS, T), f32),
        ],
        compiler_params=pltpu.CompilerParams(dimension_semantics=("arbitrary",)),
        name="peer_route",
    )(h, w_query.astype(bf16), sk)


def pack_table(tab):
    bits = lax.bitcast_convert_type(tab.astype(bf16), jnp.uint16).astype(jnp.uint32)
    words = (bits[:, :ROW_WORDS] << 16) | bits[:, ROW_WORDS:]
    return lax.bitcast_convert_type(words, jnp.int32).reshape(tab.shape[0] * ROW_SUBL, 128)


def _gather_rows(off_ref, tok, tab_ref, st):
    tok_offs = off_ref.at[tok]
    for k in range(PEER_SLOTS):
        off = pl.multiple_of(tok_offs[k], ROW_SUBL)
        st[pl.ds(k, ROW_SUBL, stride=STAGE_STRIDE), :] = tab_ref[pl.ds(off, ROW_SUBL), :]
    chunks = [st[c * STAGE_STRIDE:c * STAGE_STRIDE + PEER_SLOTS, :] for c in range(ROW_SUBL)]
    return pltpu.bitcast(jnp.concatenate(chunks, axis=1), bf16)


def _peer_u_kernel(off_ref, h_ref, gate_ref, tab_ref, w_ref, stage_scr, *, T):
    row16 = lax.broadcasted_iota(jnp.int32, (16, 2 * PEER_SLOTS), 0) & 7
    odd = (lax.broadcasted_iota(jnp.int32, (8, 2 * PEER_SLOTS), 1) & 1) == 1

    def group(g, carry):
        t0 = pl.multiple_of(g * 8, 8)
        h8 = h_ref[pl.ds(t0, 8), :]
        h16 = jnp.concatenate([h8[:, :ROW_WORDS], h8[:, ROW_WORDS:]], axis=0).astype(bf16)
        acc = jnp.zeros((16, 2 * PEER_SLOTS), f32)
        for i in range(8):
            rows = _gather_rows(off_ref, t0 + i, tab_ref, stage_scr.at[i % 2])
            res = lax.dot_general(h16, rows, (((1,), (1,)), ((), ())), preferred_element_type=f32)
            acc = jnp.where(row16 == i, res, acc)
        dots = acc[0:8] + pltpu.roll(acc[8:16], 1, axis=1)
        gelu = 0.5 * dots * (1.0 + lax.erf(dots * (2.0 ** -0.5)))
        w_ref[pl.ds(t0, 8), :] = jnp.where(odd, gelu * gate_ref[pl.ds(t0, 8), :], 0.0)
        return carry

    lax.fori_loop(0, T // 8, group, 0)


def _peer_v_kernel(off_ref, w_ref, x_ref, g2_ref, tab_ref, o_ref, stage_scr, *, T):
    row16 = lax.broadcasted_iota(jnp.int32, (16, ROW_WORDS), 0) & 7

    def group(g, carry):
        t0 = pl.multiple_of(g * 8, 8)
        w8 = w_ref[pl.ds(t0, 8), :]
        w16 = jnp.concatenate([w8, pltpu.roll(w8, 2 * PEER_SLOTS - 1, axis=1)], axis=0).astype(bf16)
        acc = jnp.zeros((16, ROW_WORDS), f32)
        for i in range(8):
            rows = _gather_rows(off_ref, t0 + i, tab_ref, stage_scr.at[i % 2])
            res = jnp.dot(w16, rows, preferred_element_type=f32)
            acc = jnp.where(row16 == i, res, acc)
        y = jnp.concatenate([acc[0:8], acc[8:16]], axis=1)
        o_ref[pl.ds(t0, 8), :] = x_ref[pl.ds(t0, 8), :] + g2_ref[0] * y
        return carry

    lax.fori_loop(0, T // 8, group, 0)


_TABLE_VMEM_LIMIT = 56 * 1024 * 1024


def _table_spec():
    return pl.BlockSpec((N_EXPERTS * ROW_SUBL, 128), lambda i: (0, 0), pipeline_mode=pl.Buffered(1))


def _stage_scratch():
    return pltpu.VMEM((2, ROW_SUBL * STAGE_STRIDE, 128), jnp.int32)


def peer_u(off, h, gate, u_pk, *, T=128):
    N, D = h.shape
    return pl.pallas_call(
        functools.partial(_peer_u_kernel, T=T),
        grid=(N // T,),
        in_specs=[
            pl.BlockSpec((T, PEER_SLOTS), lambda i: (i, 0), memory_space=pltpu.SMEM),
            pl.BlockSpec((T, D), lambda i: (i, 0)),
            pl.BlockSpec((T, 2 * PEER_SLOTS), lambda i: (i, 0)),
            _table_spec(),
        ],
        out_specs=pl.BlockSpec((T, 2 * PEER_SLOTS), lambda i: (i, 0)),
        out_shape=jax.ShapeDtypeStruct((N, 2 * PEER_SLOTS), f32),
        scratch_shapes=[_stage_scratch()],
        compiler_params=pltpu.CompilerParams(dimension_semantics=("arbitrary",),
                                             vmem_limit_bytes=_TABLE_VMEM_LIMIT),
        name="peer_u",
    )(off, h, gate, u_pk)


def peer_v(off, w, x, g2, v_pk, *, T=128):
    N, D = x.shape
    per_batch = N // g2.shape[0]
    return pl.pallas_call(
        functools.partial(_peer_v_kernel, T=T),
        grid=(N // T,),
        in_specs=[
            pl.BlockSpec((T, PEER_SLOTS), lambda i: (i, 0), memory_space=pltpu.SMEM),
            pl.BlockSpec((T, 2 * PEER_SLOTS), lambda i: (i, 0)),
            pl.BlockSpec((T, D), lambda i: (i, 0)),
            pl.BlockSpec((1, 1, D), lambda i: (i * T // per_batch, 0, 0)),
            _table_spec(),
        ],
        out_specs=pl.BlockSpec((T, D), lambda i: (i, 0)),
        out_shape=jax.ShapeDtypeStruct((N, D), f32),
        scratch_shapes=[_stage_scratch()],
        compiler_params=pltpu.CompilerParams(dimension_semantics=("arbitrary",),
                                             vmem_limit_bytes=_TABLE_VMEM_LIMIT),
        name="peer_v",
    )(off, w, x, g2.reshape(g2.shape[0], 1, D), v_pk)


def kernel(x, c, w_ada, b_ada, w_in, q_gain, k_gain, w_uv, conv_w, a_log, dt_bias,
           gdn_gain, w_out, w_query, sub_keys, u_tab, v_tab):
    B, S, D = x.shape
    xf = x.reshape(B * S, D)
    for l in range(DEPTH):
        sh1, sc1, g1, sh2, sc2, g2 = jnp.split(adaln(c, w_ada[l], b_ada[l]), 6, axis=-1)
        aq, akv, iq, small, qkvz = in_projection(xf, sh1, sc1, w_in[l])
        small = small.reshape(B, S, -1)
        ik, iw, ba, bb = (small[..., lo:hi] for lo, hi in (SMALL_IK, SMALL_IW, SMALL_BA, SMALL_BB))
        ya = dsa_attention(aq.reshape(B, S, -1), akv.reshape(B, S, -1), iq.reshape(B, S, -1), ik, iw,
                           q_gain[l], k_gain[l], w_uv[l])
        yb = gated_deltanet(qkvz.reshape(B, S, -1), ba, bb, conv_w[l], a_log[l], dt_bias[l], gdn_gain[l])
        x1, h2 = out_projection(ya.reshape(B * S, -1), yb.reshape(B * S, -1), xf, g1, sh2, sc2, w_out[l])
        off, gate = peer_route(h2, w_query[l], sub_keys[l])
        w = peer_u(off, h2, gate, pack_table(u_tab[l]))
        xf = peer_v(off, w, x1, g2, pack_table(v_tab[l]))
    return xf.reshape(B, S, D)
```

```python
import functools
import math
import jax
import jax.numpy as jnp
from jax import lax
import numpy as np
from jax.experimental import pallas as pl
from jax.experimental.pallas import tpu as pltpu

D_MODEL = 1024
BATCH = 8
SEQ = 4096
DEPTH = 1

EPS = 1e-6
A_HEADS = 8
A_LAT = 128
A_VDIM = 64
IDX_HEADS = 4
IDX_DIM = 64
IDX_TOPK_MAX = 256
Q_BLOCK = 128
B_HEADS = 4
B_DIM = 128
CONV_W = 4
CHUNK = 64
PEER_HEADS = 8
PEER_NKEYS = 128
PEER_QDIM = 256
PEER_TOPK = 16
PEER_BLOCK = 128
N_EXPERTS = PEER_NKEYS * PEER_NKEYS

A_WIDTH = A_HEADS * A_VDIM
B_WIDTH = B_HEADS * B_DIM
MIX_WIDTH = A_WIDTH + B_WIDTH
IN_SPLITS = (A_HEADS * A_LAT, A_LAT, IDX_HEADS * IDX_DIM, IDX_DIM, IDX_HEADS,
             B_WIDTH, B_WIDTH, B_WIDTH, B_WIDTH, B_HEADS, B_HEADS)
IN_WIDTH = sum(IN_SPLITS)


f32 = jnp.float32
bf16 = jnp.bfloat16
INT_MIN = -2 ** 31
SUM_LANE = 2
_PROJ_VMEM_LIMIT = 48 * 1024 * 1024


def _rms_modulate(x, shift, scale):
    xn = x * lax.rsqrt(jnp.mean(x * x, -1, keepdims=True) + EPS)
    return xn * (1.0 + scale) + shift


def _adaln_kernel(c_ref, w_ref, b_ref, o_ref):
    c = c_ref[...]
    s = (c * jax.nn.sigmoid(c)).astype(bf16)
    o_ref[...] = jnp.dot(s, w_ref[...].astype(bf16), preferred_element_type=f32) + b_ref[...]


def adaln(c, w, b, *, tn=512):
    B, D = c.shape
    N = w.shape[1]
    return pl.pallas_call(
        _adaln_kernel,
        grid=(N // tn,),
        in_specs=[pl.BlockSpec((B, D), lambda j: (0, 0)),
                  pl.BlockSpec((D, tn), lambda j: (0, j)),
                  pl.BlockSpec((1, tn), lambda j: (0, j))],
        out_specs=pl.BlockSpec((B, tn), lambda j: (0, j)),
        out_shape=jax.ShapeDtypeStruct((B, N), f32),
        name="adaln",
    )(c, w, b.reshape(1, N))


SMALL_IK = (0, IDX_DIM)
SMALL_IW = (IDX_DIM, IDX_DIM + IDX_HEADS)
SMALL_BA = (IDX_DIM + IDX_HEADS, IDX_DIM + IDX_HEADS + B_HEADS)
SMALL_BB = (IDX_DIM + IDX_HEADS + B_HEADS, IDX_DIM + IDX_HEADS + 2 * B_HEADS)


def _inproj_kernel(x_ref, sh_ref, sc_ref, wq_ref, wkv_ref, wiq_ref, wsm_ref, wb_ref,
                   oq_ref, okv_ref, oiq_ref, osm_ref, ob_ref):
    h = _rms_modulate(x_ref[...], sh_ref[0], sc_ref[0]).astype(bf16)
    for w_ref, o_ref in ((wq_ref, oq_ref), (wkv_ref, okv_ref), (wiq_ref, oiq_ref),
                         (wsm_ref, osm_ref), (wb_ref, ob_ref)):
        o_ref[...] = jnp.dot(h, w_ref[...], preferred_element_type=f32)


def in_projection(x, shift, scale, w_in, *, tm=512):
    N, D = x.shape
    per_batch = N // shift.shape[0]
    o = [int(v) for v in np.cumsum((0,) + IN_SPLITS)]
    wb = w_in.astype(bf16)
    w_q, w_kv, w_iq = wb[:, o[0]:o[1]], wb[:, o[1]:o[2]], wb[:, o[2]:o[3]]
    w_small = jnp.concatenate([wb[:, o[3]:o[5]], wb[:, o[9]:o[11]],
                               jnp.zeros((D, 128 - SMALL_BB[1]), bf16)], axis=1)
    w_b = wb[:, o[5]:o[9]]
    weights = (w_q, w_kv, w_iq, w_small, w_b)
    mod = pl.BlockSpec((1, 1, D), lambda i: (i * tm // per_batch, 0, 0))
    return pl.pallas_call(
        _inproj_kernel,
        grid=(N // tm,),
        in_specs=[pl.BlockSpec((tm, D), lambda i: (i, 0)), mod, mod]
                 + [pl.BlockSpec(w.shape, lambda i: (0, 0), pipeline_mode=pl.Buffered(1)) for w in weights],
        out_specs=[pl.BlockSpec((tm, w.shape[1]), lambda i: (i, 0)) for w in weights],
        out_shape=[jax.ShapeDtypeStruct((N, w.shape[1]), f32) for w in weights],
        compiler_params=pltpu.CompilerParams(dimension_semantics=("arbitrary",),
                                             vmem_limit_bytes=_PROJ_VMEM_LIMIT),
        name="in_projection",
    )(x, shift[:, None, :], scale[:, None, :], *weights)


def _outproj_kernel(ya_ref, yb_ref, x_ref, g1_ref, sh_ref, sc_ref, wa_ref, wb_ref, x1_ref, h2_ref):
    mix = (jnp.dot(ya_ref[...].astype(bf16), wa_ref[...], preferred_element_type=f32)
           + jnp.dot(yb_ref[...].astype(bf16), wb_ref[...], preferred_element_type=f32))
    x1 = x_ref[...] + g1_ref[0] * mix
    x1_ref[...] = x1
    h2_ref[...] = _rms_modulate(x1, sh_ref[0], sc_ref[0])


def out_projection(ya, yb, x, gate, shift, scale, w_out, *, tm=512):
    N, D = x.shape
    per_batch = N // gate.shape[0]
    wb = w_out.astype(bf16)
    w_a, w_b = wb[:A_WIDTH], wb[A_WIDTH:]
    mod = pl.BlockSpec((1, 1, D), lambda i: (i * tm // per_batch, 0, 0))
    row = pl.BlockSpec((tm, D), lambda i: (i, 0))
    half = pl.BlockSpec((tm, A_WIDTH), lambda i: (i, 0))
    return pl.pallas_call(
        _outproj_kernel,
        grid=(N // tm,),
        in_specs=[half, half, row, mod, mod, mod,
                  pl.BlockSpec(w_a.shape, lambda i: (0, 0)), pl.BlockSpec(w_b.shape, lambda i: (0, 0))],
        out_specs=[row, row],
        out_shape=[jax.ShapeDtypeStruct((N, D), f32), jax.ShapeDtypeStruct((N, D), f32)],
        compiler_params=pltpu.CompilerParams(dimension_semantics=("arbitrary",),
                                             vmem_limit_bytes=_PROJ_VMEM_LIMIT),
        name="out_projection",
    )(ya, yb, x, gate[:, None, :], shift[:, None, :], scale[:, None, :], w_a, w_b)


def _dsa_kernel(q_ref, kv_ref, iq_ref, ik_ref, iw_ref, qg_ref, kg_ref, wbd_ref, tri_ref, o_ref,
                kaug_scr, ikb_scr, key_scr, qaug_scr, *head_scr, S, TQ, TK, topk):
    qi = pl.program_id(1)
    scale = A_LAT ** -0.5
    idx_scale = (IDX_DIM ** -0.5) * (IDX_HEADS ** -0.5)

    @pl.when(qi == 0)
    def _prep_keys():
        kv = kv_ref[0]
        kn = kv * lax.rsqrt(jnp.mean(kv * kv, -1, keepdims=True) + EPS) * kg_ref[...]
        pos = lax.broadcasted_iota(jnp.int32, (S, A_LAT), 0)
        lane = lax.broadcasted_iota(jnp.int32, (S, A_LAT), 1)
        hi = (pos >> 6).astype(f32)
        lo = (pos & 63).astype(f32)
        extra = jnp.where(lane == 0, hi, jnp.where(lane == 1, lo, jnp.where(lane == SUM_LANE, 1.0, 0.0)))
        kaug_scr[:, :A_LAT] = kn.astype(bf16)
        kaug_scr[:, A_LAT:] = extra.astype(bf16)
        ikb_scr[...] = ik_ref[0].astype(bf16)

    q = q_ref[0]
    lane = lax.broadcasted_iota(jnp.int32, (TQ, A_LAT), 1)
    for h in range(A_HEADS):
        qh = q[:, h * A_LAT:(h + 1) * A_LAT]
        qn = qh * lax.rsqrt(jnp.mean(qh * qh, -1, keepdims=True) + EPS) * qg_ref[...] * scale
        slope = 2.0 ** (-8.0 * (h + 1) / A_HEADS)
        extra = jnp.where(lane == 0, slope * 64.0, jnp.where(lane == 1, slope, 0.0))
        qaug_scr[h * TQ:(h + 1) * TQ, :A_LAT] = qn.astype(bf16)
        qaug_scr[h * TQ:(h + 1) * TQ, A_LAT:] = extra.astype(bf16)

    nchunks = (qi * TQ + TQ + TK - 1) // TK
    qpos = qi * TQ + lax.broadcasted_iota(jnp.int32, (TQ, 1), 0)
    colb = lax.broadcasted_iota(jnp.int32, (TQ, TK), 1)
    iq = iq_ref[0]
    iqb = [iq[:, h * IDX_DIM:(h + 1) * IDX_DIM].astype(bf16) for h in range(IDX_HEADS)]
    iw = iw_ref[0]

    def score_chunk(c, carry):
        off = pl.multiple_of(c * TK, TK)
        ikc = ikb_scr[pl.ds(off, TK), :]
        acc = jnp.zeros((TQ, TK), f32)
        for h in range(IDX_HEADS):
            lg = lax.dot_general(iqb[h], ikc, (((1,), (1,)), ((), ())), preferred_element_type=f32)
            acc = acc + jnp.maximum(lg, 0.0) * iw[:, h:h + 1]
        sc = acc * idx_scale
        sc = jnp.where(colb + off <= qpos, sc, -jnp.inf)
        bits = pltpu.bitcast(sc, jnp.int32)
        key_scr[:, pl.ds(off, TK)] = jnp.where(bits < 0, bits ^ jnp.int32(0x7FFFFFFF), bits)
        return carry

    lax.fori_loop(0, nchunks, score_chunk, 0)

    def count(pred_fn):
        def body(c, cnt):
            off = pl.multiple_of(c * TK, TK)
            hit = pred_fn(key_scr[:, pl.ds(off, TK)]).astype(jnp.int32)
            part = hit[:, 0:128]
            for j in range(1, TK // 128):
                part = part + hit[:, j * 128:(j + 1) * 128]
            return cnt + part
        cnt = lax.fori_loop(0, nchunks, body, jnp.zeros((TQ, 128), jnp.int32))
        return jnp.sum(cnt, axis=1, keepdims=True)

    def bit_step(i, T):
        cand = T + lax.shift_left(jnp.int32(1), jnp.int32(31) - i)
        cnt = count(lambda k: k >= cand)
        return jnp.where(cnt >= topk, cand, T)

    T = lax.fori_loop(0, 32, bit_step, jnp.full((TQ, 1), INT_MIN, jnp.int32))
    n_gt = count(lambda k: k > T)
    room = (topk - n_gt).astype(f32)

    m_scr, acc_scr = head_scr[:A_HEADS], head_scr[A_HEADS:]
    for h in range(A_HEADS):
        m_scr[h][...] = jnp.full((TQ, 1), -jnp.inf, f32)
        acc_scr[h][...] = jnp.zeros((TQ, 2 * A_LAT), f32)

    def attend_chunk(c, ties_before):
        off = pl.multiple_of(c * TK, TK)
        keyc = key_scr[:, pl.ds(off, TK)]
        eq = keyc == T
        eqb = jnp.where(eq, 1.0, 0.0).astype(bf16)
        ties = ties_before
        admit = []
        for j in range(TK // 128):
            blk = slice(j * 128, (j + 1) * 128)
            cnt = jnp.dot(eqb[:, blk], tri_ref[...], preferred_element_type=f32)
            admit.append(eq[:, blk] & (cnt[:, :128] + ties < room))
            ties = ties + cnt[:, 128:]
        sel = (keyc > T) | jnp.concatenate(admit, axis=1)
        sel = sel & (colb + off <= qpos)
        kc = kaug_scr[pl.ds(off, TK), :]
        scores = lax.dot_general(qaug_scr[...], kc, (((1,), (1,)), ((), ())), preferred_element_type=f32)
        probs, alphas = [], []
        for h in range(A_HEADS):
            s = jnp.where(sel, scores[h * TQ:(h + 1) * TQ], -jnp.inf)
            m_old = m_scr[h][...]
            m_new = jnp.maximum(m_old, jnp.max(s, axis=1, keepdims=True))
            m_safe = jnp.where(m_new == -jnp.inf, 0.0, m_new)
            probs.append(jnp.exp(s - m_safe).astype(bf16))
            alphas.append(jnp.exp(m_old - m_safe))
            m_scr[h][...] = m_new
        for h in range(A_HEADS):
            acc_scr[h][...] = alphas[h] * acc_scr[h][...] + jnp.dot(probs[h], kc, preferred_element_type=f32)
        return ties

    lax.fori_loop(0, nchunks, attend_chunk, jnp.zeros((TQ, 128), f32))

    heads = []
    for h in range(A_HEADS):
        acc = acc_scr[h][...]
        row_sum = acc[:, A_LAT + SUM_LANE:A_LAT + SUM_LANE + 1]
        heads.append((acc[:, :A_LAT] / row_sum).astype(bf16))
    o = jnp.concatenate(heads, axis=1)
    o_ref[0] = jnp.dot(o, wbd_ref[...], preferred_element_type=f32)


def dsa_attention(q_lat, kv, iq, ik, iw, q_gain, k_gain, w_uv, *, TQ=128, TK=1024):
    B, S, _ = q_lat.shape
    topk = min(IDX_TOPK_MAX, S // 4)
    TK = min(TK, S)
    wbd = jnp.zeros((A_HEADS * A_LAT, A_WIDTH), f32)
    for h in range(A_HEADS):
        wbd = wbd.at[h * A_LAT:(h + 1) * A_LAT, h * A_VDIM:(h + 1) * A_VDIM].set(w_uv[h])
    blk = jnp.arange(128)
    tri = jnp.concatenate([blk[:, None] < blk[None, :], jnp.ones((128, 128), bool)], axis=1).astype(bf16)
    kern = functools.partial(_dsa_kernel, S=S, TQ=TQ, TK=TK, topk=topk)
    return pl.pallas_call(
        kern,
        grid=(B, S // TQ),
        in_specs=[
            pl.BlockSpec((1, TQ, A_HEADS * A_LAT), lambda b, i: (b, i, 0)),
            pl.BlockSpec((1, S, A_LAT), lambda b, i: (b, 0, 0)),
            pl.BlockSpec((1, TQ, IDX_HEADS * IDX_DIM), lambda b, i: (b, i, 0)),
            pl.BlockSpec((1, S, IDX_DIM), lambda b, i: (b, 0, 0)),
            pl.BlockSpec((1, TQ, IDX_HEADS), lambda b, i: (b, i, 0)),
            pl.BlockSpec((1, A_LAT), lambda b, i: (0, 0)),
            pl.BlockSpec((1, A_LAT), lambda b, i: (0, 0)),
            pl.BlockSpec((A_HEADS * A_LAT, A_WIDTH), lambda b, i: (0, 0)),
            pl.BlockSpec((128, 256), lambda b, i: (0, 0)),
        ],
        out_specs=pl.BlockSpec((1, TQ, A_WIDTH), lambda b, i: (b, i, 0)),
        out_shape=jax.ShapeDtypeStruct((B, S, A_WIDTH), f32),
        scratch_shapes=[
            pltpu.VMEM((S, 2 * A_LAT), bf16),
            pltpu.VMEM((S, IDX_DIM), bf16),
            pltpu.VMEM((TQ, S), jnp.int32),
            pltpu.VMEM((A_HEADS * TQ, 2 * A_LAT), bf16),
            *[pltpu.VMEM((TQ, 1), f32) for _ in range(A_HEADS)],
            *[pltpu.VMEM((TQ, 2 * A_LAT), f32) for _ in range(A_HEADS)],
        ],
        compiler_params=pltpu.CompilerParams(dimension_semantics=("arbitrary", "arbitrary")),
        name="dsa_attention",
    )(q_lat, kv, iq, ik, iw, q_gain.reshape(1, -1), k_gain.reshape(1, -1), wbd.astype(bf16), tri)


CONV_HALO = 8


def _split_bf16(x, terms):
    parts = []
    for _ in range(terms):
        p = x.astype(bf16)
        parts.append(p)
        x = x - p.astype(f32)
    return parts


def _dot_exact_lhs(mask_bf16, x):
    return sum(jnp.dot(mask_bf16, p, preferred_element_type=f32) for p in _split_bf16(x, 3))


def _dot_exact_rhs(x, mask_bf16):
    return sum(jnp.dot(p, mask_bf16, preferred_element_type=f32) for p in _split_bf16(x, 3))


def _dot_3pass(a, b):
    ah, al = _split_bf16(a, 2)
    bh, bl = _split_bf16(b, 2)
    return (jnp.dot(ah, bh, preferred_element_type=f32) + jnp.dot(ah, bl, preferred_element_type=f32)
            + jnp.dot(al, bh, preferred_element_type=f32))


def _softplus(x):
    return jnp.maximum(x, 0.0) + jnp.log1p(jnp.exp(-jnp.abs(x)))


def _gdn_kernel(q_ref, k_ref, v_ref, z_ref, a_ref, b_ref, at_ref, bt_ref, cw_ref, alog_ref, dtb_ref,
                gain_ref, o_ref, tail_scr, act_scr, *state_scr, SB):
    sj = pl.program_id(1)

    @pl.when(sj == 0)
    def _reset():
        tail_scr[...] = jnp.zeros(tail_scr.shape, f32)
        for st_ref in state_scr:
            st_ref[...] = jnp.zeros((B_DIM, B_DIM), f32)

    cw = cw_ref[...]
    for idx, x_ref in enumerate((q_ref, k_ref, v_ref)):
        x = x_ref[0]
        xc = jnp.concatenate([tail_scr[idx], x], axis=0)
        y = jnp.zeros((SB, B_WIDTH), f32)
        for j in range(CONV_W):
            lo = CONV_HALO - (CONV_W - 1) + j
            y = y + cw[j:j + 1, idx * B_WIDTH:(idx + 1) * B_WIDTH] * xc[lo:lo + SB]
        tail_scr[idx] = x[SB - CONV_HALO:SB]
        act_scr[idx] = y * jax.nn.sigmoid(y)

    ri = lax.broadcasted_iota(jnp.int32, (CHUNK, CHUNK), 0)
    ci = lax.broadcasted_iota(jnp.int32, (CHUNK, CHUNK), 1)
    tril = ri >= ci
    strict = ri > ci
    tril_b = jnp.where(tril, 1.0, 0.0).astype(bf16)
    triu_b = jnp.where(ri <= ci, 1.0, 0.0).astype(bf16)
    eye = jnp.where(ri == ci, 1.0, 0.0)
    nt = (((1,), (1,)), ((), ()))

    def chunk(c, carry):
        rows = pl.ds(pl.multiple_of(c * CHUNK, CHUNK), CHUNK)
        H = range(B_HEADS)
        lanes = [slice(h * B_DIM, (h + 1) * B_DIM) for h in H]
        lp = dict(preferred_element_type=f32)
        q = [act_scr[0, rows, lanes[h]] for h in H]
        k = [act_scr[1, rows, lanes[h]] for h in H]
        v = [act_scr[2, rows, lanes[h]] for h in H]
        q = [x * lax.rsqrt(jnp.sum(x * x, -1, keepdims=True) + EPS) * (B_DIM ** -0.5) for x in q]
        k = [x * lax.rsqrt(jnp.sum(x * x, -1, keepdims=True) + EPS) for x in k]
        neg_rate = [-jnp.exp(alog_ref[0:1, h:h + 1]) for h in H]
        dtb = [dtb_ref[0:1, h:h + 1] for h in H]
        g_col = [neg_rate[h] * _softplus(a_ref[0, rows, h:h + 1] + dtb[h]) for h in H]
        g_row = [neg_rate[h] * _softplus(at_ref[0, h, pl.ds(c, 1), :] + dtb[h]) for h in H]
        beta = [jax.nn.sigmoid(b_ref[0, rows, h:h + 1]) for h in H]
        Gc = [_dot_exact_lhs(tril_b, jnp.broadcast_to(g_col[h], (CHUNK, CHUNK))) for h in H]
        Gr = [_dot_exact_rhs(jnp.broadcast_to(g_row[h], (CHUNK, CHUNK)), triu_b) for h in H]
        G = [x[:, 0:1] for x in Gc]
        G_last = [x[CHUNK - 1:CHUNK, 0:1] for x in Gc]
        decay = [jnp.exp(jnp.where(tril, Gc[h] - Gr[h], -jnp.inf)) for h in H]
        kb = [k[h] * beta[h] for h in H]
        kbf = [x.astype(bf16) for x in k]
        L = [jnp.where(strict, lax.dot_general(kb[h].astype(bf16), kbf[h], nt, **lp) * decay[h], 0.0) for h in H]
        P = [eye - x for x in L]
        M = [_dot_3pass(x, x) for x in L]
        for lvl in range(5):
            P = [P[h] + _dot_3pass(P[h], M[h]) for h in H]
            if lvl < 4:
                M = [_dot_3pass(x, x) for x in M]
        Tb = [x.astype(bf16) for x in P]
        eG = [jnp.exp(x) for x in G]
        u = [jnp.dot(Tb[h], (v[h] * beta[h]).astype(bf16), **lp) for h in H]
        w = [jnp.dot(Tb[h], (kb[h] * eG[h]).astype(bf16), **lp) for h in H]
        attn = [lax.dot_general(q[h].astype(bf16), kbf[h], nt, **lp) * decay[h] for h in H]
        q_dec = [q[h] * eG[h] for h in H]
        k_dec = [k[h] * jnp.exp(G_last[h] - G[h]) for h in H]
        st = [state_scr[h][...] for h in H]
        stb = [x.astype(bf16) for x in st]
        v_new = [u[h] - jnp.dot(w[h].astype(bf16), stb[h], **lp) for h in H]
        vnb = [x.astype(bf16) for x in v_new]
        o = [jnp.dot(q_dec[h].astype(bf16), stb[h], **lp) + jnp.dot(attn[h].astype(bf16), vnb[h], **lp) for h in H]
        for h in H:
            state_scr[h][...] = st[h] * jnp.exp(G_last[h]) + lax.dot_general(
                k_dec[h].astype(bf16), vnb[h], (((0,), (0,)), ((), ())), **lp)
        for h in H:
            zh = z_ref[0, rows, lanes[h]]
            on = o[h] * lax.rsqrt(jnp.mean(o[h] * o[h], -1, keepdims=True) + EPS) * gain_ref[...]
            o_ref[0, rows, lanes[h]] = on * (zh * jax.nn.sigmoid(zh))
        return carry

    lax.fori_loop(0, SB // CHUNK, chunk, 0)


def gated_deltanet(qkvz, a, b, conv_w, a_log, dt_bias, norm_gain, *, SB=512):
    B, S, _ = qkvz.shape
    SB = min(SB, S)
    nch = S // CHUNK
    at = a.transpose(0, 2, 1).reshape(B, B_HEADS, nch, CHUNK)
    bt = b.transpose(0, 2, 1).reshape(B, B_HEADS, nch, CHUNK)
    wide = pl.BlockSpec((1, SB, B_WIDTH), lambda bi, j: (bi, j, 0))
    part = [pl.BlockSpec((1, SB, B_WIDTH), functools.partial(lambda bi, j, n: (bi, j, n), n=n))
            for n in range(4)]
    narrow = pl.BlockSpec((1, SB, B_HEADS), lambda bi, j: (bi, j, 0))
    rowwise = pl.BlockSpec((1, B_HEADS, SB // CHUNK, CHUNK), lambda bi, j: (bi, 0, j, 0))
    return pl.pallas_call(
        functools.partial(_gdn_kernel, SB=SB),
        grid=(B, S // SB),
        in_specs=[*part, narrow, narrow, rowwise, rowwise,
                  pl.BlockSpec((CONV_W, 3 * B_WIDTH), lambda bi, j: (0, 0)),
                  pl.BlockSpec((1, B_HEADS), lambda bi, j: (0, 0)),
                  pl.BlockSpec((1, B_HEADS), lambda bi, j: (0, 0)),
                  pl.BlockSpec((1, B_DIM), lambda bi, j: (0, 0))],
        out_specs=wide,
        out_shape=jax.ShapeDtypeStruct((B, S, B_WIDTH), f32),
        scratch_shapes=[pltpu.VMEM((3, CONV_HALO, B_WIDTH), f32),
                        pltpu.VMEM((3, SB, B_WIDTH), f32),
                        *[pltpu.VMEM((B_DIM, B_DIM), f32) for _ in range(B_HEADS)]],
        compiler_params=pltpu.CompilerParams(dimension_semantics=("arbitrary", "arbitrary")),
        name="gated_deltanet",
    )(qkvz, qkvz, qkvz, qkvz, a, b, at, bt, conv_w,
      a_log.reshape(1, -1), dt_bias.reshape(1, -1), norm_gain.reshape(1, -1))


PEER_SLOTS = PEER_HEADS * PEER_TOPK
PEER_HALF = PEER_QDIM // 2
ROW_WORDS = D_MODEL // 2
ROW_SUBL = ROW_WORDS // 128
STAGE_STRIDE = PEER_SLOTS + 8


def _top16_rows(s, iota_rows, fill, vals_scr, idx_scr, payload=None, pay_scr=None):
    for r in range(PEER_TOPK):
        m = jnp.max(s, axis=0, keepdims=True)
        first = jnp.min(jnp.where(s == m, iota_rows, fill), axis=0, keepdims=True)
        taken = iota_rows == first
        vals_scr[r:r + 1, :] = m
        if payload is None:
            idx_scr[r:r + 1, :] = first.astype(jnp.int32)
        else:
            pay_scr[r:r + 1, :] = jnp.max(jnp.where(taken, payload, -1.0), axis=0,
                                          keepdims=True).astype(jnp.int32)
        s = jnp.where(taken, -jnp.inf, s)


def _peer_route_kernel(h_ref, wq_ref, sk_ref, off_ref, gate_ref,
                       v1_scr, i1_scr, v2_scr, i2_scr, cv_scr, ce_scr, eid_scr, gate_scr, *, T):
    q = jnp.dot(h_ref[...].astype(bf16), wq_ref[...], preferred_element_type=f32).astype(bf16)
    gate_scr[...] = jnp.zeros(gate_scr.shape, f32)
    kiota = lax.broadcasted_iota(jnp.int32, (PEER_NKEYS, T), 0).astype(f32)
    sub8 = lax.broadcasted_iota(jnp.int32, (8, T), 0).astype(f32)
    for p in range(PEER_HEADS):
        for half, (vs, is_) in enumerate(((v1_scr, i1_scr), (v2_scr, i2_scr))):
            g = 2 * p + half
            s = lax.dot_general(sk_ref[g], q[:, g * PEER_HALF:(g + 1) * PEER_HALF],
                                (((1,), (1,)), ((), ())), preferred_element_type=f32)
            _top16_rows(s, kiota, PEER_NKEYS, vs, is_)
        v1 = v1_scr[...]
        v2 = v2_scr[...]
        e1 = (i1_scr[...] * PEER_NKEYS).astype(f32)
        e2 = i2_scr[...].astype(f32)
        cv = [v1[0:1] + v2[0:8], v1[0:1] + v2[8:16]]
        ce = [e1[0:1] + e2[0:8], e1[0:1] + e2[8:16]]
        cf = [sub8, sub8 + 8]
        for a in range(1, 8):
            cv.append(v1[a:a + 1] + v2[0:8])
            ce.append(e1[a:a + 1] + e2[0:8])
            cf.append(sub8 + a * PEER_TOPK)
        cv.append(v1[8:16] + v2[0:1])
        ce.append(e1[8:16] + e2[0:1])
        cf.append((sub8 + 8) * PEER_TOPK)
        cand = jnp.concatenate(cv, axis=0)
        cexp = jnp.concatenate(ce, axis=0)
        cflat = jnp.concatenate(cf, axis=0)
        _top16_rows(cand, cflat, PEER_TOPK * PEER_TOPK, cv_scr, None, payload=cexp, pay_scr=ce_scr)
        top = cv_scr[...]
        ex = jnp.exp(top - top[0:1])
        gate_scr[pl.ds(2 * p * PEER_TOPK + 1, PEER_TOPK, stride=2), :] = ex / jnp.sum(ex, axis=0, keepdims=True)
        eid_scr[p * PEER_TOPK:(p + 1) * PEER_TOPK, :] = ce_scr[...] * ROW_SUBL
    off_ref[...] = eid_scr[...].T
    gate_ref[...] = gate_scr[...].T


def peer_route(h, w_query, sub_keys, *, T=128):
    N, D = h.shape
    sk = sub_keys.reshape(PEER_HEADS * 2, PEER_NKEYS, PEER_HALF).astype(bf16)
    kern = functools.partial(_peer_route_kernel, T=T)
    return pl.pallas_call(
        kern,
        grid=(N // T,),
        in_specs=[
            pl.BlockSpec((T, D), lambda i: (i, 0)),
            pl.BlockSpec((D, PEER_HEADS * PEER_QDIM), lambda i: (0, 0)),
            pl.BlockSpec((PEER_HEADS * 2, PEER_NKEYS, PEER_HALF), lambda i: (0, 0, 0)),
        ],
        out_specs=[pl.BlockSpec((T, PEER_SLOTS), lambda i: (i, 0)),
                   pl.BlockSpec((T, 2 * PEER_SLOTS), lambda i: (i, 0))],
        out_shape=[jax.ShapeDtypeStruct((N, PEER_SLOTS), jnp.int32),
                   jax.ShapeDtypeStruct((N, 2 * PEER_SLOTS), f32)],
        scratch_shapes=[
            pltpu.VMEM((PEER_TOPK, T), f32), pltpu.VMEM((PEER_TOPK, T), jnp.int32),
            pltpu.VMEM((PEER_TOPK, T), f32), pltpu.VMEM((PEER_TOPK, T), jnp.int32),
            pltpu.VMEM((PEER_TOPK, T), f32), pltpu.VMEM((PEER_TOPK, T), jnp.int32),
            pltpu.VMEM((PEER_SLOTS, T), jnp.int32), pltpu.VMEM((2 * PEER_SLOTS, T), f32),
        ],
        compiler_params=pltpu.CompilerParams(dimension_semantics=("arbitrary",)),
        name="peer_route",
    )(h, w_query.astype(bf16), sk)


def pack_table(tab):
    bits = lax.bitcast_convert_type(tab.astype(bf16), jnp.uint16).astype(jnp.uint32)
    words = (bits[:, :ROW_WORDS] << 16) | bits[:, ROW_WORDS:]
    return lax.bitcast_convert_type(words, jnp.int32).reshape(tab.shape[0] * ROW_SUBL, 128)


def _gather_rows(off_ref, tok, tab_ref, st):
    tok_offs = off_ref.at[tok]
    for k in range(PEER_SLOTS):
        off = pl.multiple_of(tok_offs[k], ROW_SUBL)
        st[pl.ds(k, ROW_SUBL, stride=STAGE_STRIDE), :] = tab_ref[pl.ds(off, ROW_SUBL), :]
    chunks = [st[c * STAGE_STRIDE:c * STAGE_STRIDE + PEER_SLOTS, :] for c in range(ROW_SUBL)]
    return pltpu.bitcast(jnp.concatenate(chunks, axis=1), bf16)


def _peer_u_kernel(off_ref, h_ref, gate_ref, tab_ref, w_ref, stage_scr, *, T):
    row16 = lax.broadcasted_iota(jnp.int32, (16, 2 * PEER_SLOTS), 0) & 7
    odd = (lax.broadcasted_iota(jnp.int32, (8, 2 * PEER_SLOTS), 1) & 1) == 1

    def group(g, carry):
        t0 = pl.multiple_of(g * 8, 8)
        h8 = h_ref[pl.ds(t0, 8), :]
        h16 = jnp.concatenate([h8[:, :ROW_WORDS], h8[:, ROW_WORDS:]], axis=0).astype(bf16)
        acc = jnp.zeros((16, 2 * PEER_SLOTS), f32)
        for i in range(8):
            rows = _gather_rows(off_ref, t0 + i, tab_ref, stage_scr.at[i % 2])
            res = lax.dot_general(h16, rows, (((1,), (1,)), ((), ())), preferred_element_type=f32)
            acc = jnp.where(row16 == i, res, acc)
        dots = acc[0:8] + pltpu.roll(acc[8:16], 1, axis=1)
        gelu = 0.5 * dots * (1.0 + lax.erf(dots * (2.0 ** -0.5)))
        w_ref[pl.ds(t0, 8), :] = jnp.where(odd, gelu * gate_ref[pl.ds(t0, 8), :], 0.0)
        return carry

    lax.fori_loop(0, T // 8, group, 0)


def _peer_v_kernel(off_ref, w_ref, x_ref, g2_ref, tab_ref, o_ref, stage_scr, *, T):
    row16 = lax.broadcasted_iota(jnp.int32, (16, ROW_WORDS), 0) & 7

    def group(g, carry):
        t0 = pl.multiple_of(g * 8, 8)
        w8 = w_ref[pl.ds(t0, 8), :]
        w16 = jnp.concatenate([w8, pltpu.roll(w8, 2 * PEER_SLOTS - 1, axis=1)], axis=0).astype(bf16)
        acc = jnp.zeros((16, ROW_WORDS), f32)
        for i in range(8):
            rows = _gather_rows(off_ref, t0 + i, tab_ref, stage_scr.at[i % 2])
            res = jnp.dot(w16, rows, preferred_element_type=f32)
            acc = jnp.where(row16 == i, res, acc)
        y = jnp.concatenate([acc[0:8], acc[8:16]], axis=1)
        o_ref[pl.ds(t0, 8), :] = x_ref[pl.ds(t0, 8), :] + g2_ref[0] * y
        return carry

    lax.fori_loop(0, T // 8, group, 0)


_TABLE_VMEM_LIMIT = 56 * 1024 * 1024


def _table_spec():
    return pl.BlockSpec((N_EXPERTS * ROW_SUBL, 128), lambda i: (0, 0), pipeline_mode=pl.Buffered(1))


def _stage_scratch():
    return pltpu.VMEM((2, ROW_SUBL * STAGE_STRIDE, 128), jnp.int32)


def peer_u(off, h, gate, u_pk, *, T=128):
    N, D = h.shape
    return pl.pallas_call(
        functools.partial(_peer_u_kernel, T=T),
        grid=(N // T,),
        in_specs=[
            pl.BlockSpec((T, PEER_SLOTS), lambda i: (i, 0), memory_space=pltpu.SMEM),
            pl.BlockSpec((T, D), lambda i: (i, 0)),
            pl.BlockSpec((T, 2 * PEER_SLOTS), lambda i: (i, 0)),
            _table_spec(),
        ],
        out_specs=pl.BlockSpec((T, 2 * PEER_SLOTS), lambda i: (i, 0)),
        out_shape=jax.ShapeDtypeStruct((N, 2 * PEER_SLOTS), f32),
        scratch_shapes=[_stage_scratch()],
        compiler_params=pltpu.CompilerParams(dimension_semantics=("arbitrary",),
                                             vmem_limit_bytes=_TABLE_VMEM_LIMIT),
        name="peer_u",
    )(off, h, gate, u_pk)


def peer_v(off, w, x, g2, v_pk, *, T=128):
    N, D = x.shape
    per_batch = N // g2.shape[0]
    return pl.pallas_call(
        functools.partial(_peer_v_kernel, T=T),
        grid=(N // T,),
        in_specs=[
            pl.BlockSpec((T, PEER_SLOTS), lambda i: (i, 0), memory_space=pltpu.SMEM),
            pl.BlockSpec((T, 2 * PEER_SLOTS), lambda i: (i, 0)),
            pl.BlockSpec((T, D), lambda i: (i, 0)),
            pl.BlockSpec((1, 1, D), lambda i: (i * T // per_batch, 0, 0)),
            _table_spec(),
        ],
        out_specs=pl.BlockSpec((T, D), lambda i: (i, 0)),
        out_shape=jax.ShapeDtypeStruct((N, D), f32),
        scratch_shapes=[_stage_scratch()],
        compiler_params=pltpu.CompilerParams(dimension_semantics=("arbitrary",),
                                             vmem_limit_bytes=_TABLE_VMEM_LIMIT),
        name="peer_v",
    )(off, w, x, g2.reshape(g2.shape[0], 1, D), v_pk)


def kernel(x, c, w_ada, b_ada, w_in, q_gain, k_gain, w_uv, conv_w, a_log, dt_bias,
           gdn_gain, w_out, w_query, sub_keys, u_tab, v_tab):
    B, S, D = x.shape
    xf = x.reshape(B * S, D)
    for l in range(DEPTH):
        sh1, sc1, g1, sh2, sc2, g2 = jnp.split(adaln(c, w_ada[l], b_ada[l]), 6, axis=-1)
        aq, akv, iq, small, qkvz = in_projection(xf, sh1, sc1, w_in[l])
        small = small.reshape(B, S, -1)
        ik, iw, ba, bb = (small[..., lo:hi] for lo, hi in (SMALL_IK, SMALL_IW, SMALL_BA, SMALL_BB))
        ya = dsa_attention(aq.reshape(B, S, -1), akv.reshape(B, S, -1), iq.reshape(B, S, -1), ik, iw,
                           q_gain[l], k_gain[l], w_uv[l])
        yb = gated_deltanet(qkvz.reshape(B, S, -1), ba, bb, conv_w[l], a_log[l], dt_bias[l], gdn_gain[l])
        x1, h2 = out_projection(ya.reshape(B * S, -1), yb.reshape(B * S, -1), xf, g1, sh2, sc2, w_out[l])
        off, gate = peer_route(h2, w_query[l], sub_keys[l])
        w = peer_u(off, h2, gate, pack_table(u_tab[l]))
        xf = peer_v(off, w, x1, g2, pack_table(v_tab[l]))
    return xf.reshape(B, S, D)
```

```python
import functools
import math
import jax
import jax.numpy as jnp
from jax import lax
import numpy as np
from jax.experimental import pallas as pl
from jax.experimental.pallas import tpu as pltpu

D_MODEL = 1024
BATCH = 8
SEQ = 4096
DEPTH = 1

EPS = 1e-6
A_HEADS = 8
A_LAT = 128
A_VDIM = 64
IDX_HEADS = 4
IDX_DIM = 64
IDX_TOPK_MAX = 256
Q_BLOCK = 128
B_HEADS = 4
B_DIM = 128
CONV_W = 4
CHUNK = 64
PEER_HEADS = 8
PEER_NKEYS = 128
PEER_QDIM = 256
PEER_TOPK = 16
PEER_BLOCK = 128
N_EXPERTS = PEER_NKEYS * PEER_NKEYS

A_WIDTH = A_HEADS * A_VDIM
B_WIDTH = B_HEADS * B_DIM
MIX_WIDTH = A_WIDTH + B_WIDTH
IN_SPLITS = (A_HEADS * A_LAT, A_LAT, IDX_HEADS * IDX_DIM, IDX_DIM, IDX_HEADS,
             B_WIDTH, B_WIDTH, B_WIDTH, B_WIDTH, B_HEADS, B_HEADS)
IN_WIDTH = sum(IN_SPLITS)


f32 = jnp.float32
bf16 = jnp.bfloat16
INT_MIN = -2 ** 31
SUM_LANE = 2
_PROJ_VMEM_LIMIT = 48 * 1024 * 1024


def _rms_modulate(x, shift, scale):
    xn = x * lax.rsqrt(jnp.mean(x * x, -1, keepdims=True) + EPS)
    return xn * (1.0 + scale) + shift


def _adaln_kernel(c_ref, w_ref, b_ref, o_ref):
    c = c_ref[...]
    s = (c * jax.nn.sigmoid(c)).astype(bf16)
    o_ref[...] = jnp.dot(s, w_ref[...].astype(bf16), preferred_element_type=f32) + b_ref[...]


def adaln(c, w, b, *, tn=512):
    B, D = c.shape
    N = w.shape[1]
    return pl.pallas_call(
        _adaln_kernel,
        grid=(N // tn,),
        in_specs=[pl.BlockSpec((B, D), lambda j: (0, 0)),
                  pl.BlockSpec((D, tn), lambda j: (0, j)),
                  pl.BlockSpec((1, tn), lambda j: (0, j))],
        out_specs=pl.BlockSpec((B, tn), lambda j: (0, j)),
        out_shape=jax.ShapeDtypeStruct((B, N), f32),
        name="adaln",
    )(c, w, b.reshape(1, N))


SMALL_IK = (0, IDX_DIM)
SMALL_IW = (IDX_DIM, IDX_DIM + IDX_HEADS)
SMALL_BA = (IDX_DIM + IDX_HEADS, IDX_DIM + IDX_HEADS + B_HEADS)
SMALL_BB = (IDX_DIM + IDX_HEADS + B_HEADS, IDX_DIM + IDX_HEADS + 2 * B_HEADS)


def _inproj_kernel(x_ref, sh_ref, sc_ref, wq_ref, wkv_ref, wiq_ref, wsm_ref, wb_ref,
                   oq_ref, okv_ref, oiq_ref, osm_ref, ob_ref):
    h = _rms_modulate(x_ref[...], sh_ref[0], sc_ref[0]).astype(bf16)
    for w_ref, o_ref in ((wq_ref, oq_ref), (wkv_ref, okv_ref), (wiq_ref, oiq_ref),
                         (wsm_ref, osm_ref), (wb_ref, ob_ref)):
        o_ref[...] = jnp.dot(h, w_ref[...], preferred_element_type=f32)


def in_projection(x, shift, scale, w_in, *, tm=512):
    N, D = x.shape
    per_batch = N // shift.shape[0]
    o = [int(v) for v in np.cumsum((0,) + IN_SPLITS)]
    wb = w_in.astype(bf16)
    w_q, w_kv, w_iq = wb[:, o[0]:o[1]], wb[:, o[1]:o[2]], wb[:, o[2]:o[3]]
    w_small = jnp.concatenate([wb[:, o[3]:o[5]], wb[:, o[9]:o[11]],
                               jnp.zeros((D, 128 - SMALL_BB[1]), bf16)], axis=1)
    w_b = wb[:, o[5]:o[9]]
    weights = (w_q, w_kv, w_iq, w_small, w_b)
    mod = pl.BlockSpec((1, 1, D), lambda i: (i * tm // per_batch, 0, 0))
    return pl.pallas_call(
        _inproj_kernel,
        grid=(N // tm,),
        in_specs=[pl.BlockSpec((tm, D), lambda i: (i, 0)), mod, mod]
                 + [pl.BlockSpec(w.shape, lambda i: (0, 0), pipeline_mode=pl.Buffered(1)) for w in weights],
        out_specs=[pl.BlockSpec((tm, w.shape[1]), lambda i: (i, 0)) for w in weights],
        out_shape=[jax.ShapeDtypeStruct((N, w.shape[1]), f32) for w in weights],
        compiler_params=pltpu.CompilerParams(dimension_semantics=("arbitrary",),
                                             vmem_limit_bytes=_PROJ_VMEM_LIMIT),
        name="in_projection",
    )(x, shift[:, None, :], scale[:, None, :], *weights)


def _outproj_kernel(ya_ref, yb_ref, x_ref, g1_ref, sh_ref, sc_ref, wa_ref, wb_ref, x1_ref, h2_ref):
    mix = (jnp.dot(ya_ref[...].astype(bf16), wa_ref[...], preferred_element_type=f32)
           + jnp.dot(yb_ref[...].astype(bf16), wb_ref[...], preferred_element_type=f32))
    x1 = x_ref[...] + g1_ref[0] * mix
    x1_ref[...] = x1
    h2_ref[...] = _rms_modulate(x1, sh_ref[0], sc_ref[0])


def out_projection(ya, yb, x, gate, shift, scale, w_out, *, tm=512):
    N, D = x.shape
    per_batch = N // gate.shape[0]
    wb = w_out.astype(bf16)
    w_a, w_b = wb[:A_WIDTH], wb[A_WIDTH:]
    mod = pl.BlockSpec((1, 1, D), lambda i: (i * tm // per_batch, 0, 0))
    row = pl.BlockSpec((tm, D), lambda i: (i, 0))
    half = pl.BlockSpec((tm, A_WIDTH), lambda i: (i, 0))
    return pl.pallas_call(
        _outproj_kernel,
        grid=(N // tm,),
        in_specs=[half, half, row, mod, mod, mod,
                  pl.BlockSpec(w_a.shape, lambda i: (0, 0)), pl.BlockSpec(w_b.shape, lambda i: (0, 0))],
        out_specs=[row, row],
        out_shape=[jax.ShapeDtypeStruct((N, D), f32), jax.ShapeDtypeStruct((N, D), f32)],
        compiler_params=pltpu.CompilerParams(dimension_semantics=("arbitrary",),
                                             vmem_limit_bytes=_PROJ_VMEM_LIMIT),
        name="out_projection",
    )(ya, yb, x, gate[:, None, :], shift[:, None, :], scale[:, None, :], w_a, w_b)


def _dsa_kernel(q_ref, kv_ref, iq_ref, ik_ref, iw_ref, qg_ref, kg_ref, wbd_ref, tri_ref, o_ref,
                kaug_scr, ikb_scr, key_scr, qaug_scr, *head_scr, S, TQ, TK, topk):
    qi = pl.program_id(1)
    scale = A_LAT ** -0.5
    idx_scale = (IDX_DIM ** -0.5) * (IDX_HEADS ** -0.5)

    @pl.when(qi == 0)
    def _prep_keys():
        kv = kv_ref[0]
        kn = kv * lax.rsqrt(jnp.mean(kv * kv, -1, keepdims=True) + EPS) * kg_ref[...]
        pos = lax.broadcasted_iota(jnp.int32, (S, A_LAT), 0)
        lane = lax.broadcasted_iota(jnp.int32, (S, A_LAT), 1)
        hi = (pos >> 6).astype(f32)
        lo = (pos & 63).astype(f32)
        extra = jnp.where(lane == 0, hi, jnp.where(lane == 1, lo, jnp.where(lane == SUM_LANE, 1.0, 0.0)))
        kaug_scr[:, :A_LAT] = kn.astype(bf16)
        kaug_scr[:, A_LAT:] = extra.astype(bf16)
        ikb_scr[...] = ik_ref[0].astype(bf16)

    q = q_ref[0]
    lane = lax.broadcasted_iota(jnp.int32, (TQ, A_LAT), 1)
    for h in range(A_HEADS):
        qh = q[:, h * A_LAT:(h + 1) * A_LAT]
        qn = qh * lax.rsqrt(jnp.mean(qh * qh, -1, keepdims=True) + EPS) * qg_ref[...] * scale
        slope = 2.0 ** (-8.0 * (h + 1) / A_HEADS)
        extra = jnp.where(lane == 0, slope * 64.0, jnp.where(lane == 1, slope, 0.0))
        qaug_scr[h * TQ:(h + 1) * TQ, :A_LAT] = qn.astype(bf16)
        qaug_scr[h * TQ:(h + 1) * TQ, A_LAT:] = extra.astype(bf16)

    nchunks = (qi * TQ + TQ + TK - 1) // TK
    qpos = qi * TQ + lax.broadcasted_iota(jnp.int32, (TQ, 1), 0)
    colb = lax.broadcasted_iota(jnp.int32, (TQ, TK), 1)
    iq = iq_ref[0]
    iqb = [iq[:, h * IDX_DIM:(h + 1) * IDX_DIM].astype(bf16) for h in range(IDX_HEADS)]
    iw = iw_ref[0]

    def score_chunk(c, carry):
        off = pl.multiple_of(c * TK, TK)
        ikc = ikb_scr[pl.ds(off, TK), :]
        acc = jnp.zeros((TQ, TK), f32)
        for h in range(IDX_HEADS):
            lg = lax.dot_general(iqb[h], ikc, (((1,), (1,)), ((), ())), preferred_element_type=f32)
            acc = acc + jnp.maximum(lg, 0.0) * iw[:, h:h + 1]
        sc = acc * idx_scale
        sc = jnp.where(colb + off <= qpos, sc, -jnp.inf)
        bits = pltpu.bitcast(sc, jnp.int32)
        key_scr[:, pl.ds(off, TK)] = jnp.where(bits < 0, bits ^ jnp.int32(0x7FFFFFFF), bits)
        return carry

    lax.fori_loop(0, nchunks, score_chunk, 0)

    def count(pred_fn):
        def body(c, cnt):
            off = pl.multiple_of(c * TK, TK)
            hit = pred_fn(key_scr[:, pl.ds(off, TK)]).astype(jnp.int32)
            part = hit[:, 0:128]
            for j in range(1, TK // 128):
                part = part + hit[:, j * 128:(j + 1) * 128]
            return cnt + part
        cnt = lax.fori_loop(0, nchunks, body, jnp.zeros((TQ, 128), jnp.int32))
        return jnp.sum(cnt, axis=1, keepdims=True)

    def bit_step(i, T):
        cand = T + lax.shift_left(jnp.int32(1), jnp.int32(31) - i)
        cnt = count(lambda k: k >= cand)
        return jnp.where(cnt >= topk, cand, T)

    T = lax.fori_loop(0, 32, bit_step, jnp.full((TQ, 1), INT_MIN, jnp.int32))
    n_gt = count(lambda k: k > T)
    room = (topk - n_gt).astype(f32)

    m_scr, acc_scr = head_scr[:A_HEADS], head_scr[A_HEADS:]
    for h in range(A_HEADS):
        m_scr[h][...] = jnp.full((TQ, 1), -jnp.inf, f32)
        acc_scr[h][...] = jnp.zeros((TQ, 2 * A_LAT), f32)

    def attend_chunk(c, ties_before):
        off = pl.multiple_of(c * TK, TK)
        keyc = key_scr[:, pl.ds(off, TK)]
        eq = keyc == T
        eqb = jnp.where(eq, 1.0, 0.0).astype(bf16)
        ties = ties_before
        admit = []
        for j in range(TK // 128):
            blk = slice(j * 128, (j + 1) * 128)
            cnt = jnp.dot(eqb[:, blk], tri_ref[...], preferred_element_type=f32)
            admit.append(eq[:, blk] & (cnt[:, :128] + ties < room))
            ties = ties + cnt[:, 128:]
        sel = (keyc > T) | jnp.concatenate(admit, axis=1)
        sel = sel & (colb + off <= qpos)
        kc = kaug_scr[pl.ds(off, TK), :]
        scores = lax.dot_general(qaug_scr[...], kc, (((1,), (1,)), ((), ())), preferred_element_type=f32)
        probs, alphas = [], []
        for h in range(A_HEADS):
            s = jnp.where(sel, scores[h * TQ:(h + 1) * TQ], -jnp.inf)
            m_old = m_scr[h][...]
            m_new = jnp.maximum(m_old, jnp.max(s, axis=1, keepdims=True))
            m_safe = jnp.where(m_new == -jnp.inf, 0.0, m_new)
            probs.append(jnp.exp(s - m_safe).astype(bf16))
            alphas.append(jnp.exp(m_old - m_safe))
            m_scr[h][...] = m_new
        for h in range(A_HEADS):
            acc_scr[h][...] = alphas[h] * acc_scr[h][...] + jnp.dot(probs[h], kc, preferred_element_type=f32)
        return ties

    lax.fori_loop(0, nchunks, attend_chunk, jnp.zeros((TQ, 128), f32))

    heads = []
    for h in range(A_HEADS):
        acc = acc_scr[h][...]
        row_sum = acc[:, A_LAT + SUM_LANE:A_LAT + SUM_LANE + 1]
        heads.append((acc[:, :A_LAT] / row_sum).astype(bf16))
    o = jnp.concatenate(heads, axis=1)
    o_ref[0] = jnp.dot(o, wbd_ref[...], preferred_element_type=f32)


def dsa_attention(q_lat, kv, iq, ik, iw, q_gain, k_gain, w_uv, *, TQ=128, TK=1024):
    B, S, _ = q_lat.shape
    topk = min(IDX_TOPK_MAX, S // 4)
    TK = min(TK, S)
    wbd = jnp.zeros((A_HEADS * A_LAT, A_WIDTH), f32)
    for h in range(A_HEADS):
        wbd = wbd.at[h * A_LAT:(h + 1) * A_LAT, h * A_VDIM:(h + 1) * A_VDIM].set(w_uv[h])
    blk = jnp.arange(128)
    tri = jnp.concatenate([blk[:, None] < blk[None, :], jnp.ones((128, 128), bool)], axis=1).astype(bf16)
    kern = functools.partial(_dsa_kernel, S=S, TQ=TQ, TK=TK, topk=topk)
    return pl.pallas_call(
        kern,
        grid=(B, S // TQ),
        in_specs=[
            pl.BlockSpec((1, TQ, A_HEADS * A_LAT), lambda b, i: (b, i, 0)),
            pl.BlockSpec((1, S, A_LAT), lambda b, i: (b, 0, 0)),
            pl.BlockSpec((1, TQ, IDX_HEADS * IDX_DIM), lambda b, i: (b, i, 0)),
            pl.BlockSpec((1, S, IDX_DIM), lambda b, i: (b, 0, 0)),
            pl.BlockSpec((1, TQ, IDX_HEADS), lambda b, i: (b, i, 0)),
            pl.BlockSpec((1, A_LAT), lambda b, i: (0, 0)),
            pl.BlockSpec((1, A_LAT), lambda b, i: (0, 0)),
            pl.BlockSpec((A_HEADS * A_LAT, A_WIDTH), lambda b, i: (0, 0)),
            pl.BlockSpec((128, 256), lambda b, i: (0, 0)),
        ],
        out_specs=pl.BlockSpec((1, TQ, A_WIDTH), lambda b, i: (b, i, 0)),
        out_shape=jax.ShapeDtypeStruct((B, S, A_WIDTH), f32),
        scratch_shapes=[
            pltpu.VMEM((S, 2 * A_LAT), bf16),
            pltpu.VMEM((S, IDX_DIM), bf16),
            pltpu.VMEM((TQ, S), jnp.int32),
            pltpu.VMEM((A_HEADS * TQ, 2 * A_LAT), bf16),
            *[pltpu.VMEM((TQ, 1), f32) for _ in range(A_HEADS)],
            *[pltpu.VMEM((TQ, 2 * A_LAT), f32) for _ in range(A_HEADS)],
        ],
        compiler_params=pltpu.CompilerParams(dimension_semantics=("arbitrary", "arbitrary")),
        name="dsa_attention",
    )(q_lat, kv, iq, ik, iw, q_gain.reshape(1, -1), k_gain.reshape(1, -1), wbd.astype(bf16), tri)


CONV_HALO = 8


def _split_bf16(x, terms):
    parts = []
    for _ in range(terms):
        p = x.astype(bf16)
        parts.append(p)
        x = x - p.astype(f32)
    return parts


def _dot_exact_lhs(mask_bf16, x):
    return sum(jnp.dot(mask_bf16, p, preferred_element_type=f32) for p in _split_bf16(x, 3))


def _dot_exact_rhs(x, mask_bf16):
    return sum(jnp.dot(p, mask_bf16, preferred_element_type=f32) for p in _split_bf16(x, 3))


def _dot_3pass(a, b):
    ah, al = _split_bf16(a, 2)
    bh, bl = _split_bf16(b, 2)
    return (jnp.dot(ah, bh, preferred_element_type=f32) + jnp.dot(ah, bl, preferred_element_type=f32)
            + jnp.dot(al, bh, preferred_element_type=f32))


def _softplus(x):
    return jnp.maximum(x, 0.0) + jnp.log1p(jnp.exp(-jnp.abs(x)))


def _gdn_kernel(q_ref, k_ref, v_ref, z_ref, a_ref, b_ref, at_ref, bt_ref, cw_ref, alog_ref, dtb_ref,
                gain_ref, o_ref, tail_scr, act_scr, *state_scr, SB):
    sj = pl.program_id(1)

    @pl.when(sj == 0)
    def _reset():
        tail_scr[...] = jnp.zeros(tail_scr.shape, f32)
        for st_ref in state_scr:
            st_ref[...] = jnp.zeros((B_DIM, B_DIM), f32)

    cw = cw_ref[...]
    for idx, x_ref in enumerate((q_ref, k_ref, v_ref)):
        x = x_ref[0]
        xc = jnp.concatenate([tail_scr[idx], x], axis=0)
        y = jnp.zeros((SB, B_WIDTH), f32)
        for j in range(CONV_W):
            lo = CONV_HALO - (CONV_W - 1) + j
            y = y + cw[j:j + 1, idx * B_WIDTH:(idx + 1) * B_WIDTH] * xc[lo:lo + SB]
        tail_scr[idx] = x[SB - CONV_HALO:SB]
        act_scr[idx] = y * jax.nn.sigmoid(y)

    ri = lax.broadcasted_iota(jnp.int32, (CHUNK, CHUNK), 0)
    ci = lax.broadcasted_iota(jnp.int32, (CHUNK, CHUNK), 1)
    tril = ri >= ci
    strict = ri > ci
    tril_b = jnp.where(tril, 1.0, 0.0).astype(bf16)
    triu_b = jnp.where(ri <= ci, 1.0, 0.0).astype(bf16)
    eye = jnp.where(ri == ci, 1.0, 0.0)
    nt = (((1,), (1,)), ((), ()))

    def chunk(c, carry):
        rows = pl.ds(pl.multiple_of(c * CHUNK, CHUNK), CHUNK)
        H = range(B_HEADS)
        lanes = [slice(h * B_DIM, (h + 1) * B_DIM) for h in H]
        lp = dict(preferred_element_type=f32)
        q = [act_scr[0, rows, lanes[h]] for h in H]
        k = [act_scr[1, rows, lanes[h]] for h in H]
        v = [act_scr[2, rows, lanes[h]] for h in H]
        q = [x * lax.rsqrt(jnp.sum(x * x, -1, keepdims=True) + EPS) * (B_DIM ** -0.5) for x in q]
        k = [x * lax.rsqrt(jnp.sum(x * x, -1, keepdims=True) + EPS) for x in k]
        neg_rate = [-jnp.exp(alog_ref[0:1, h:h + 1]) for h in H]
        dtb = [dtb_ref[0:1, h:h + 1] for h in H]
        g_col = [neg_rate[h] * _softplus(a_ref[0, rows, h:h + 1] + dtb[h]) for h in H]
        g_row = [neg_rate[h] * _softplus(at_ref[0, h, pl.ds(c, 1), :] + dtb[h]) for h in H]
        beta = [jax.nn.sigmoid(b_ref[0, rows, h:h + 1]) for h in H]
        Gc = [_dot_exact_lhs(tril_b, jnp.broadcast_to(g_col[h], (CHUNK, CHUNK))) for h in H]
        Gr = [_dot_exact_rhs(jnp.broadcast_to(g_row[h], (CHUNK, CHUNK)), triu_b) for h in H]
        G = [x[:, 0:1] for x in Gc]
        G_last = [x[CHUNK - 1:CHUNK, 0:1] for x in Gc]
        decay = [jnp.exp(jnp.where(tril, Gc[h] - Gr[h], -jnp.inf)) for h in H]
        kb = [k[h] * beta[h] for h in H]
        kbf = [x.astype(bf16) for x in k]
        L = [jnp.where(strict, lax.dot_general(kb[h].astype(bf16), kbf[h], nt, **lp) * decay[h], 0.0) for h in H]
        P = [eye - x for x in L]
        M = [_dot_3pass(x, x) for x in L]
        for lvl in range(5):
            P = [P[h] + _dot_3pass(P[h], M[h]) for h in H]
            if lvl < 4:
                M = [_dot_3pass(x, x) for x in M]
        Tb = [x.astype(bf16) for x in P]
        eG = [jnp.exp(x) for x in G]
        u = [jnp.dot(Tb[h], (v[h] * beta[h]).astype(bf16), **lp) for h in H]
        w = [jnp.dot(Tb[h], (kb[h] * eG[h]).astype(bf16), **lp) for h in H]
        attn = [lax.dot_general(q[h].astype(bf16), kbf[h], nt, **lp) * decay[h] for h in H]
        q_dec = [q[h] * eG[h] for h in H]
        k_dec = [k[h] * jnp.exp(G_last[h] - G[h]) for h in H]
        st = [state_scr[h][...] for h in H]
        stb = [x.astype(bf16) for x in st]
        v_new = [u[h] - jnp.dot(w[h].astype(bf16), stb[h], **lp) for h in H]
        vnb = [x.astype(bf16) for x in v_new]
        o = [jnp.dot(q_dec[h].astype(bf16), stb[h], **lp) + jnp.dot(attn[h].astype(bf16), vnb[h], **lp) for h in H]
        for h in H:
            state_scr[h][...] = st[h] * jnp.exp(G_last[h]) + lax.dot_general(
                k_dec[h].astype(bf16), vnb[h], (((0,), (0,)), ((), ())), **lp)
        for h in H:
            zh = z_ref[0, rows, lanes[h]]
            on = o[h] * lax.rsqrt(jnp.mean(o[h] * o[h], -1, keepdims=True) + EPS) * gain_ref[...]
            o_ref[0, rows, lanes[h]] = on * (zh * jax.nn.sigmoid(zh))
        return carry

    lax.fori_loop(0, SB // CHUNK, chunk, 0)


def gated_deltanet(qkvz, a, b, conv_w, a_log, dt_bias, norm_gain, *, SB=512):
    B, S, _ = qkvz.shape
    SB = min(SB, S)
    nch = S // CHUNK
    at = a.transpose(0, 2, 1).reshape(B, B_HEADS, nch, CHUNK)
    bt = b.transpose(0, 2, 1).reshape(B, B_HEADS, nch, CHUNK)
    wide = pl.BlockSpec((1, SB, B_WIDTH), lambda bi, j: (bi, j, 0))
    part = [pl.BlockSpec((1, SB, B_WIDTH), functools.partial(lambda bi, j, n: (bi, j, n), n=n))
            for n in range(4)]
    narrow = pl.BlockSpec((1, SB, B_HEADS), lambda bi, j: (bi, j, 0))
    rowwise = pl.BlockSpec((1, B_HEADS, SB // CHUNK, CHUNK), lambda bi, j: (bi, 0, j, 0))
    return pl.pallas_call(
        functools.partial(_gdn_kernel, SB=SB),
        grid=(B, S // SB),
        in_specs=[*part, narrow, narrow, rowwise, rowwise,
                  pl.BlockSpec((CONV_W, 3 * B_WIDTH), lambda bi, j: (0, 0)),
                  pl.BlockSpec((1, B_HEADS), lambda bi, j: (0, 0)),
                  pl.BlockSpec((1, B_HEADS), lambda bi, j: (0, 0)),
                  pl.BlockSpec((1, B_DIM), lambda bi, j: (0, 0))],
        out_specs=wide,
        out_shape=jax.ShapeDtypeStruct((B, S, B_WIDTH), f32),
        scratch_shapes=[pltpu.VMEM((3, CONV_HALO, B_WIDTH), f32),
                        pltpu.VMEM((3, SB, B_WIDTH), f32),
                        *[pltpu.VMEM((B_DIM, B_DIM), f32) for _ in range(B_HEADS)]],
        compiler_params=pltpu.CompilerParams(dimension_semantics=("arbitrary", "arbitrary")),
        name="gated_deltanet",
    )(qkvz, qkvz, qkvz, qkvz, a, b, at, bt, conv_w,
      a_log.reshape(1, -1), dt_bias.reshape(1, -1), norm_gain.reshape(1, -1))


PEER_SLOTS = PEER_HEADS * PEER_TOPK
PEER_HALF = PEER_QDIM // 2
ROW_WORDS = D_MODEL // 2
ROW_SUBL = ROW_WORDS // 128
STAGE_STRIDE = PEER_SLOTS + 8
HALF_EXPERTS = N_EXPERTS // 2
PAIR_SUBL = 2 * ROW_SUBL


def _table_half(expert):
    return ((expert >> 7) ^ expert) & 1


def _top16_rows(s, iota_rows, fill, vals_scr, idx_scr, payload=None, pay_scr=None):
    for r in range(PEER_TOPK):
        m = jnp.max(s, axis=0, keepdims=True)
        first = jnp.min(jnp.where(s == m, iota_rows, fill), axis=0, keepdims=True)
        taken = iota_rows == first
        vals_scr[r:r + 1, :] = m
        if payload is None:
            idx_scr[r:r + 1, :] = first.astype(jnp.int32)
        else:
            pay_scr[r:r + 1, :] = jnp.max(jnp.where(taken, payload, -1.0), axis=0,
                                          keepdims=True).astype(jnp.int32)
        s = jnp.where(taken, -jnp.inf, s)


def _peer_route_kernel(h_ref, wq_ref, sk_ref, off_ref, gate_lo_ref, gate_hi_ref, nlow_ref,
                       v1_scr, i1_scr, v2_scr, i2_scr, cv_scr, ce_scr, eid_scr, gate_scr,
                       off_scr, gate_lo_scr, gate_hi_scr, *, T):
    q = jnp.dot(h_ref[...].astype(bf16), wq_ref[...], preferred_element_type=f32).astype(bf16)
    kiota = lax.broadcasted_iota(jnp.int32, (PEER_NKEYS, T), 0).astype(f32)
    sub8 = lax.broadcasted_iota(jnp.int32, (8, T), 0).astype(f32)
    for p in range(PEER_HEADS):
        for half, (vs, is_) in enumerate(((v1_scr, i1_scr), (v2_scr, i2_scr))):
            g = 2 * p + half
            s = lax.dot_general(sk_ref[g], q[:, g * PEER_HALF:(g + 1) * PEER_HALF],
                                (((1,), (1,)), ((), ())), preferred_element_type=f32)
            _top16_rows(s, kiota, PEER_NKEYS, vs, is_)
        v1 = v1_scr[...]
        v2 = v2_scr[...]
        e1 = (i1_scr[...] * PEER_NKEYS).astype(f32)
        e2 = i2_scr[...].astype(f32)
        cv = [v1[0:1] + v2[0:8], v1[0:1] + v2[8:16]]
        ce = [e1[0:1] + e2[0:8], e1[0:1] + e2[8:16]]
        cf = [sub8, sub8 + 8]
        for a in range(1, 8):
            cv.append(v1[a:a + 1] + v2[0:8])
            ce.append(e1[a:a + 1] + e2[0:8])
            cf.append(sub8 + a * PEER_TOPK)
        cv.append(v1[8:16] + v2[0:1])
        ce.append(e1[8:16] + e2[0:1])
        cf.append((sub8 + 8) * PEER_TOPK)
        cand = jnp.concatenate(cv, axis=0)
        cexp = jnp.concatenate(ce, axis=0)
        cflat = jnp.concatenate(cf, axis=0)
        _top16_rows(cand, cflat, PEER_TOPK * PEER_TOPK, cv_scr, None, payload=cexp, pay_scr=ce_scr)
        top = cv_scr[...]
        ex = jnp.exp(top - top[0:1])
        gate_scr[p * PEER_TOPK:(p + 1) * PEER_TOPK, :] = ex / jnp.sum(ex, axis=0, keepdims=True)
        eid_scr[p * PEER_TOPK:(p + 1) * PEER_TOPK, :] = ce_scr[...]

    eid = eid_scr[...]
    gates = gate_scr[...]
    low = _table_half(eid) == 0
    ri = lax.broadcasted_iota(jnp.int32, (PEER_SLOTS, PEER_SLOTS), 0)
    ci = lax.broadcasted_iota(jnp.int32, (PEER_SLOTS, PEER_SLOTS), 1)
    before = jnp.where(ci < ri, 1.0, 0.0).astype(bf16)
    low_f = jnp.where(low, 1.0, 0.0)
    low_before = jnp.dot(before, low_f.astype(bf16), preferred_element_type=f32)
    n_low = low_before[PEER_SLOTS - 1:PEER_SLOTS] + low_f[PEER_SLOTS - 1:PEER_SLOTS]
    slot_f = lax.broadcasted_iota(jnp.int32, (PEER_SLOTS, T), 0).astype(f32)
    pos = jnp.where(low, low_before, n_low + (slot_f - low_before))
    row_off = ((eid >> 1) * PAIR_SUBL).astype(f32)
    gate_lo_scr[...] = jnp.zeros(gate_lo_scr.shape, f32)
    gate_hi_scr[...] = jnp.zeros(gate_hi_scr.shape, f32)
    for dst in range(PEER_SLOTS):
        here = pos == dst
        off_scr[dst:dst + 1, :] = jnp.sum(jnp.where(here, row_off, 0.0), axis=0, keepdims=True).astype(jnp.int32)
        g = jnp.sum(jnp.where(here, gates, 0.0), axis=0, keepdims=True)
        in_low = dst < n_low
        gate_lo_scr[2 * dst + 1:2 * dst + 2, :] = jnp.where(in_low, g, 0.0)
        gate_hi_scr[2 * dst + 1:2 * dst + 2, :] = jnp.where(in_low, 0.0, g)
    off_ref[...] = off_scr[...].T
    gate_lo_ref[...] = gate_lo_scr[...].T
    gate_hi_ref[...] = gate_hi_scr[...].T
    nlow_ref[...] = n_low.astype(jnp.int32).reshape(1, 1, T)


def peer_route(h, w_query, sub_keys, *, T=128):
    N, D = h.shape
    sk = sub_keys.reshape(PEER_HEADS * 2, PEER_NKEYS, PEER_HALF).astype(bf16)
    kern = functools.partial(_peer_route_kernel, T=T)
    wide = pl.BlockSpec((T, 2 * PEER_SLOTS), lambda i: (i, 0))
    return pl.pallas_call(
        kern,
        grid=(N // T,),
        in_specs=[
            pl.BlockSpec((T, D), lambda i: (i, 0)),
            pl.BlockSpec((D, PEER_HEADS * PEER_QDIM), lambda i: (0, 0)),
            pl.BlockSpec((PEER_HEADS * 2, PEER_NKEYS, PEER_HALF), lambda i: (0, 0, 0)),
        ],
        out_specs=[pl.BlockSpec((T, PEER_SLOTS), lambda i: (i, 0)), wide, wide,
                   pl.BlockSpec((1, 1, T), lambda i: (i, 0, 0))],
        out_shape=[jax.ShapeDtypeStruct((N, PEER_SLOTS), jnp.int32),
                   jax.ShapeDtypeStruct((N, 2 * PEER_SLOTS), f32),
                   jax.ShapeDtypeStruct((N, 2 * PEER_SLOTS), f32),
                   jax.ShapeDtypeStruct((N // T, 1, T), jnp.int32)],
        scratch_shapes=[
            pltpu.VMEM((PEER_TOPK, T), f32), pltpu.VMEM((PEER_TOPK, T), jnp.int32),
            pltpu.VMEM((PEER_TOPK, T), f32), pltpu.VMEM((PEER_TOPK, T), jnp.int32),
            pltpu.VMEM((PEER_TOPK, T), f32), pltpu.VMEM((PEER_TOPK, T), jnp.int32),
            pltpu.VMEM((PEER_SLOTS, T), jnp.int32), pltpu.VMEM((PEER_SLOTS, T), f32),
            pltpu.VMEM((PEER_SLOTS, T), jnp.int32),
            pltpu.VMEM((2 * PEER_SLOTS, T), f32), pltpu.VMEM((2 * PEER_SLOTS, T), f32),
        ],
        compiler_params=pltpu.CompilerParams(dimension_semantics=("arbitrary",)),
        name="peer_route",
    )(h, w_query.astype(bf16), sk)


def _pack_rows(tab):
    bits = lax.bitcast_convert_type(tab.astype(bf16), jnp.uint16).astype(jnp.uint32)
    words = (bits[:, :ROW_WORDS] << 16) | bits[:, ROW_WORDS:]
    return lax.bitcast_convert_type(words, jnp.int32).reshape(tab.shape[0], ROW_SUBL, 128)


def pack_pair_tables(u_tab, v_tab):
    pair = jnp.concatenate([_pack_rows(u_tab), _pack_rows(v_tab)], axis=1)
    two = pair.reshape(HALF_EXPERTS, 2, PAIR_SUBL, 128)
    even_is_low = (_table_half(2 * jnp.arange(HALF_EXPERTS)) == 0)[:, None, None]
    low = jnp.where(even_is_low, two[:, 0], two[:, 1])
    high = jnp.where(even_is_low, two[:, 1], two[:, 0])
    return low.reshape(HALF_EXPERTS * PAIR_SUBL, 128), high.reshape(HALF_EXPERTS * PAIR_SUBL, 128)


def _staged_rows(st, first_chunk):
    chunks = [st[(first_chunk + c) * STAGE_STRIDE:(first_chunk + c) * STAGE_STRIDE + PEER_SLOTS, :]
              for c in range(ROW_SUBL)]
    return pltpu.bitcast(jnp.concatenate(chunks, axis=1), bf16)


def _peer_pass_kernel(off_ref, nlow_ref, h_ref, gate_ref, tab_ref, *rest, T, lo, hi, lower_half, final):
    if final:
        y_ref, x_ref, g2_ref, o_ref, stage_scr = rest
    else:
        o_ref, stage_scr = rest

    @pl.when(pl.program_id(0) == 0)
    def _clear():
        stage_scr[...] = jnp.zeros(stage_scr.shape, jnp.int32)

    row16u = lax.broadcasted_iota(jnp.int32, (16, 2 * PEER_SLOTS), 0) & 7
    row16v = lax.broadcasted_iota(jnp.int32, (16, ROW_WORDS), 0) & 7
    odd = (lax.broadcasted_iota(jnp.int32, (8, 2 * PEER_SLOTS), 1) & 1) == 1

    def fetch(t0, first, last):
        for i in range(8):
            tok_offs = off_ref.at[t0 + i]
            st = stage_scr.at[i]
            for k in range(first, last):
                off = pl.multiple_of(tok_offs[k], PAIR_SUBL)
                st[pl.ds(k, PAIR_SUBL, stride=STAGE_STRIDE), :] = tab_ref[pl.ds(off, PAIR_SUBL), :]

    def group(g, carry):
        t0 = pl.multiple_of(g * 8, 8)
        inside = jnp.bool_(True)
        for i in range(8):
            n_low = nlow_ref[0, 0, t0 + i]
            inside = jnp.logical_and(inside, (n_low <= hi) if lower_half else (n_low >= lo))
        lax.cond(inside, lambda: fetch(t0, lo, hi), lambda: fetch(t0, 0, PEER_SLOTS))

        h8 = h_ref[pl.ds(t0, 8), :]
        h16 = jnp.concatenate([h8[:, :ROW_WORDS], h8[:, ROW_WORDS:]], axis=0).astype(bf16)
        acc = jnp.zeros((16, 2 * PEER_SLOTS), f32)
        for i in range(8):
            res = lax.dot_general(h16, _staged_rows(stage_scr.at[i], 0), (((1,), (1,)), ((), ())),
                                  preferred_element_type=f32)
            acc = jnp.where(row16u == i, res, acc)
        dots = acc[0:8] + pltpu.roll(acc[8:16], 1, axis=1)
        gelu = 0.5 * dots * (1.0 + lax.erf(dots * (2.0 ** -0.5)))
        w8 = jnp.where(odd, gelu * gate_ref[pl.ds(t0, 8), :], 0.0)
        w16 = jnp.concatenate([w8, pltpu.roll(w8, 2 * PEER_SLOTS - 1, axis=1)], axis=0).astype(bf16)
        acc = jnp.zeros((16, ROW_WORDS), f32)
        for i in range(8):
            res = jnp.dot(w16, _staged_rows(stage_scr.at[i], ROW_SUBL), preferred_element_type=f32)
            acc = jnp.where(row16v == i, res, acc)
        y = jnp.concatenate([acc[0:8], acc[8:16]], axis=1)
        if final:
            o_ref[pl.ds(t0, 8), :] = x_ref[pl.ds(t0, 8), :] + g2_ref[0] * (y_ref[pl.ds(t0, 8), :] + y)
        else:
            o_ref[pl.ds(t0, 8), :] = y
        return carry

    lax.fori_loop(0, T // 8, group, 0)


_TABLE_VMEM_LIMIT = 56 * 1024 * 1024
FETCH_SPAN = 80


def peer_pass(off, nlow, h, gate, table, *, lower_half, finish=None, T=128):
    N, D = h.shape
    lo, hi = (0, FETCH_SPAN) if lower_half else (PEER_SLOTS - FETCH_SPAN, PEER_SLOTS)
    row = pl.BlockSpec((T, D), lambda i: (i, 0))
    in_specs = [
        pl.BlockSpec((T, PEER_SLOTS), lambda i: (i, 0), memory_space=pltpu.SMEM),
        pl.BlockSpec((1, 1, T), lambda i: (i, 0, 0), memory_space=pltpu.SMEM),
        row,
        pl.BlockSpec((T, 2 * PEER_SLOTS), lambda i: (i, 0)),
        pl.BlockSpec((HALF_EXPERTS * PAIR_SUBL, 128), lambda i: (0, 0), pipeline_mode=pl.Buffered(1)),
    ]
    args = [off, nlow, h, gate, table]
    if finish is not None:
        y_first, x, g2 = finish
        per_batch = N // g2.shape[0]
        in_specs += [row, row, pl.BlockSpec((1, 1, D), lambda i: (i * T // per_batch, 0, 0))]
        args += [y_first, x, g2.reshape(g2.shape[0], 1, D)]
    return pl.pallas_call(
        functools.partial(_peer_pass_kernel, T=T, lo=lo, hi=hi, lower_half=lower_half,
                          final=finish is not None),
        grid=(N // T,),
        in_specs=in_specs,
        out_specs=row,
        out_shape=jax.ShapeDtypeStruct((N, D), f32),
        scratch_shapes=[pltpu.VMEM((8, PAIR_SUBL * STAGE_STRIDE, 128), jnp.int32)],
        compiler_params=pltpu.CompilerParams(dimension_semantics=("arbitrary",),
                                             vmem_limit_bytes=_TABLE_VMEM_LIMIT),
        name="peer_pass",
    )(*args)


def kernel(x, c, w_ada, b_ada, w_in, q_gain, k_gain, w_uv, conv_w, a_log, dt_bias,
           gdn_gain, w_out, w_query, sub_keys, u_tab, v_tab):
    B, S, D = x.shape
    xf = x.reshape(B * S, D)
    for l in range(DEPTH):
        sh1, sc1, g1, sh2, sc2, g2 = jnp.split(adaln(c, w_ada[l], b_ada[l]), 6, axis=-1)
        aq, akv, iq, small, qkvz = in_projection(xf, sh1, sc1, w_in[l])
        small = small.reshape(B, S, -1)
        ik, iw, ba, bb = (small[..., lo:hi] for lo, hi in (SMALL_IK, SMALL_IW, SMALL_BA, SMALL_BB))
        ya = dsa_attention(aq.reshape(B, S, -1), akv.reshape(B, S, -1), iq.reshape(B, S, -1), ik, iw,
                           q_gain[l], k_gain[l], w_uv[l])
        yb = gated_deltanet(qkvz.reshape(B, S, -1), ba, bb, conv_w[l], a_log[l], dt_bias[l], gdn_gain[l])
        x1, h2 = out_projection(ya.reshape(B * S, -1), yb.reshape(B * S, -1), xf, g1, sh2, sc2, w_out[l])
        off, gate_lo, gate_hi, nlow = peer_route(h2, w_query[l], sub_keys[l])
        tab_lo, tab_hi = pack_pair_tables(u_tab[l], v_tab[l])
        y_lo = peer_pass(off, nlow, h2, gate_lo, tab_lo, lower_half=True)
        xf = peer_pass(off, nlow, h2, gate_hi, tab_hi, lower_half=False, finish=(y_lo, x1, g2))
    return xf.reshape(B, S, D)
```

```python
import functools
import math
import jax
import jax.numpy as jnp
from jax import lax
import numpy as np
from jax.experimental import pallas as pl
from jax.experimental.pallas import tpu as pltpu

D_MODEL = 1024
BATCH = 8
SEQ = 4096
DEPTH = 1

EPS = 1e-6
A_HEADS = 8
A_LAT = 128
A_VDIM = 64
IDX_HEADS = 4
IDX_DIM = 64
IDX_TOPK_MAX = 256
Q_BLOCK = 128
B_HEADS = 4
B_DIM = 128
CONV_W = 4
CHUNK = 64
PEER_HEADS = 8
PEER_NKEYS = 128
PEER_QDIM = 256
PEER_TOPK = 16
PEER_BLOCK = 128
N_EXPERTS = PEER_NKEYS * PEER_NKEYS

A_WIDTH = A_HEADS * A_VDIM
B_WIDTH = B_HEADS * B_DIM
MIX_WIDTH = A_WIDTH + B_WIDTH
IN_SPLITS = (A_HEADS * A_LAT, A_LAT, IDX_HEADS * IDX_DIM, IDX_DIM, IDX_HEADS,
             B_WIDTH, B_WIDTH, B_WIDTH, B_WIDTH, B_HEADS, B_HEADS)
IN_WIDTH = sum(IN_SPLITS)


f32 = jnp.float32
bf16 = jnp.bfloat16
INT_MIN = -2 ** 31
SUM_LANE = 2
_PROJ_VMEM_LIMIT = 48 * 1024 * 1024


def _rms_modulate(x, shift, scale):
    xn = x * lax.rsqrt(jnp.mean(x * x, -1, keepdims=True) + EPS)
    return xn * (1.0 + scale) + shift


def _adaln_kernel(c_ref, w_ref, b_ref, o_ref):
    c = c_ref[...]
    s = (c * jax.nn.sigmoid(c)).astype(bf16)
    o_ref[...] = jnp.dot(s, w_ref[...].astype(bf16), preferred_element_type=f32) + b_ref[...]


def adaln(c, w, b, *, tn=512):
    B, D = c.shape
    N = w.shape[1]
    return pl.pallas_call(
        _adaln_kernel,
        grid=(N // tn,),
        in_specs=[pl.BlockSpec((B, D), lambda j: (0, 0)),
                  pl.BlockSpec((D, tn), lambda j: (0, j)),
                  pl.BlockSpec((1, tn), lambda j: (0, j))],
        out_specs=pl.BlockSpec((B, tn), lambda j: (0, j)),
        out_shape=jax.ShapeDtypeStruct((B, N), f32),
        name="adaln",
    )(c, w, b.reshape(1, N))


SMALL_IK = (0, IDX_DIM)
SMALL_IW = (IDX_DIM, IDX_DIM + IDX_HEADS)
SMALL_BA = (IDX_DIM + IDX_HEADS, IDX_DIM + IDX_HEADS + B_HEADS)
SMALL_BB = (IDX_DIM + IDX_HEADS + B_HEADS, IDX_DIM + IDX_HEADS + 2 * B_HEADS)


def _inproj_kernel(x_ref, sh_ref, sc_ref, wq_ref, wkv_ref, wiq_ref, wsm_ref, wb_ref,
                   oq_ref, okv_ref, oiq_ref, osm_ref, ob_ref):
    h = _rms_modulate(x_ref[...], sh_ref[0], sc_ref[0]).astype(bf16)
    for w_ref, o_ref in ((wq_ref, oq_ref), (wkv_ref, okv_ref), (wiq_ref, oiq_ref),
                         (wsm_ref, osm_ref), (wb_ref, ob_ref)):
        o_ref[...] = jnp.dot(h, w_ref[...], preferred_element_type=f32)


def in_projection(x, shift, scale, w_in, *, tm=512):
    N, D = x.shape
    per_batch = N // shift.shape[0]
    o = [int(v) for v in np.cumsum((0,) + IN_SPLITS)]
    wb = w_in.astype(bf16)
    w_q, w_kv, w_iq = wb[:, o[0]:o[1]], wb[:, o[1]:o[2]], wb[:, o[2]:o[3]]
    w_small = jnp.concatenate([wb[:, o[3]:o[5]], wb[:, o[9]:o[11]],
                               jnp.zeros((D, 128 - SMALL_BB[1]), bf16)], axis=1)
    w_b = wb[:, o[5]:o[9]]
    weights = (w_q, w_kv, w_iq, w_small, w_b)
    mod = pl.BlockSpec((1, 1, D), lambda i: (i * tm // per_batch, 0, 0))
    return pl.pallas_call(
        _inproj_kernel,
        grid=(N // tm,),
        in_specs=[pl.BlockSpec((tm, D), lambda i: (i, 0)), mod, mod]
                 + [pl.BlockSpec(w.shape, lambda i: (0, 0), pipeline_mode=pl.Buffered(1)) for w in weights],
        out_specs=[pl.BlockSpec((tm, w.shape[1]), lambda i: (i, 0)) for w in weights],
        out_shape=[jax.ShapeDtypeStruct((N, w.shape[1]), f32) for w in weights],
        compiler_params=pltpu.CompilerParams(dimension_semantics=("arbitrary",),
                                             vmem_limit_bytes=_PROJ_VMEM_LIMIT),
        name="in_projection",
    )(x, shift[:, None, :], scale[:, None, :], *weights)


def _outproj_kernel(ya_ref, yb_ref, x_ref, g1_ref, sh_ref, sc_ref, wa_ref, wb_ref, x1_ref, h2_ref):
    mix = (jnp.dot(ya_ref[...].astype(bf16), wa_ref[...], preferred_element_type=f32)
           + jnp.dot(yb_ref[...].astype(bf16), wb_ref[...], preferred_element_type=f32))
    x1 = x_ref[...] + g1_ref[0] * mix
    x1_ref[...] = x1
    h2_ref[...] = _rms_modulate(x1, sh_ref[0], sc_ref[0])


def out_projection(ya, yb, x, gate, shift, scale, w_out, *, tm=512):
    N, D = x.shape
    per_batch = N // gate.shape[0]
    wb = w_out.astype(bf16)
    w_a, w_b = wb[:A_WIDTH], wb[A_WIDTH:]
    mod = pl.BlockSpec((1, 1, D), lambda i: (i * tm // per_batch, 0, 0))
    row = pl.BlockSpec((tm, D), lambda i: (i, 0))
    half = pl.BlockSpec((tm, A_WIDTH), lambda i: (i, 0))
    return pl.pallas_call(
        _outproj_kernel,
        grid=(N // tm,),
        in_specs=[half, half, row, mod, mod, mod,
                  pl.BlockSpec(w_a.shape, lambda i: (0, 0)), pl.BlockSpec(w_b.shape, lambda i: (0, 0))],
        out_specs=[row, row],
        out_shape=[jax.ShapeDtypeStruct((N, D), f32), jax.ShapeDtypeStruct((N, D), f32)],
        compiler_params=pltpu.CompilerParams(dimension_semantics=("arbitrary",),
                                             vmem_limit_bytes=_PROJ_VMEM_LIMIT),
        name="out_projection",
    )(ya, yb, x, gate[:, None, :], shift[:, None, :], scale[:, None, :], w_a, w_b)


def _dsa_kernel(q_ref, kv_ref, iq_ref, ik_ref, iw_ref, qg_ref, kg_ref, wbd_ref, tri_ref, o_ref,
                kaug_scr, ikb_scr, key_scr, qaug_scr, *head_scr, S, TQ, TK, topk):
    qi = pl.program_id(1)
    scale = A_LAT ** -0.5
    idx_scale = (IDX_DIM ** -0.5) * (IDX_HEADS ** -0.5)

    @pl.when(qi == 0)
    def _prep_keys():
        kv = kv_ref[0]
        kn = kv * lax.rsqrt(jnp.mean(kv * kv, -1, keepdims=True) + EPS) * kg_ref[...]
        pos = lax.broadcasted_iota(jnp.int32, (S, A_LAT), 0)
        lane = lax.broadcasted_iota(jnp.int32, (S, A_LAT), 1)
        hi = (pos >> 6).astype(f32)
        lo = (pos & 63).astype(f32)
        extra = jnp.where(lane == 0, hi, jnp.where(lane == 1, lo, jnp.where(lane == SUM_LANE, 1.0, 0.0)))
        kaug_scr[:, :A_LAT] = kn.astype(bf16)
        kaug_scr[:, A_LAT:] = extra.astype(bf16)
        ikb_scr[...] = ik_ref[0].astype(bf16)

    q = q_ref[0]
    lane = lax.broadcasted_iota(jnp.int32, (TQ, A_LAT), 1)
    for h in range(A_HEADS):
        qh = q[:, h * A_LAT:(h + 1) * A_LAT]
        qn = qh * lax.rsqrt(jnp.mean(qh * qh, -1, keepdims=True) + EPS) * qg_ref[...] * scale
        slope = 2.0 ** (-8.0 * (h + 1) / A_HEADS)
        extra = jnp.where(lane == 0, slope * 64.0, jnp.where(lane == 1, slope, 0.0))
        qaug_scr[h * TQ:(h + 1) * TQ, :A_LAT] = qn.astype(bf16)
        qaug_scr[h * TQ:(h + 1) * TQ, A_LAT:] = extra.astype(bf16)

    nchunks = (qi * TQ + TQ + TK - 1) // TK
    qpos = qi * TQ + lax.broadcasted_iota(jnp.int32, (TQ, 1), 0)
    colb = lax.broadcasted_iota(jnp.int32, (TQ, TK), 1)
    iq = iq_ref[0]
    iqb = [iq[:, h * IDX_DIM:(h + 1) * IDX_DIM].astype(bf16) for h in range(IDX_HEADS)]
    iw = iw_ref[0]

    def score_chunk(c, carry):
        off = pl.multiple_of(c * TK, TK)
        ikc = ikb_scr[pl.ds(off, TK), :]
        acc = jnp.zeros((TQ, TK), f32)
        for h in range(IDX_HEADS):
            lg = lax.dot_general(iqb[h], ikc, (((1,), (1,)), ((), ())), preferred_element_type=f32)
            acc = acc + jnp.maximum(lg, 0.0) * iw[:, h:h + 1]
        sc = acc * idx_scale
        sc = jnp.where(colb + off <= qpos, sc, -jnp.inf)
        bits = pltpu.bitcast(sc, jnp.int32)
        key_scr[:, pl.ds(off, TK)] = jnp.where(bits < 0, bits ^ jnp.int32(0x7FFFFFFF), bits)
        return carry

    lax.fori_loop(0, nchunks, score_chunk, 0)

    def count(pred_fn):
        def body(c, cnt):
            off = pl.multiple_of(c * TK, TK)
            hit = pred_fn(key_scr[:, pl.ds(off, TK)]).astype(jnp.int32)
            part = hit[:, 0:128]
            for j in range(1, TK // 128):
                part = part + hit[:, j * 128:(j + 1) * 128]
            return cnt + part
        cnt = lax.fori_loop(0, nchunks, body, jnp.zeros((TQ, 128), jnp.int32))
        return jnp.sum(cnt, axis=1, keepdims=True)

    def bit_step(i, T):
        cand = T + lax.shift_left(jnp.int32(1), jnp.int32(31) - i)
        cnt = count(lambda k: k >= cand)
        return jnp.where(cnt >= topk, cand, T)

    T = lax.fori_loop(0, 32, bit_step, jnp.full((TQ, 1), INT_MIN, jnp.int32))
    n_gt = count(lambda k: k > T)
    room = (topk - n_gt).astype(f32)

    m_scr, acc_scr = head_scr[:A_HEADS], head_scr[A_HEADS:]
    for h in range(A_HEADS):
        m_scr[h][...] = jnp.full((TQ, 1), -jnp.inf, f32)
        acc_scr[h][...] = jnp.zeros((TQ, 2 * A_LAT), f32)

    def attend_chunk(c, ties_before):
        off = pl.multiple_of(c * TK, TK)
        keyc = key_scr[:, pl.ds(off, TK)]
        eq = keyc == T
        eqb = jnp.where(eq, 1.0, 0.0).astype(bf16)
        ties = ties_before
        admit = []
        for j in range(TK // 128):
            blk = slice(j * 128, (j + 1) * 128)
            cnt = jnp.dot(eqb[:, blk], tri_ref[...], preferred_element_type=f32)
            admit.append(eq[:, blk] & (cnt[:, :128] + ties < room))
            ties = ties + cnt[:, 128:]
        sel = (keyc > T) | jnp.concatenate(admit, axis=1)
        sel = sel & (colb + off <= qpos)
        kc = kaug_scr[pl.ds(off, TK), :]
        scores = lax.dot_general(qaug_scr[...], kc, (((1,), (1,)), ((), ())), preferred_element_type=f32)
        probs, alphas = [], []
        for h in range(A_HEADS):
            s = jnp.where(sel, scores[h * TQ:(h + 1) * TQ], -jnp.inf)
            m_old = m_scr[h][...]
            m_new = jnp.maximum(m_old, jnp.max(s, axis=1, keepdims=True))
            m_safe = jnp.where(m_new == -jnp.inf, 0.0, m_new)
            probs.append(jnp.exp(s - m_safe).astype(bf16))
            alphas.append(jnp.exp(m_old - m_safe))
            m_scr[h][...] = m_new
        for h in range(A_HEADS):
            acc_scr[h][...] = alphas[h] * acc_scr[h][...] + jnp.dot(probs[h], kc, preferred_element_type=f32)
        return ties

    lax.fori_loop(0, nchunks, attend_chunk, jnp.zeros((TQ, 128), f32))

    heads = []
    for h in range(A_HEADS):
        acc = acc_scr[h][...]
        row_sum = acc[:, A_LAT + SUM_LANE:A_LAT + SUM_LANE + 1]
        heads.append((acc[:, :A_LAT] / row_sum).astype(bf16))
    o = jnp.concatenate(heads, axis=1)
    o_ref[0] = jnp.dot(o, wbd_ref[...], preferred_element_type=f32)


def dsa_attention(q_lat, kv, iq, ik, iw, q_gain, k_gain, w_uv, *, TQ=128, TK=1024):
    B, S, _ = q_lat.shape
    topk = min(IDX_TOPK_MAX, S // 4)
    TK = min(TK, S)
    wbd = jnp.zeros((A_HEADS * A_LAT, A_WIDTH), f32)
    for h in range(A_HEADS):
        wbd = wbd.at[h * A_LAT:(h + 1) * A_LAT, h * A_VDIM:(h + 1) * A_VDIM].set(w_uv[h])
    blk = jnp.arange(128)
    tri = jnp.concatenate([blk[:, None] < blk[None, :], jnp.ones((128, 128), bool)], axis=1).astype(bf16)
    kern = functools.partial(_dsa_kernel, S=S, TQ=TQ, TK=TK, topk=topk)
    return pl.pallas_call(
        kern,
        grid=(B, S // TQ),
        in_specs=[
            pl.BlockSpec((1, TQ, A_HEADS * A_LAT), lambda b, i: (b, i, 0)),
            pl.BlockSpec((1, S, A_LAT), lambda b, i: (b, 0, 0)),
            pl.BlockSpec((1, TQ, IDX_HEADS * IDX_DIM), lambda b, i: (b, i, 0)),
            pl.BlockSpec((1, S, IDX_DIM), lambda b, i: (b, 0, 0)),
            pl.BlockSpec((1, TQ, IDX_HEADS), lambda b, i: (b, i, 0)),
            pl.BlockSpec((1, A_LAT), lambda b, i: (0, 0)),
            pl.BlockSpec((1, A_LAT), lambda b, i: (0, 0)),
            pl.BlockSpec((A_HEADS * A_LAT, A_WIDTH), lambda b, i: (0, 0)),
            pl.BlockSpec((128, 256), lambda b, i: (0, 0)),
        ],
        out_specs=pl.BlockSpec((1, TQ, A_WIDTH), lambda b, i: (b, i, 0)),
        out_shape=jax.ShapeDtypeStruct((B, S, A_WIDTH), f32),
        scratch_shapes=[
            pltpu.VMEM((S, 2 * A_LAT), bf16),
            pltpu.VMEM((S, IDX_DIM), bf16),
            pltpu.VMEM((TQ, S), jnp.int32),
            pltpu.VMEM((A_HEADS * TQ, 2 * A_LAT), bf16),
            *[pltpu.VMEM((TQ, 1), f32) for _ in range(A_HEADS)],
            *[pltpu.VMEM((TQ, 2 * A_LAT), f32) for _ in range(A_HEADS)],
        ],
        compiler_params=pltpu.CompilerParams(dimension_semantics=("arbitrary", "arbitrary")),
        name="dsa_attention",
    )(q_lat, kv, iq, ik, iw, q_gain.reshape(1, -1), k_gain.reshape(1, -1), wbd.astype(bf16), tri)


CONV_HALO = 8


def _split_bf16(x, terms):
    parts = []
    for _ in range(terms):
        p = x.astype(bf16)
        parts.append(p)
        x = x - p.astype(f32)
    return parts


def _dot_exact_lhs(mask_bf16, x):
    return sum(jnp.dot(mask_bf16, p, preferred_element_type=f32) for p in _split_bf16(x, 3))


def _dot_exact_rhs(x, mask_bf16):
    return sum(jnp.dot(p, mask_bf16, preferred_element_type=f32) for p in _split_bf16(x, 3))


def _dot_3pass(a, b):
    ah, al = _split_bf16(a, 2)
    bh, bl = _split_bf16(b, 2)
    return (jnp.dot(ah, bh, preferred_element_type=f32) + jnp.dot(ah, bl, preferred_element_type=f32)
            + jnp.dot(al, bh, preferred_element_type=f32))


def _softplus(x):
    return jnp.maximum(x, 0.0) + jnp.log1p(jnp.exp(-jnp.abs(x)))


def _gdn_kernel(q_ref, k_ref, v_ref, z_ref, a_ref, b_ref, at_ref, bt_ref, cw_ref, alog_ref, dtb_ref,
                gain_ref, o_ref, tail_scr, act_scr, *state_scr, SB):
    sj = pl.program_id(1)

    @pl.when(sj == 0)
    def _reset():
        tail_scr[...] = jnp.zeros(tail_scr.shape, f32)
        for st_ref in state_scr:
            st_ref[...] = jnp.zeros((B_DIM, B_DIM), f32)

    cw = cw_ref[...]
    for idx, x_ref in enumerate((q_ref, k_ref, v_ref)):
        x = x_ref[0]
        xc = jnp.concatenate([tail_scr[idx], x], axis=0)
        y = jnp.zeros((SB, B_WIDTH), f32)
        for j in range(CONV_W):
            lo = CONV_HALO - (CONV_W - 1) + j
            y = y + cw[j:j + 1, idx * B_WIDTH:(idx + 1) * B_WIDTH] * xc[lo:lo + SB]
        tail_scr[idx] = x[SB - CONV_HALO:SB]
        act_scr[idx] = y * jax.nn.sigmoid(y)

    ri = lax.broadcasted_iota(jnp.int32, (CHUNK, CHUNK), 0)
    ci = lax.broadcasted_iota(jnp.int32, (CHUNK, CHUNK), 1)
    tril = ri >= ci
    strict = ri > ci
    tril_b = jnp.where(tril, 1.0, 0.0).astype(bf16)
    triu_b = jnp.where(ri <= ci, 1.0, 0.0).astype(bf16)
    eye = jnp.where(ri == ci, 1.0, 0.0)
    nt = (((1,), (1,)), ((), ()))

    def chunk(c, carry):
        rows = pl.ds(pl.multiple_of(c * CHUNK, CHUNK), CHUNK)
        H = range(B_HEADS)
        lanes = [slice(h * B_DIM, (h + 1) * B_DIM) for h in H]
        lp = dict(preferred_element_type=f32)
        q = [act_scr[0, rows, lanes[h]] for h in H]
        k = [act_scr[1, rows, lanes[h]] for h in H]
        v = [act_scr[2, rows, lanes[h]] for h in H]
        q = [x * lax.rsqrt(jnp.sum(x * x, -1, keepdims=True) + EPS) * (B_DIM ** -0.5) for x in q]
        k = [x * lax.rsqrt(jnp.sum(x * x, -1, keepdims=True) + EPS) for x in k]
        neg_rate = [-jnp.exp(alog_ref[0:1, h:h + 1]) for h in H]
        dtb = [dtb_ref[0:1, h:h + 1] for h in H]
        g_col = [neg_rate[h] * _softplus(a_ref[0, rows, h:h + 1] + dtb[h]) for h in H]
        g_row = [neg_rate[h] * _softplus(at_ref[0, h, pl.ds(c, 1), :] + dtb[h]) for h in H]
        beta = [jax.nn.sigmoid(b_ref[0, rows, h:h + 1]) for h in H]
        Gc = [_dot_exact_lhs(tril_b, jnp.broadcast_to(g_col[h], (CHUNK, CHUNK))) for h in H]
        Gr = [_dot_exact_rhs(jnp.broadcast_to(g_row[h], (CHUNK, CHUNK)), triu_b) for h in H]
        G = [x[:, 0:1] for x in Gc]
        G_last = [x[CHUNK - 1:CHUNK, 0:1] for x in Gc]
        decay = [jnp.exp(jnp.where(tril, Gc[h] - Gr[h], -jnp.inf)) for h in H]
        kb = [k[h] * beta[h] for h in H]
        kbf = [x.astype(bf16) for x in k]
        L = [jnp.where(strict, lax.dot_general(kb[h].astype(bf16), kbf[h], nt, **lp) * decay[h], 0.0) for h in H]
        P = [eye - x for x in L]
        M = [_dot_3pass(x, x) for x in L]
        for lvl in range(5):
            P = [P[h] + _dot_3pass(P[h], M[h]) for h in H]
            if lvl < 4:
                M = [_dot_3pass(x, x) for x in M]
        Tb = [x.astype(bf16) for x in P]
        eG = [jnp.exp(x) for x in G]
        u = [jnp.dot(Tb[h], (v[h] * beta[h]).astype(bf16), **lp) for h in H]
        w = [jnp.dot(Tb[h], (kb[h] * eG[h]).astype(bf16), **lp) for h in H]
        attn = [lax.dot_general(q[h].astype(bf16), kbf[h], nt, **lp) * decay[h] for h in H]
        q_dec = [q[h] * eG[h] for h in H]
        k_dec = [k[h] * jnp.exp(G_last[h] - G[h]) for h in H]
        st = [state_scr[h][...] for h in H]
        stb = [x.astype(bf16) for x in st]
        v_new = [u[h] - jnp.dot(w[h].astype(bf16), stb[h], **lp) for h in H]
        vnb = [x.astype(bf16) for x in v_new]
        o = [jnp.dot(q_dec[h].astype(bf16), stb[h], **lp) + jnp.dot(attn[h].astype(bf16), vnb[h], **lp) for h in H]
        for h in H:
            state_scr[h][...] = st[h] * jnp.exp(G_last[h]) + lax.dot_general(
                k_dec[h].astype(bf16), vnb[h], (((0,), (0,)), ((), ())), **lp)
        for h in H:
            zh = z_ref[0, rows, lanes[h]]
            on = o[h] * lax.rsqrt(jnp.mean(o[h] * o[h], -1, keepdims=True) + EPS) * gain_ref[...]
            o_ref[0, rows, lanes[h]] = on * (zh * jax.nn.sigmoid(zh))
        return carry

    lax.fori_loop(0, SB // CHUNK, chunk, 0)


def gated_deltanet(qkvz, a, b, conv_w, a_log, dt_bias, norm_gain, *, SB=512):
    B, S, _ = qkvz.shape
    SB = min(SB, S)
    nch = S // CHUNK
    at = a.transpose(0, 2, 1).reshape(B, B_HEADS, nch, CHUNK)
    bt = b.transpose(0, 2, 1).reshape(B, B_HEADS, nch, CHUNK)
    wide = pl.BlockSpec((1, SB, B_WIDTH), lambda bi, j: (bi, j, 0))
    part = [pl.BlockSpec((1, SB, B_WIDTH), functools.partial(lambda bi, j, n: (bi, j, n), n=n))
            for n in range(4)]
    narrow = pl.BlockSpec((1, SB, B_HEADS), lambda bi, j: (bi, j, 0))
    rowwise = pl.BlockSpec((1, B_HEADS, SB // CHUNK, CHUNK), lambda bi, j: (bi, 0, j, 0))
    return pl.pallas_call(
        functools.partial(_gdn_kernel, SB=SB),
        grid=(B, S // SB),
        in_specs=[*part, narrow, narrow, rowwise, rowwise,
                  pl.BlockSpec((CONV_W, 3 * B_WIDTH), lambda bi, j: (0, 0)),
                  pl.BlockSpec((1, B_HEADS), lambda bi, j: (0, 0)),
                  pl.BlockSpec((1, B_HEADS), lambda bi, j: (0, 0)),
                  pl.BlockSpec((1, B_DIM), lambda bi, j: (0, 0))],
        out_specs=wide,
        out_shape=jax.ShapeDtypeStruct((B, S, B_WIDTH), f32),
        scratch_shapes=[pltpu.VMEM((3, CONV_HALO, B_WIDTH), f32),
                        pltpu.VMEM((3, SB, B_WIDTH), f32),
                        *[pltpu.VMEM((B_DIM, B_DIM), f32) for _ in range(B_HEADS)]],
        compiler_params=pltpu.CompilerParams(dimension_semantics=("arbitrary", "arbitrary")),
        name="gated_deltanet",
    )(qkvz, qkvz, qkvz, qkvz, a, b, at, bt, conv_w,
      a_log.reshape(1, -1), dt_bias.reshape(1, -1), norm_gain.reshape(1, -1))


PEER_SLOTS = PEER_HEADS * PEER_TOPK
PEER_HALF = PEER_QDIM // 2
ROW_WORDS = D_MODEL // 2
ROW_SUBL = ROW_WORDS // 128
STAGE_STRIDE = PEER_SLOTS + 8


def _top16_rows(s, iota_rows, fill, vals_scr, idx_scr, payload=None, pay_scr=None):
    for r in range(PEER_TOPK):
        m = jnp.max(s, axis=0, keepdims=True)
        first = jnp.min(jnp.where(s == m, iota_rows, fill), axis=0, keepdims=True)
        taken = iota_rows == first
        vals_scr[r:r + 1, :] = m
        if payload is None:
            idx_scr[r:r + 1, :] = first.astype(jnp.int32)
        else:
            pay_scr[r:r + 1, :] = jnp.max(jnp.where(taken, payload, -1.0), axis=0,
                                          keepdims=True).astype(jnp.int32)
        s = jnp.where(taken, -jnp.inf, s)


def _peer_route_kernel(h_ref, wq_ref, sk_ref, off_ref, gate_ref,
                       v1_scr, i1_scr, v2_scr, i2_scr, cv_scr, ce_scr, eid_scr, gate_scr, *, T):
    q = jnp.dot(h_ref[...].astype(bf16), wq_ref[...], preferred_element_type=f32).astype(bf16)
    gate_scr[...] = jnp.zeros(gate_scr.shape, f32)
    kiota = lax.broadcasted_iota(jnp.int32, (PEER_NKEYS, T), 0).astype(f32)
    sub8 = lax.broadcasted_iota(jnp.int32, (8, T), 0).astype(f32)
    for p in range(PEER_HEADS):
        for half, (vs, is_) in enumerate(((v1_scr, i1_scr), (v2_scr, i2_scr))):
            g = 2 * p + half
            s = lax.dot_general(sk_ref[g], q[:, g * PEER_HALF:(g + 1) * PEER_HALF],
                                (((1,), (1,)), ((), ())), preferred_element_type=f32)
            _top16_rows(s, kiota, PEER_NKEYS, vs, is_)
        v1 = v1_scr[...]
        v2 = v2_scr[...]
        e1 = (i1_scr[...] * PEER_NKEYS).astype(f32)
        e2 = i2_scr[...].astype(f32)
        cv = [v1[0:1] + v2[0:8], v1[0:1] + v2[8:16]]
        ce = [e1[0:1] + e2[0:8], e1[0:1] + e2[8:16]]
        cf = [sub8, sub8 + 8]
        for a in range(1, 8):
            cv.append(v1[a:a + 1] + v2[0:8])
            ce.append(e1[a:a + 1] + e2[0:8])
            cf.append(sub8 + a * PEER_TOPK)
        cv.append(v1[8:16] + v2[0:1])
        ce.append(e1[8:16] + e2[0:1])
        cf.append((sub8 + 8) * PEER_TOPK)
        cand = jnp.concatenate(cv, axis=0)
        cexp = jnp.concatenate(ce, axis=0)
        cflat = jnp.concatenate(cf, axis=0)
        _top16_rows(cand, cflat, PEER_TOPK * PEER_TOPK, cv_scr, None, payload=cexp, pay_scr=ce_scr)
        top = cv_scr[...]
        ex = jnp.exp(top - top[0:1])
        gate_scr[pl.ds(2 * p * PEER_TOPK + 1, PEER_TOPK, stride=2), :] = ex / jnp.sum(ex, axis=0, keepdims=True)
        eid_scr[p * PEER_TOPK:(p + 1) * PEER_TOPK, :] = ce_scr[...] * ROW_SUBL
    off_ref[...] = eid_scr[...]
    gate_ref[...] = gate_scr[...].T


def peer_route(h, w_query, sub_keys, *, T=128):
    N, D = h.shape
    sk = sub_keys.reshape(PEER_HEADS * 2, PEER_NKEYS, PEER_HALF).astype(bf16)
    kern = functools.partial(_peer_route_kernel, T=T)
    return pl.pallas_call(
        kern,
        grid=(N // T,),
        in_specs=[
            pl.BlockSpec((T, D), lambda i: (i, 0)),
            pl.BlockSpec((D, PEER_HEADS * PEER_QDIM), lambda i: (0, 0)),
            pl.BlockSpec((PEER_HEADS * 2, PEER_NKEYS, PEER_HALF), lambda i: (0, 0, 0)),
        ],
        out_specs=[pl.BlockSpec((PEER_SLOTS, T), lambda i: (0, i)),
                   pl.BlockSpec((T, 2 * PEER_SLOTS), lambda i: (i, 0))],
        out_shape=[jax.ShapeDtypeStruct((PEER_SLOTS, N), jnp.int32),
                   jax.ShapeDtypeStruct((N, 2 * PEER_SLOTS), f32)],
        scratch_shapes=[
            pltpu.VMEM((PEER_TOPK, T), f32), pltpu.VMEM((PEER_TOPK, T), jnp.int32),
            pltpu.VMEM((PEER_TOPK, T), f32), pltpu.VMEM((PEER_TOPK, T), jnp.int32),
            pltpu.VMEM((PEER_TOPK, T), f32), pltpu.VMEM((PEER_TOPK, T), jnp.int32),
            pltpu.VMEM((PEER_SLOTS, T), jnp.int32), pltpu.VMEM((2 * PEER_SLOTS, T), f32),
        ],
        compiler_params=pltpu.CompilerParams(dimension_semantics=("arbitrary",)),
        name="peer_route",
    )(h, w_query.astype(bf16), sk)


def pack_table(tab):
    bits = lax.bitcast_convert_type(tab.astype(bf16), jnp.uint16).astype(jnp.uint32)
    words = (bits[:, :ROW_WORDS] << 16) | bits[:, ROW_WORDS:]
    return lax.bitcast_convert_type(words, jnp.int32).reshape(tab.shape[0] * ROW_SUBL, 128)


GROUP = 8


def _fetch_slots(off_ref, t0, tab_ref, stage, first, last):
    for k in range(first, last):
        offs = off_ref.at[k, pl.ds(t0, GROUP)]
        for i in range(GROUP):
            off = pl.multiple_of(offs[i], ROW_SUBL)
            stage[i, pl.ds(k, ROW_SUBL, stride=STAGE_STRIDE), :] = tab_ref[pl.ds(off, ROW_SUBL), :]


def _staged_rows(stage, i):
    chunks = [stage[i, c * STAGE_STRIDE:c * STAGE_STRIDE + PEER_SLOTS, :] for c in range(ROW_SUBL)]
    return pltpu.bitcast(jnp.concatenate(chunks, axis=1), bf16)


def _pipelined_groups(off_ref, tab_ref, stage_a, stage_b, start, step, finish, T):
    ngroups = T // GROUP
    per_step = PEER_SLOTS // GROUP
    _fetch_slots(off_ref, 0, tab_ref, stage_a, 0, PEER_SLOTS)

    def run(g, stage, g_next, stage_next):
        t_next = pl.multiple_of(g_next * GROUP, GROUP)
        carry = start(g)
        for i in range(GROUP):
            carry = step(i, carry, stage)
            _fetch_slots(off_ref, t_next, tab_ref, stage_next, i * per_step, (i + 1) * per_step)
        finish(g, carry)

    def pair(j, carry):
        g0 = 2 * j
        run(g0, stage_a, g0 + 1, stage_b)
        run(g0 + 1, stage_b, jnp.minimum(g0 + 2, ngroups - 1), stage_a)
        return carry

    lax.fori_loop(0, ngroups // 2, pair, 0)


def _peer_u_kernel(off_ref, h_ref, gate_ref, tab_ref, w_ref, stage_a, stage_b, *, T):
    row16 = lax.broadcasted_iota(jnp.int32, (2 * GROUP, 2 * PEER_SLOTS), 0) & (GROUP - 1)
    odd = (lax.broadcasted_iota(jnp.int32, (GROUP, 2 * PEER_SLOTS), 1) & 1) == 1

    def start(g):
        h8 = h_ref[pl.ds(pl.multiple_of(g * GROUP, GROUP), GROUP), :]
        h16 = jnp.concatenate([h8[:, :ROW_WORDS], h8[:, ROW_WORDS:]], axis=0).astype(bf16)
        return h16, jnp.zeros((2 * GROUP, 2 * PEER_SLOTS), f32)

    def step(i, carry, stage):
        h16, acc = carry
        res = lax.dot_general(h16, _staged_rows(stage, i), (((1,), (1,)), ((), ())),
                              preferred_element_type=f32)
        return h16, jnp.where(row16 == i, res, acc)

    def finish(g, carry):
        t0 = pl.multiple_of(g * GROUP, GROUP)
        acc = carry[1]
        dots = acc[0:GROUP] + pltpu.roll(acc[GROUP:], 1, axis=1)
        gelu = 0.5 * dots * (1.0 + lax.erf(dots * (2.0 ** -0.5)))
        w_ref[pl.ds(t0, GROUP), :] = jnp.where(odd, gelu * gate_ref[pl.ds(t0, GROUP), :], 0.0)

    _pipelined_groups(off_ref, tab_ref, stage_a, stage_b, start, step, finish, T)


def _peer_v_kernel(off_ref, w_ref, x_ref, g2_ref, tab_ref, o_ref, stage_a, stage_b, *, T):
    row16 = lax.broadcasted_iota(jnp.int32, (2 * GROUP, ROW_WORDS), 0) & (GROUP - 1)

    def start(g):
        w8 = w_ref[pl.ds(pl.multiple_of(g * GROUP, GROUP), GROUP), :]
        w16 = jnp.concatenate([w8, pltpu.roll(w8, 2 * PEER_SLOTS - 1, axis=1)], axis=0).astype(bf16)
        return w16, jnp.zeros((2 * GROUP, ROW_WORDS), f32)

    def step(i, carry, stage):
        w16, acc = carry
        res = jnp.dot(w16, _staged_rows(stage, i), preferred_element_type=f32)
        return w16, jnp.where(row16 == i, res, acc)

    def finish(g, carry):
        t0 = pl.multiple_of(g * GROUP, GROUP)
        acc = carry[1]
        y = jnp.concatenate([acc[0:GROUP], acc[GROUP:]], axis=1)
        o_ref[pl.ds(t0, GROUP), :] = x_ref[pl.ds(t0, GROUP), :] + g2_ref[0] * y

    _pipelined_groups(off_ref, tab_ref, stage_a, stage_b, start, step, finish, T)


_TABLE_VMEM_LIMIT = 56 * 1024 * 1024


def _table_spec():
    return pl.BlockSpec((N_EXPERTS * ROW_SUBL, 128), lambda i: (0, 0), pipeline_mode=pl.Buffered(1))


def _stage_scratch():
    return [pltpu.VMEM((GROUP, ROW_SUBL * STAGE_STRIDE, 128), jnp.int32) for _ in range(2)]


def _offset_spec(T):
    return pl.BlockSpec((PEER_SLOTS, T), lambda i: (0, i), memory_space=pltpu.SMEM)


def peer_u(off, h, gate, u_pk, *, T=256):
    N, D = h.shape
    return pl.pallas_call(
        functools.partial(_peer_u_kernel, T=T),
        grid=(N // T,),
        in_specs=[
            _offset_spec(T),
            pl.BlockSpec((T, D), lambda i: (i, 0)),
            pl.BlockSpec((T, 2 * PEER_SLOTS), lambda i: (i, 0)),
            _table_spec(),
        ],
        out_specs=pl.BlockSpec((T, 2 * PEER_SLOTS), lambda i: (i, 0)),
        out_shape=jax.ShapeDtypeStruct((N, 2 * PEER_SLOTS), f32),
        scratch_shapes=_stage_scratch(),
        compiler_params=pltpu.CompilerParams(dimension_semantics=("arbitrary",),
                                             vmem_limit_bytes=_TABLE_VMEM_LIMIT),
        name="peer_u",
    )(off, h, gate, u_pk)


def peer_v(off, w, x, g2, v_pk, *, T=256):
    N, D = x.shape
    per_batch = N // g2.shape[0]
    return pl.pallas_call(
        functools.partial(_peer_v_kernel, T=T),
        grid=(N // T,),
        in_specs=[
            _offset_spec(T),
            pl.BlockSpec((T, 2 * PEER_SLOTS), lambda i: (i, 0)),
            pl.BlockSpec((T, D), lambda i: (i, 0)),
            pl.BlockSpec((1, 1, D), lambda i: (i * T // per_batch, 0, 0)),
            _table_spec(),
        ],
        out_specs=pl.BlockSpec((T, D), lambda i: (i, 0)),
        out_shape=jax.ShapeDtypeStruct((N, D), f32),
        scratch_shapes=_stage_scratch(),
        compiler_params=pltpu.CompilerParams(dimension_semantics=("arbitrary",),
                                             vmem_limit_bytes=_TABLE_VMEM_LIMIT),
        name="peer_v",
    )(off, w, x, g2.reshape(g2.shape[0], 1, D), v_pk)


def kernel(x, c, w_ada, b_ada, w_in, q_gain, k_gain, w_uv, conv_w, a_log, dt_bias,
           gdn_gain, w_out, w_query, sub_keys, u_tab, v_tab):
    B, S, D = x.shape
    xf = x.reshape(B * S, D)
    for l in range(DEPTH):
        sh1, sc1, g1, sh2, sc2, g2 = jnp.split(adaln(c, w_ada[l], b_ada[l]), 6, axis=-1)
        aq, akv, iq, small, qkvz = in_projection(xf, sh1, sc1, w_in[l])
        small = small.reshape(B, S, -1)
        ik, iw, ba, bb = (small[..., lo:hi] for lo, hi in (SMALL_IK, SMALL_IW, SMALL_BA, SMALL_BB))
        ya = dsa_attention(aq.reshape(B, S, -1), akv.reshape(B, S, -1), iq.reshape(B, S, -1), ik, iw,
                           q_gain[l], k_gain[l], w_uv[l])
        yb = gated_deltanet(qkvz.reshape(B, S, -1), ba, bb, conv_w[l], a_log[l], dt_bias[l], gdn_gain[l])
        x1, h2 = out_projection(ya.reshape(B * S, -1), yb.reshape(B * S, -1), xf, g1, sh2, sc2, w_out[l])
        off, gate = peer_route(h2, w_query[l], sub_keys[l])
        w = peer_u(off, h2, gate, pack_table(u_tab[l]))
        xf = peer_v(off, w, x1, g2, pack_table(v_tab[l]))
    return xf.reshape(B, S, D)
```

```python
import functools
import math
import jax
import jax.numpy as jnp
from jax import lax
import numpy as np
from jax.experimental import pallas as pl
from jax.experimental.pallas import tpu as pltpu

D_MODEL = 1024
BATCH = 8
SEQ = 4096
DEPTH = 1

EPS = 1e-6
A_HEADS = 8
A_LAT = 128
A_VDIM = 64
IDX_HEADS = 4
IDX_DIM = 64
IDX_TOPK_MAX = 256
Q_BLOCK = 128
B_HEADS = 4
B_DIM = 128
CONV_W = 4
CHUNK = 64
PEER_HEADS = 8
PEER_NKEYS = 128
PEER_QDIM = 256
PEER_TOPK = 16
PEER_BLOCK = 128
N_EXPERTS = PEER_NKEYS * PEER_NKEYS

A_WIDTH = A_HEADS * A_VDIM
B_WIDTH = B_HEADS * B_DIM
MIX_WIDTH = A_WIDTH + B_WIDTH
IN_SPLITS = (A_HEADS * A_LAT, A_LAT, IDX_HEADS * IDX_DIM, IDX_DIM, IDX_HEADS,
             B_WIDTH, B_WIDTH, B_WIDTH, B_WIDTH, B_HEADS, B_HEADS)
IN_WIDTH = sum(IN_SPLITS)


f32 = jnp.float32
bf16 = jnp.bfloat16
INT_MIN = -2 ** 31
SUM_LANE = 2
_PROJ_VMEM_LIMIT = 48 * 1024 * 1024


def _rms_modulate(x, shift, scale):
    xn = x * lax.rsqrt(jnp.mean(x * x, -1, keepdims=True) + EPS)
    return xn * (1.0 + scale) + shift


def _adaln_kernel(c_ref, w_ref, b_ref, o_ref):
    c = c_ref[...]
    s = (c * jax.nn.sigmoid(c)).astype(bf16)
    o_ref[...] = jnp.dot(s, w_ref[...].astype(bf16), preferred_element_type=f32) + b_ref[...]


def adaln(c, w, b, *, tn=512):
    B, D = c.shape
    N = w.shape[1]
    return pl.pallas_call(
        _adaln_kernel,
        grid=(N // tn,),
        in_specs=[pl.BlockSpec((B, D), lambda j: (0, 0)),
                  pl.BlockSpec((D, tn), lambda j: (0, j)),
                  pl.BlockSpec((1, tn), lambda j: (0, j))],
        out_specs=pl.BlockSpec((B, tn), lambda j: (0, j)),
        out_shape=jax.ShapeDtypeStruct((B, N), f32),
        name="adaln",
    )(c, w, b.reshape(1, N))


SMALL_IK = (0, IDX_DIM)
SMALL_IW = (IDX_DIM, IDX_DIM + IDX_HEADS)
SMALL_BA = (IDX_DIM + IDX_HEADS, IDX_DIM + IDX_HEADS + B_HEADS)
SMALL_BB = (IDX_DIM + IDX_HEADS + B_HEADS, IDX_DIM + IDX_HEADS + 2 * B_HEADS)


def _inproj_kernel(x_ref, sh_ref, sc_ref, wq_ref, wkv_ref, wiq_ref, wsm_ref, wb_ref,
                   oq_ref, okv_ref, oiq_ref, osm_ref, ob_ref):
    h = _rms_modulate(x_ref[...], sh_ref[0], sc_ref[0]).astype(bf16)
    for w_ref, o_ref in ((wq_ref, oq_ref), (wkv_ref, okv_ref), (wiq_ref, oiq_ref),
                         (wsm_ref, osm_ref), (wb_ref, ob_ref)):
        o_ref[...] = jnp.dot(h, w_ref[...], preferred_element_type=f32)


def in_projection(x, shift, scale, w_in, *, tm=512):
    N, D = x.shape
    per_batch = N // shift.shape[0]
    o = [int(v) for v in np.cumsum((0,) + IN_SPLITS)]
    wb = w_in.astype(bf16)
    w_q, w_kv, w_iq = wb[:, o[0]:o[1]], wb[:, o[1]:o[2]], wb[:, o[2]:o[3]]
    w_small = jnp.concatenate([wb[:, o[3]:o[5]], wb[:, o[9]:o[11]],
                               jnp.zeros((D, 128 - SMALL_BB[1]), bf16)], axis=1)
    w_b = wb[:, o[5]:o[9]]
    weights = (w_q, w_kv, w_iq, w_small, w_b)
    mod = pl.BlockSpec((1, 1, D), lambda i: (i * tm // per_batch, 0, 0))
    return pl.pallas_call(
        _inproj_kernel,
        grid=(N // tm,),
        in_specs=[pl.BlockSpec((tm, D), lambda i: (i, 0)), mod, mod]
                 + [pl.BlockSpec(w.shape, lambda i: (0, 0), pipeline_mode=pl.Buffered(1)) for w in weights],
        out_specs=[pl.BlockSpec((tm, w.shape[1]), lambda i: (i, 0)) for w in weights],
        out_shape=[jax.ShapeDtypeStruct((N, w.shape[1]), f32) for w in weights],
        compiler_params=pltpu.CompilerParams(dimension_semantics=("arbitrary",),
                                             vmem_limit_bytes=_PROJ_VMEM_LIMIT),
        name="in_projection",
    )(x, shift[:, None, :], scale[:, None, :], *weights)


def _outproj_kernel(ya_ref, yb_ref, x_ref, g1_ref, sh_ref, sc_ref, wa_ref, wb_ref, x1_ref, h2_ref):
    mix = (jnp.dot(ya_ref[...].astype(bf16), wa_ref[...], preferred_element_type=f32)
           + jnp.dot(yb_ref[...].astype(bf16), wb_ref[...], preferred_element_type=f32))
    x1 = x_ref[...] + g1_ref[0] * mix
    x1_ref[...] = x1
    h2_ref[...] = _rms_modulate(x1, sh_ref[0], sc_ref[0])


def out_projection(ya, yb, x, gate, shift, scale, w_out, *, tm=512):
    N, D = x.shape
    per_batch = N // gate.shape[0]
    wb = w_out.astype(bf16)
    w_a, w_b = wb[:A_WIDTH], wb[A_WIDTH:]
    mod = pl.BlockSpec((1, 1, D), lambda i: (i * tm // per_batch, 0, 0))
    row = pl.BlockSpec((tm, D), lambda i: (i, 0))
    half = pl.BlockSpec((tm, A_WIDTH), lambda i: (i, 0))
    return pl.pallas_call(
        _outproj_kernel,
        grid=(N // tm,),
        in_specs=[half, half, row, mod, mod, mod,
                  pl.BlockSpec(w_a.shape, lambda i: (0, 0)), pl.BlockSpec(w_b.shape, lambda i: (0, 0))],
        out_specs=[row, row],
        out_shape=[jax.ShapeDtypeStruct((N, D), f32), jax.ShapeDtypeStruct((N, D), f32)],
        compiler_params=pltpu.CompilerParams(dimension_semantics=("arbitrary",),
                                             vmem_limit_bytes=_PROJ_VMEM_LIMIT),
        name="out_projection",
    )(ya, yb, x, gate[:, None, :], shift[:, None, :], scale[:, None, :], w_a, w_b)


def _dsa_kernel(q_ref, kv_ref, iq_ref, ik_ref, iw_ref, qg_ref, kg_ref, wbd_ref, tri_ref, o_ref,
                kaug_scr, ikb_scr, key_scr, khi_scr, klo_scr, qaug_scr, *head_scr, S, TQ, TK, topk):
    qi = pl.program_id(1)
    scale = A_LAT ** -0.5
    idx_scale = (IDX_DIM ** -0.5) * (IDX_HEADS ** -0.5)

    @pl.when(qi == 0)
    def _prep_keys():
        kv = kv_ref[0]
        kn = kv * lax.rsqrt(jnp.mean(kv * kv, -1, keepdims=True) + EPS) * kg_ref[...]
        pos = lax.broadcasted_iota(jnp.int32, (S, A_LAT), 0)
        lane = lax.broadcasted_iota(jnp.int32, (S, A_LAT), 1)
        hi = (pos >> 6).astype(f32)
        lo = (pos & 63).astype(f32)
        extra = jnp.where(lane == 0, hi, jnp.where(lane == 1, lo, jnp.where(lane == SUM_LANE, 1.0, 0.0)))
        kaug_scr[:, :A_LAT] = kn.astype(bf16)
        kaug_scr[:, A_LAT:] = extra.astype(bf16)
        ikb_scr[...] = ik_ref[0].astype(bf16)

    q = q_ref[0]
    lane = lax.broadcasted_iota(jnp.int32, (TQ, A_LAT), 1)
    for h in range(A_HEADS):
        qh = q[:, h * A_LAT:(h + 1) * A_LAT]
        qn = qh * lax.rsqrt(jnp.mean(qh * qh, -1, keepdims=True) + EPS) * qg_ref[...] * scale
        slope = 2.0 ** (-8.0 * (h + 1) / A_HEADS)
        extra = jnp.where(lane == 0, slope * 64.0, jnp.where(lane == 1, slope, 0.0))
        qaug_scr[h * TQ:(h + 1) * TQ, :A_LAT] = qn.astype(bf16)
        qaug_scr[h * TQ:(h + 1) * TQ, A_LAT:] = extra.astype(bf16)

    nchunks = (qi * TQ + TQ + TK - 1) // TK
    qpos = qi * TQ + lax.broadcasted_iota(jnp.int32, (TQ, 1), 0)
    colb = lax.broadcasted_iota(jnp.int32, (TQ, TK), 1)
    iq = iq_ref[0]
    iqb = [iq[:, h * IDX_DIM:(h + 1) * IDX_DIM].astype(bf16) for h in range(IDX_HEADS)]
    iw = iw_ref[0]

    def score_chunk(c, carry):
        off = pl.multiple_of(c * TK, TK)
        ikc = ikb_scr[pl.ds(off, TK), :]
        acc = jnp.zeros((TQ, TK), f32)
        for h in range(IDX_HEADS):
            lg = lax.dot_general(iqb[h], ikc, (((1,), (1,)), ((), ())), preferred_element_type=f32)
            acc = acc + jnp.maximum(lg, 0.0) * iw[:, h:h + 1]
        sc = acc * idx_scale
        sc = jnp.where(colb + off <= qpos, sc, -jnp.inf)
        bits = pltpu.bitcast(sc, jnp.int32)
        key = jnp.where(bits < 0, bits ^ jnp.int32(0x7FFFFFFF), bits)
        key_scr[:, pl.ds(off, TK)] = key
        khi_scr[:, pl.ds(off, TK)] = (key >> 16).astype(jnp.int16)
        klo_scr[:, pl.ds(off, TK)] = ((key & 0xFFFF) - 32768).astype(jnp.int16)
        return carry

    lax.fori_loop(0, nchunks, score_chunk, 0)

    def count(pred_fn):
        def body(c, cnt):
            off = pl.multiple_of(c * TK, TK)
            hit = pred_fn(key_scr[:, pl.ds(off, TK)]).astype(jnp.int32)
            part = hit[:, 0:128]
            for j in range(1, TK // 128):
                part = part + hit[:, j * 128:(j + 1) * 128]
            return cnt + part
        cnt = lax.fori_loop(0, nchunks, body, jnp.zeros((TQ, 128), jnp.int32))
        return jnp.sum(cnt, axis=1, keepdims=True)

    def count16(ref, pred_fn):
        def body(c, cnt):
            off = pl.multiple_of(c * TK, TK)
            hit = pred_fn(ref[:, pl.ds(off, TK)]).astype(jnp.int16)
            part = hit[:, 0:128]
            for j in range(1, TK // 128):
                part = part + hit[:, j * 128:(j + 1) * 128]
            return cnt + part
        cnt = lax.fori_loop(0, nchunks, body, jnp.zeros((TQ, 128), jnp.int16))
        return jnp.sum(cnt.astype(jnp.int32), axis=1, keepdims=True)

    def top_half_word(ref, wanted):
        def bit_step(i, t):
            cand = t + lax.shift_left(jnp.int32(1), jnp.int32(15) - i)
            cand16 = cand.astype(jnp.int16)
            cnt = count16(ref, lambda k: k >= cand16)
            return jnp.where(cnt >= wanted, cand, t)
        return lax.fori_loop(0, 16, bit_step, jnp.full((TQ, 1), -32768, jnp.int32))

    t_hi = top_half_word(khi_scr, topk)
    t_hi16 = t_hi.astype(jnp.int16)
    above = count16(khi_scr, lambda k: k > t_hi16)

    def keep_bucket(c, carry):
        off = pl.multiple_of(c * TK, TK)
        cols = pl.ds(off, TK)
        klo_scr[:, cols] = jnp.where(khi_scr[:, cols] == t_hi16, klo_scr[:, cols], jnp.int16(-32768))
        return carry

    lax.fori_loop(0, nchunks, keep_bucket, 0)
    t_lo = top_half_word(klo_scr, topk - above)
    T = t_hi * 65536 + (t_lo + 32768)
    n_gt = count(lambda k: k > T)
    room = (topk - n_gt).astype(f32)

    m_scr, acc_scr = head_scr[:A_HEADS], head_scr[A_HEADS:]
    for h in range(A_HEADS):
        m_scr[h][...] = jnp.full((TQ, 1), -jnp.inf, f32)
        acc_scr[h][...] = jnp.zeros((TQ, 2 * A_LAT), f32)

    def attend_chunk(c, ties_before):
        off = pl.multiple_of(c * TK, TK)
        keyc = key_scr[:, pl.ds(off, TK)]
        eq = keyc == T
        eqb = jnp.where(eq, 1.0, 0.0).astype(bf16)
        ties = ties_before
        admit = []
        for j in range(TK // 128):
            blk = slice(j * 128, (j + 1) * 128)
            cnt = jnp.dot(eqb[:, blk], tri_ref[...], preferred_element_type=f32)
            admit.append(eq[:, blk] & (cnt[:, :128] + ties < room))
            ties = ties + cnt[:, 128:]
        sel = (keyc > T) | jnp.concatenate(admit, axis=1)
        sel = sel & (colb + off <= qpos)
        kc = kaug_scr[pl.ds(off, TK), :]
        scores = lax.dot_general(qaug_scr[...], kc, (((1,), (1,)), ((), ())), preferred_element_type=f32)
        probs, alphas = [], []
        for h in range(A_HEADS):
            s = jnp.where(sel, scores[h * TQ:(h + 1) * TQ], -jnp.inf)
            m_old = m_scr[h][...]
            m_new = jnp.maximum(m_old, jnp.max(s, axis=1, keepdims=True))
            m_safe = jnp.where(m_new == -jnp.inf, 0.0, m_new)
            probs.append(jnp.exp(s - m_safe).astype(bf16))
            alphas.append(jnp.exp(m_old - m_safe))
            m_scr[h][...] = m_new
        for h in range(A_HEADS):
            acc_scr[h][...] = alphas[h] * acc_scr[h][...] + jnp.dot(probs[h], kc, preferred_element_type=f32)
        return ties

    lax.fori_loop(0, nchunks, attend_chunk, jnp.zeros((TQ, 128), f32))

    heads = []
    for h in range(A_HEADS):
        acc = acc_scr[h][...]
        row_sum = acc[:, A_LAT + SUM_LANE:A_LAT + SUM_LANE + 1]
        heads.append((acc[:, :A_LAT] / row_sum).astype(bf16))
    o = jnp.concatenate(heads, axis=1)
    o_ref[0] = jnp.dot(o, wbd_ref[...], preferred_element_type=f32)


def dsa_attention(q_lat, kv, iq, ik, iw, q_gain, k_gain, w_uv, *, TQ=128, TK=1024):
    B, S, _ = q_lat.shape
    topk = min(IDX_TOPK_MAX, S // 4)
    TK = min(TK, S)
    wbd = jnp.zeros((A_HEADS * A_LAT, A_WIDTH), f32)
    for h in range(A_HEADS):
        wbd = wbd.at[h * A_LAT:(h + 1) * A_LAT, h * A_VDIM:(h + 1) * A_VDIM].set(w_uv[h])
    blk = jnp.arange(128)
    tri = jnp.concatenate([blk[:, None] < blk[None, :], jnp.ones((128, 128), bool)], axis=1).astype(bf16)
    kern = functools.partial(_dsa_kernel, S=S, TQ=TQ, TK=TK, topk=topk)
    return pl.pallas_call(
        kern,
        grid=(B, S // TQ),
        in_specs=[
            pl.BlockSpec((1, TQ, A_HEADS * A_LAT), lambda b, i: (b, i, 0)),
            pl.BlockSpec((1, S, A_LAT), lambda b, i: (b, 0, 0)),
            pl.BlockSpec((1, TQ, IDX_HEADS * IDX_DIM), lambda b, i: (b, i, 0)),
            pl.BlockSpec((1, S, IDX_DIM), lambda b, i: (b, 0, 0)),
            pl.BlockSpec((1, TQ, IDX_HEADS), lambda b, i: (b, i, 0)),
            pl.BlockSpec((1, A_LAT), lambda b, i: (0, 0)),
            pl.BlockSpec((1, A_LAT), lambda b, i: (0, 0)),
            pl.BlockSpec((A_HEADS * A_LAT, A_WIDTH), lambda b, i: (0, 0)),
            pl.BlockSpec((128, 256), lambda b, i: (0, 0)),
        ],
        out_specs=pl.BlockSpec((1, TQ, A_WIDTH), lambda b, i: (b, i, 0)),
        out_shape=jax.ShapeDtypeStruct((B, S, A_WIDTH), f32),
        scratch_shapes=[
            pltpu.VMEM((S, 2 * A_LAT), bf16),
            pltpu.VMEM((S, IDX_DIM), bf16),
            pltpu.VMEM((TQ, S), jnp.int32),
            pltpu.VMEM((TQ, S), jnp.int16),
            pltpu.VMEM((TQ, S), jnp.int16),
            pltpu.VMEM((A_HEADS * TQ, 2 * A_LAT), bf16),
            *[pltpu.VMEM((TQ, 1), f32) for _ in range(A_HEADS)],
            *[pltpu.VMEM((TQ, 2 * A_LAT), f32) for _ in range(A_HEADS)],
        ],
        compiler_params=pltpu.CompilerParams(dimension_semantics=("arbitrary", "arbitrary")),
        name="dsa_attention",
    )(q_lat, kv, iq, ik, iw, q_gain.reshape(1, -1), k_gain.reshape(1, -1), wbd.astype(bf16), tri)


CONV_HALO = 8


def _split_bf16(x, terms):
    parts = []
    for _ in range(terms):
        p = x.astype(bf16)
        parts.append(p)
        x = x - p.astype(f32)
    return parts


def _dot_exact_lhs(mask_bf16, x):
    return sum(jnp.dot(mask_bf16, p, preferred_element_type=f32) for p in _split_bf16(x, 3))


def _dot_exact_rhs(x, mask_bf16):
    return sum(jnp.dot(p, mask_bf16, preferred_element_type=f32) for p in _split_bf16(x, 3))


def _dot_3pass(a, b):
    ah, al = _split_bf16(a, 2)
    bh, bl = _split_bf16(b, 2)
    return (jnp.dot(ah, bh, preferred_element_type=f32) + jnp.dot(ah, bl, preferred_element_type=f32)
            + jnp.dot(al, bh, preferred_element_type=f32))


def _softplus(x):
    return jnp.maximum(x, 0.0) + jnp.log1p(jnp.exp(-jnp.abs(x)))


def _gdn_kernel(q_ref, k_ref, v_ref, z_ref, a_ref, b_ref, at_ref, bt_ref, cw_ref, alog_ref, dtb_ref,
                gain_ref, o_ref, tail_scr, act_scr, *state_scr, SB):
    sj = pl.program_id(1)

    @pl.when(sj == 0)
    def _reset():
        tail_scr[...] = jnp.zeros(tail_scr.shape, f32)
        for st_ref in state_scr:
            st_ref[...] = jnp.zeros((B_DIM, B_DIM), f32)

    cw = cw_ref[...]
    for idx, x_ref in enumerate((q_ref, k_ref, v_ref)):
        x = x_ref[0]
        xc = jnp.concatenate([tail_scr[idx], x], axis=0)
        y = jnp.zeros((SB, B_WIDTH), f32)
        for j in range(CONV_W):
            lo = CONV_HALO - (CONV_W - 1) + j
            y = y + cw[j:j + 1, idx * B_WIDTH:(idx + 1) * B_WIDTH] * xc[lo:lo + SB]
        tail_scr[idx] = x[SB - CONV_HALO:SB]
        act_scr[idx] = y * jax.nn.sigmoid(y)

    ri = lax.broadcasted_iota(jnp.int32, (CHUNK, CHUNK), 0)
    ci = lax.broadcasted_iota(jnp.int32, (CHUNK, CHUNK), 1)
    tril = ri >= ci
    strict = ri > ci
    tril_b = jnp.where(tril, 1.0, 0.0).astype(bf16)
    triu_b = jnp.where(ri <= ci, 1.0, 0.0).astype(bf16)
    eye = jnp.where(ri == ci, 1.0, 0.0)
    nt = (((1,), (1,)), ((), ()))

    def chunk(c, carry):
        rows = pl.ds(pl.multiple_of(c * CHUNK, CHUNK), CHUNK)
        H = range(B_HEADS)
        lanes = [slice(h * B_DIM, (h + 1) * B_DIM) for h in H]
        lp = dict(preferred_element_type=f32)
        q = [act_scr[0, rows, lanes[h]] for h in H]
        k = [act_scr[1, rows, lanes[h]] for h in H]
        v = [act_scr[2, rows, lanes[h]] for h in H]
        q = [x * lax.rsqrt(jnp.sum(x * x, -1, keepdims=True) + EPS) * (B_DIM ** -0.5) for x in q]
        k = [x * lax.rsqrt(jnp.sum(x * x, -1, keepdims=True) + EPS) for x in k]
        neg_rate = [-jnp.exp(alog_ref[0:1, h:h + 1]) for h in H]
        dtb = [dtb_ref[0:1, h:h + 1] for h in H]
        g_col = [neg_rate[h] * _softplus(a_ref[0, rows, h:h + 1] + dtb[h]) for h in H]
        g_row = [neg_rate[h] * _softplus(at_ref[0, h, pl.ds(c, 1), :] + dtb[h]) for h in H]
        beta = [jax.nn.sigmoid(b_ref[0, rows, h:h + 1]) for h in H]
        Gc = [_dot_exact_lhs(tril_b, jnp.broadcast_to(g_col[h], (CHUNK, CHUNK))) for h in H]
        Gr = [_dot_exact_rhs(jnp.broadcast_to(g_row[h], (CHUNK, CHUNK)), triu_b) for h in H]
        G = [x[:, 0:1] for x in Gc]
        G_last = [x[CHUNK - 1:CHUNK, 0:1] for x in Gc]
        decay = [jnp.exp(jnp.where(tril, Gc[h] - Gr[h], -jnp.inf)) for h in H]
        kb = [k[h] * beta[h] for h in H]
        kbf = [x.astype(bf16) for x in k]
        L = [jnp.where(strict, lax.dot_general(kb[h].astype(bf16), kbf[h], nt, **lp) * decay[h], 0.0) for h in H]
        P = [eye - x for x in L]
        M = [_dot_3pass(x, x) for x in L]
        for lvl in range(5):
            P = [P[h] + _dot_3pass(P[h], M[h]) for h in H]
            if lvl < 4:
                M = [_dot_3pass(x, x) for x in M]
        Tb = [x.astype(bf16) for x in P]
        eG = [jnp.exp(x) for x in G]
        u = [jnp.dot(Tb[h], (v[h] * beta[h]).astype(bf16), **lp) for h in H]
        w = [jnp.dot(Tb[h], (kb[h] * eG[h]).astype(bf16), **lp) for h in H]
        attn = [lax.dot_general(q[h].astype(bf16), kbf[h], nt, **lp) * decay[h] for h in H]
        q_dec = [q[h] * eG[h] for h in H]
        k_dec = [k[h] * jnp.exp(G_last[h] - G[h]) for h in H]
        st = [state_scr[h][...] for h in H]
        stb = [x.astype(bf16) for x in st]
        v_new = [u[h] - jnp.dot(w[h].astype(bf16), stb[h], **lp) for h in H]
        vnb = [x.astype(bf16) for x in v_new]
        o = [jnp.dot(q_dec[h].astype(bf16), stb[h], **lp) + jnp.dot(attn[h].astype(bf16), vnb[h], **lp) for h in H]
        for h in H:
            state_scr[h][...] = st[h] * jnp.exp(G_last[h]) + lax.dot_general(
                k_dec[h].astype(bf16), vnb[h], (((0,), (0,)), ((), ())), **lp)
        for h in H:
            zh = z_ref[0, rows, lanes[h]]
            on = o[h] * lax.rsqrt(jnp.mean(o[h] * o[h], -1, keepdims=True) + EPS) * gain_ref[...]
            o_ref[0, rows, lanes[h]] = on * (zh * jax.nn.sigmoid(zh))
        return carry

    lax.fori_loop(0, SB // CHUNK, chunk, 0)


def gated_deltanet(qkvz, a, b, conv_w, a_log, dt_bias, norm_gain, *, SB=512):
    B, S, _ = qkvz.shape
    SB = min(SB, S)
    nch = S // CHUNK
    at = a.transpose(0, 2, 1).reshape(B, B_HEADS, nch, CHUNK)
    bt = b.transpose(0, 2, 1).reshape(B, B_HEADS, nch, CHUNK)
    wide = pl.BlockSpec((1, SB, B_WIDTH), lambda bi, j: (bi, j, 0))
    part = [pl.BlockSpec((1, SB, B_WIDTH), functools.partial(lambda bi, j, n: (bi, j, n), n=n))
            for n in range(4)]
    narrow = pl.BlockSpec((1, SB, B_HEADS), lambda bi, j: (bi, j, 0))
    rowwise = pl.BlockSpec((1, B_HEADS, SB // CHUNK, CHUNK), lambda bi, j: (bi, 0, j, 0))
    return pl.pallas_call(
        functools.partial(_gdn_kernel, SB=SB),
        grid=(B, S // SB),
        in_specs=[*part, narrow, narrow, rowwise, rowwise,
                  pl.BlockSpec((CONV_W, 3 * B_WIDTH), lambda bi, j: (0, 0)),
                  pl.BlockSpec((1, B_HEADS), lambda bi, j: (0, 0)),
                  pl.BlockSpec((1, B_HEADS), lambda bi, j: (0, 0)),
                  pl.BlockSpec((1, B_DIM), lambda bi, j: (0, 0))],
        out_specs=wide,
        out_shape=jax.ShapeDtypeStruct((B, S, B_WIDTH), f32),
        scratch_shapes=[pltpu.VMEM((3, CONV_HALO, B_WIDTH), f32),
                        pltpu.VMEM((3, SB, B_WIDTH), f32),
                        *[pltpu.VMEM((B_DIM, B_DIM), f32) for _ in range(B_HEADS)]],
        compiler_params=pltpu.CompilerParams(dimension_semantics=("arbitrary", "arbitrary")),
        name="gated_deltanet",
    )(qkvz, qkvz, qkvz, qkvz, a, b, at, bt, conv_w,
      a_log.reshape(1, -1), dt_bias.reshape(1, -1), norm_gain.reshape(1, -1))


PEER_SLOTS = PEER_HEADS * PEER_TOPK
PEER_HALF = PEER_QDIM // 2
ROW_WORDS = D_MODEL // 2
ROW_SUBL = ROW_WORDS // 128
STAGE_STRIDE = PEER_SLOTS + 8


def _top16_rows(s, iota_rows, fill, vals_scr, idx_scr, payload=None, pay_scr=None):
    for r in range(PEER_TOPK):
        m = jnp.max(s, axis=0, keepdims=True)
        first = jnp.min(jnp.where(s == m, iota_rows, fill), axis=0, keepdims=True)
        taken = iota_rows == first
        vals_scr[r:r + 1, :] = m
        if payload is None:
            idx_scr[r:r + 1, :] = first.astype(jnp.int32)
        else:
            pay_scr[r:r + 1, :] = jnp.max(jnp.where(taken, payload, -1.0), axis=0,
                                          keepdims=True).astype(jnp.int32)
        s = jnp.where(taken, -jnp.inf, s)


def _peer_route_kernel(h_ref, wq_ref, sk_ref, off_ref, gate_ref,
                       v1_scr, i1_scr, v2_scr, i2_scr, cv_scr, ce_scr, eid_scr, gate_scr, *, T):
    q = jnp.dot(h_ref[...].astype(bf16), wq_ref[...], preferred_element_type=f32).astype(bf16)
    gate_scr[...] = jnp.zeros(gate_scr.shape, f32)
    kiota = lax.broadcasted_iota(jnp.int32, (PEER_NKEYS, T), 0).astype(f32)
    sub8 = lax.broadcasted_iota(jnp.int32, (8, T), 0).astype(f32)
    for p in range(PEER_HEADS):
        for half, (vs, is_) in enumerate(((v1_scr, i1_scr), (v2_scr, i2_scr))):
            g = 2 * p + half
            s = lax.dot_general(sk_ref[g], q[:, g * PEER_HALF:(g + 1) * PEER_HALF],
                                (((1,), (1,)), ((), ())), preferred_element_type=f32)
            _top16_rows(s, kiota, PEER_NKEYS, vs, is_)
        v1 = v1_scr[...]
        v2 = v2_scr[...]
        e1 = (i1_scr[...] * PEER_NKEYS).astype(f32)
        e2 = i2_scr[...].astype(f32)
        cv = [v1[0:1] + v2[0:8], v1[0:1] + v2[8:16]]
        ce = [e1[0:1] + e2[0:8], e1[0:1] + e2[8:16]]
        cf = [sub8, sub8 + 8]
        for a in range(1, 8):
            cv.append(v1[a:a + 1] + v2[0:8])
            ce.append(e1[a:a + 1] + e2[0:8])
            cf.append(sub8 + a * PEER_TOPK)
        cv.append(v1[8:16] + v2[0:1])
        ce.append(e1[8:16] + e2[0:1])
        cf.append((sub8 + 8) * PEER_TOPK)
        cand = jnp.concatenate(cv, axis=0)
        cexp = jnp.concatenate(ce, axis=0)
        cflat = jnp.concatenate(cf, axis=0)
        _top16_rows(cand, cflat, PEER_TOPK * PEER_TOPK, cv_scr, None, payload=cexp, pay_scr=ce_scr)
        top = cv_scr[...]
        ex = jnp.exp(top - top[0:1])
        gate_scr[pl.ds(2 * p * PEER_TOPK + 1, PEER_TOPK, stride=2), :] = ex / jnp.sum(ex, axis=0, keepdims=True)
        eid_scr[p * PEER_TOPK:(p + 1) * PEER_TOPK, :] = ce_scr[...] * ROW_SUBL
    off_ref[...] = eid_scr[...]
    gate_ref[...] = gate_scr[...].T


def peer_route(h, w_query, sub_keys, *, T=128):
    N, D = h.shape
    sk = sub_keys.reshape(PEER_HEADS * 2, PEER_NKEYS, PEER_HALF).astype(bf16)
    kern = functools.partial(_peer_route_kernel, T=T)
    return pl.pallas_call(
        kern,
        grid=(N // T,),
        in_specs=[
            pl.BlockSpec((T, D), lambda i: (i, 0)),
            pl.BlockSpec((D, PEER_HEADS * PEER_QDIM), lambda i: (0, 0)),
            pl.BlockSpec((PEER_HEADS * 2, PEER_NKEYS, PEER_HALF), lambda i: (0, 0, 0)),
        ],
        out_specs=[pl.BlockSpec((PEER_SLOTS, T), lambda i: (0, i)),
                   pl.BlockSpec((T, 2 * PEER_SLOTS), lambda i: (i, 0))],
        out_shape=[jax.ShapeDtypeStruct((PEER_SLOTS, N), jnp.int32),
                   jax.ShapeDtypeStruct((N, 2 * PEER_SLOTS), f32)],
        scratch_shapes=[
            pltpu.VMEM((PEER_TOPK, T), f32), pltpu.VMEM((PEER_TOPK, T), jnp.int32),
            pltpu.VMEM((PEER_TOPK, T), f32), pltpu.VMEM((PEER_TOPK, T), jnp.int32),
            pltpu.VMEM((PEER_TOPK, T), f32), pltpu.VMEM((PEER_TOPK, T), jnp.int32),
            pltpu.VMEM((PEER_SLOTS, T), jnp.int32), pltpu.VMEM((2 * PEER_SLOTS, T), f32),
        ],
        compiler_params=pltpu.CompilerParams(dimension_semantics=("arbitrary",)),
        name="peer_route",
    )(h, w_query.astype(bf16), sk)


def pack_table(tab):
    bits = lax.bitcast_convert_type(tab.astype(bf16), jnp.uint16).astype(jnp.uint32)
    words = (bits[:, :ROW_WORDS] << 16) | bits[:, ROW_WORDS:]
    return lax.bitcast_convert_type(words, jnp.int32).reshape(tab.shape[0] * ROW_SUBL, 128)


GROUP = 8


def _fetch_slots(off_ref, t0, tab_ref, stage, first, last):
    for k in range(first, last):
        offs = off_ref.at[k, pl.ds(t0, GROUP)]
        for i in range(GROUP):
            off = pl.multiple_of(offs[i], ROW_SUBL)
            stage[i, pl.ds(k, ROW_SUBL, stride=STAGE_STRIDE), :] = tab_ref[pl.ds(off, ROW_SUBL), :]


def _staged_rows(stage, i):
    chunks = [stage[i, c * STAGE_STRIDE:c * STAGE_STRIDE + PEER_SLOTS, :] for c in range(ROW_SUBL)]
    return pltpu.bitcast(jnp.concatenate(chunks, axis=1), bf16)


def _pipelined_groups(off_ref, tab_ref, stage_a, stage_b, start, step, finish, T):
    ngroups = T // GROUP
    per_step = PEER_SLOTS // GROUP
    _fetch_slots(off_ref, 0, tab_ref, stage_a, 0, PEER_SLOTS)

    def run(g, stage, g_next, stage_next):
        t_next = pl.multiple_of(g_next * GROUP, GROUP)
        carry = start(g)
        for i in range(GROUP):
            carry = step(i, carry, stage)
            _fetch_slots(off_ref, t_next, tab_ref, stage_next, i * per_step, (i + 1) * per_step)
        finish(g, carry)

    def pair(j, carry):
        g0 = 2 * j
        run(g0, stage_a, g0 + 1, stage_b)
        run(g0 + 1, stage_b, jnp.minimum(g0 + 2, ngroups - 1), stage_a)
        return carry

    lax.fori_loop(0, ngroups // 2, pair, 0)


def _peer_u_kernel(off_ref, h_ref, gate_ref, tab_ref, w_ref, stage_a, stage_b, *, T):
    row16 = lax.broadcasted_iota(jnp.int32, (2 * GROUP, 2 * PEER_SLOTS), 0) & (GROUP - 1)
    odd = (lax.broadcasted_iota(jnp.int32, (GROUP, 2 * PEER_SLOTS), 1) & 1) == 1

    def start(g):
        h8 = h_ref[pl.ds(pl.multiple_of(g * GROUP, GROUP), GROUP), :]
        h16 = jnp.concatenate([h8[:, :ROW_WORDS], h8[:, ROW_WORDS:]], axis=0).astype(bf16)
        return h16, jnp.zeros((2 * GROUP, 2 * PEER_SLOTS), f32)

    def step(i, carry, stage):
        h16, acc = carry
        res = lax.dot_general(h16, _staged_rows(stage, i), (((1,), (1,)), ((), ())),
                              preferred_element_type=f32)
        return h16, jnp.where(row16 == i, res, acc)

    def finish(g, carry):
        t0 = pl.multiple_of(g * GROUP, GROUP)
        acc = carry[1]
        dots = acc[0:GROUP] + pltpu.roll(acc[GROUP:], 1, axis=1)
        gelu = 0.5 * dots * (1.0 + lax.erf(dots * (2.0 ** -0.5)))
        w_ref[pl.ds(t0, GROUP), :] = jnp.where(odd, gelu * gate_ref[pl.ds(t0, GROUP), :], 0.0)

    _pipelined_groups(off_ref, tab_ref, stage_a, stage_b, start, step, finish, T)


def _peer_v_kernel(off_ref, w_ref, x_ref, g2_ref, tab_ref, o_ref, stage_a, stage_b, *, T):
    row16 = lax.broadcasted_iota(jnp.int32, (2 * GROUP, ROW_WORDS), 0) & (GROUP - 1)

    def start(g):
        w8 = w_ref[pl.ds(pl.multiple_of(g * GROUP, GROUP), GROUP), :]
        w16 = jnp.concatenate([w8, pltpu.roll(w8, 2 * PEER_SLOTS - 1, axis=1)], axis=0).astype(bf16)
        return w16, jnp.zeros((2 * GROUP, ROW_WORDS), f32)

    def step(i, carry, stage):
        w16, acc = carry
        res = jnp.dot(w16, _staged_rows(stage, i), preferred_element_type=f32)
        return w16, jnp.where(row16 == i, res, acc)

    def finish(g, carry):
        t0 = pl.multiple_of(g * GROUP, GROUP)
        acc = carry[1]
        y = jnp.concatenate([acc[0:GROUP], acc[GROUP:]], axis=1)
        o_ref[pl.ds(t0, GROUP), :] = x_ref[pl.ds(t0, GROUP), :] + g2_ref[0] * y

    _pipelined_groups(off_ref, tab_ref, stage_a, stage_b, start, step, finish, T)


_TABLE_VMEM_LIMIT = 56 * 1024 * 1024


def _table_spec():
    return pl.BlockSpec((N_EXPERTS * ROW_SUBL, 128), lambda i: (0, 0), pipeline_mode=pl.Buffered(1))


def _stage_scratch():
    return [pltpu.VMEM((GROUP, ROW_SUBL * STAGE_STRIDE, 128), jnp.int32) for _ in range(2)]


def _offset_spec(T):
    return pl.BlockSpec((PEER_SLOTS, T), lambda i: (0, i), memory_space=pltpu.SMEM)


def peer_u(off, h, gate, u_pk, *, T=256):
    N, D = h.shape
    return pl.pallas_call(
        functools.partial(_peer_u_kernel, T=T),
        grid=(N // T,),
        in_specs=[
            _offset_spec(T),
            pl.BlockSpec((T, D), lambda i: (i, 0)),
            pl.BlockSpec((T, 2 * PEER_SLOTS), lambda i: (i, 0)),
            _table_spec(),
        ],
        out_specs=pl.BlockSpec((T, 2 * PEER_SLOTS), lambda i: (i, 0)),
        out_shape=jax.ShapeDtypeStruct((N, 2 * PEER_SLOTS), f32),
        scratch_shapes=_stage_scratch(),
        compiler_params=pltpu.CompilerParams(dimension_semantics=("arbitrary",),
                                             vmem_limit_bytes=_TABLE_VMEM_LIMIT),
        name="peer_u",
    )(off, h, gate, u_pk)


def peer_v(off, w, x, g2, v_pk, *, T=256):
    N, D = x.shape
    per_batch = N // g2.shape[0]
    return pl.pallas_call(
        functools.partial(_peer_v_kernel, T=T),
        grid=(N // T,),
        in_specs=[
            _offset_spec(T),
            pl.BlockSpec((T, 2 * PEER_SLOTS), lambda i: (i, 0)),
            pl.BlockSpec((T, D), lambda i: (i, 0)),
            pl.BlockSpec((1, 1, D), lambda i: (i * T // per_batch, 0, 0)),
            _table_spec(),
        ],
        out_specs=pl.BlockSpec((T, D), lambda i: (i, 0)),
        out_shape=jax.ShapeDtypeStruct((N, D), f32),
        scratch_shapes=_stage_scratch(),
        compiler_params=pltpu.CompilerParams(dimension_semantics=("arbitrary",),
                                             vmem_limit_bytes=_TABLE_VMEM_LIMIT),
        name="peer_v",
    )(off, w, x, g2.reshape(g2.shape[0], 1, D), v_pk)


def kernel(x, c, w_ada, b_ada, w_in, q_gain, k_gain, w_uv, conv_w, a_log, dt_bias,
           gdn_gain, w_out, w_query, sub_keys, u_tab, v_tab):
    B, S, D = x.shape
    xf = x.reshape(B * S, D)
    for l in range(DEPTH):
        sh1, sc1, g1, sh2, sc2, g2 = jnp.split(adaln(c, w_ada[l], b_ada[l]), 6, axis=-1)
        aq, akv, iq, small, qkvz = in_projection(xf, sh1, sc1, w_in[l])
        small = small.reshape(B, S, -1)
        ik, iw, ba, bb = (small[..., lo:hi] for lo, hi in (SMALL_IK, SMALL_IW, SMALL_BA, SMALL_BB))
        ya = dsa_attention(aq.reshape(B, S, -1), akv.reshape(B, S, -1), iq.reshape(B, S, -1), ik, iw,
                           q_gain[l], k_gain[l], w_uv[l])
        yb = gated_deltanet(qkvz.reshape(B, S, -1), ba, bb, conv_w[l], a_log[l], dt_bias[l], gdn_gain[l])
        x1, h2 = out_projection(ya.reshape(B * S, -1), yb.reshape(B * S, -1), xf, g1, sh2, sc2, w_out[l])
        off, gate = peer_route(h2, w_query[l], sub_keys[l])
        w = peer_u(off, h2, gate, pack_table(u_tab[l]))
        xf = peer_v(off, w, x1, g2, pack_table(v_tab[l]))
    return xf.reshape(B, S, D)
```

```python
import functools
import math
import jax
import jax.numpy as jnp
from jax import lax
import numpy as np
from jax.experimental import pallas as pl
from jax.experimental.pallas import tpu as pltpu

D_MODEL = 1024
BATCH = 8
SEQ = 4096
DEPTH = 1

EPS = 1e-6
A_HEADS = 8
A_LAT = 128
A_VDIM = 64
IDX_HEADS = 4
IDX_DIM = 64
IDX_TOPK_MAX = 256
Q_BLOCK = 128
B_HEADS = 4
B_DIM = 128
CONV_W = 4
CHUNK = 64
PEER_HEADS = 8
PEER_NKEYS = 128
PEER_QDIM = 256
PEER_TOPK = 16
PEER_BLOCK = 128
N_EXPERTS = PEER_NKEYS * PEER_NKEYS

A_WIDTH = A_HEADS * A_VDIM
B_WIDTH = B_HEADS * B_DIM
MIX_WIDTH = A_WIDTH + B_WIDTH
IN_SPLITS = (A_HEADS * A_LAT, A_LAT, IDX_HEADS * IDX_DIM, IDX_DIM, IDX_HEADS,
             B_WIDTH, B_WIDTH, B_WIDTH, B_WIDTH, B_HEADS, B_HEADS)
IN_WIDTH = sum(IN_SPLITS)


f32 = jnp.float32
bf16 = jnp.bfloat16
INT_MIN = -2 ** 31
SUM_LANE = 2
_PROJ_VMEM_LIMIT = 48 * 1024 * 1024


def _rms_modulate(x, shift, scale):
    xn = x * lax.rsqrt(jnp.mean(x * x, -1, keepdims=True) + EPS)
    return xn * (1.0 + scale) + shift


def _adaln_kernel(c_ref, w_ref, b_ref, o_ref):
    c = c_ref[...]
    s = (c * jax.nn.sigmoid(c)).astype(bf16)
    o_ref[...] = jnp.dot(s, w_ref[...].astype(bf16), preferred_element_type=f32) + b_ref[...]


def adaln(c, w, b, *, tn=512):
    B, D = c.shape
    N = w.shape[1]
    return pl.pallas_call(
        _adaln_kernel,
        grid=(N // tn,),
        in_specs=[pl.BlockSpec((B, D), lambda j: (0, 0)),
                  pl.BlockSpec((D, tn), lambda j: (0, j)),
                  pl.BlockSpec((1, tn), lambda j: (0, j))],
        out_specs=pl.BlockSpec((B, tn), lambda j: (0, j)),
        out_shape=jax.ShapeDtypeStruct((B, N), f32),
        name="adaln",
    )(c, w, b.reshape(1, N))


SMALL_IK = (0, IDX_DIM)
SMALL_IW = (IDX_DIM, IDX_DIM + IDX_HEADS)
SMALL_BA = (IDX_DIM + IDX_HEADS, IDX_DIM + IDX_HEADS + B_HEADS)
SMALL_BB = (IDX_DIM + IDX_HEADS + B_HEADS, IDX_DIM + IDX_HEADS + 2 * B_HEADS)


def _inproj_kernel(x_ref, sh_ref, sc_ref, wq_ref, wkv_ref, wiq_ref, wsm_ref, wb_ref,
                   oq_ref, okv_ref, oiq_ref, osm_ref, ob_ref):
    h = _rms_modulate(x_ref[...], sh_ref[0], sc_ref[0]).astype(bf16)
    for w_ref, o_ref in ((wq_ref, oq_ref), (wkv_ref, okv_ref), (wiq_ref, oiq_ref),
                         (wsm_ref, osm_ref), (wb_ref, ob_ref)):
        o_ref[...] = jnp.dot(h, w_ref[...], preferred_element_type=f32)


def in_projection(x, shift, scale, w_in, *, tm=512):
    N, D = x.shape
    per_batch = N // shift.shape[0]
    o = [int(v) for v in np.cumsum((0,) + IN_SPLITS)]
    wb = w_in.astype(bf16)
    w_q, w_kv, w_iq = wb[:, o[0]:o[1]], wb[:, o[1]:o[2]], wb[:, o[2]:o[3]]
    w_small = jnp.concatenate([wb[:, o[3]:o[5]], wb[:, o[9]:o[11]],
                               jnp.zeros((D, 128 - SMALL_BB[1]), bf16)], axis=1)
    w_b = wb[:, o[5]:o[9]]
    weights = (w_q, w_kv, w_iq, w_small, w_b)
    mod = pl.BlockSpec((1, 1, D), lambda i: (i * tm // per_batch, 0, 0))
    return pl.pallas_call(
        _inproj_kernel,
        grid=(N // tm,),
        in_specs=[pl.BlockSpec((tm, D), lambda i: (i, 0)), mod, mod]
                 + [pl.BlockSpec(w.shape, lambda i: (0, 0), pipeline_mode=pl.Buffered(1)) for w in weights],
        out_specs=[pl.BlockSpec((tm, w.shape[1]), lambda i: (i, 0)) for w in weights],
        out_shape=[jax.ShapeDtypeStruct((N, w.shape[1]), f32) for w in weights],
        compiler_params=pltpu.CompilerParams(dimension_semantics=("arbitrary",),
                                             vmem_limit_bytes=_PROJ_VMEM_LIMIT),
        name="in_projection",
    )(x, shift[:, None, :], scale[:, None, :], *weights)


def _outproj_kernel(ya_ref, yb_ref, x_ref, g1_ref, sh_ref, sc_ref, wa_ref, wb_ref, x1_ref, h2_ref):
    mix = (jnp.dot(ya_ref[...].astype(bf16), wa_ref[...], preferred_element_type=f32)
           + jnp.dot(yb_ref[...].astype(bf16), wb_ref[...], preferred_element_type=f32))
    x1 = x_ref[...] + g1_ref[0] * mix
    x1_ref[...] = x1
    h2_ref[...] = _rms_modulate(x1, sh_ref[0], sc_ref[0])


def out_projection(ya, yb, x, gate, shift, scale, w_out, *, tm=512):
    N, D = x.shape
    per_batch = N // gate.shape[0]
    wb = w_out.astype(bf16)
    w_a, w_b = wb[:A_WIDTH], wb[A_WIDTH:]
    mod = pl.BlockSpec((1, 1, D), lambda i: (i * tm // per_batch, 0, 0))
    row = pl.BlockSpec((tm, D), lambda i: (i, 0))
    half = pl.BlockSpec((tm, A_WIDTH), lambda i: (i, 0))
    return pl.pallas_call(
        _outproj_kernel,
        grid=(N // tm,),
        in_specs=[half, half, row, mod, mod, mod,
                  pl.BlockSpec(w_a.shape, lambda i: (0, 0)), pl.BlockSpec(w_b.shape, lambda i: (0, 0))],
        out_specs=[row, row],
        out_shape=[jax.ShapeDtypeStruct((N, D), f32), jax.ShapeDtypeStruct((N, D), f32)],
        compiler_params=pltpu.CompilerParams(dimension_semantics=("arbitrary",),
                                             vmem_limit_bytes=_PROJ_VMEM_LIMIT),
        name="out_projection",
    )(ya, yb, x, gate[:, None, :], shift[:, None, :], scale[:, None, :], w_a, w_b)


def _dsa_kernel(q_ref, kv_ref, iq_ref, ik_ref, iwt_ref, qg_ref, kg_ref, wbd_ref, tri_ref, o_ref,
                kaug_scr, ikb_scr, key_scr, khi_scr, klo_scr, qaug_scr, *head_scr, S, TQ, TK, topk):
    qi = pl.program_id(1)
    scale = A_LAT ** -0.5
    idx_scale = (IDX_DIM ** -0.5) * (IDX_HEADS ** -0.5)

    @pl.when(qi == 0)
    def _prep_keys():
        kv = kv_ref[0]
        kn = kv * lax.rsqrt(jnp.mean(kv * kv, -1, keepdims=True) + EPS) * kg_ref[...]
        pos = lax.broadcasted_iota(jnp.int32, (S, A_LAT), 0)
        lane = lax.broadcasted_iota(jnp.int32, (S, A_LAT), 1)
        hi = (pos >> 6).astype(f32)
        lo = (pos & 63).astype(f32)
        extra = jnp.where(lane == 0, hi, jnp.where(lane == 1, lo, jnp.where(lane == SUM_LANE, 1.0, 0.0)))
        kaug_scr[:, :A_LAT] = kn.astype(bf16)
        kaug_scr[:, A_LAT:] = extra.astype(bf16)
        ikb_scr[...] = ik_ref[0].astype(bf16)

    q = q_ref[0]
    lane = lax.broadcasted_iota(jnp.int32, (TQ, A_LAT), 1)
    for h in range(A_HEADS):
        qh = q[:, h * A_LAT:(h + 1) * A_LAT]
        qn = qh * lax.rsqrt(jnp.mean(qh * qh, -1, keepdims=True) + EPS) * qg_ref[...] * scale
        slope = 2.0 ** (-8.0 * (h + 1) / A_HEADS)
        extra = jnp.where(lane == 0, slope * 64.0, jnp.where(lane == 1, slope, 0.0))
        qaug_scr[h * TQ:(h + 1) * TQ, :A_LAT] = qn.astype(bf16)
        qaug_scr[h * TQ:(h + 1) * TQ, A_LAT:] = extra.astype(bf16)

    nchunks = (qi * TQ + TQ + TK - 1) // TK
    qpos = qi * TQ + lax.broadcasted_iota(jnp.int32, (TQ, 1), 0)
    qpos_row = qi * TQ + lax.broadcasted_iota(jnp.int32, (1, TQ), 1)
    colb = lax.broadcasted_iota(jnp.int32, (TQ, TK), 1)
    rowb = lax.broadcasted_iota(jnp.int32, (TK, TQ), 0)
    iq = iq_ref[0]
    iqb = [iq[:, h * IDX_DIM:(h + 1) * IDX_DIM].astype(bf16) for h in range(IDX_HEADS)]
    iwt = iwt_ref[0]

    def score_chunk(c, carry):
        off = pl.multiple_of(c * TK, TK)
        ikc = ikb_scr[pl.ds(off, TK), :]
        acc = jnp.zeros((TK, TQ), f32)
        for h in range(IDX_HEADS):
            lg = lax.dot_general(ikc, iqb[h], (((1,), (1,)), ((), ())), preferred_element_type=f32)
            acc = acc + jnp.maximum(lg, 0.0) * iwt[h:h + 1, :]
        sc = acc * idx_scale
        sc = jnp.where(rowb + off <= qpos_row, sc, -jnp.inf)
        bits = pltpu.bitcast(sc, jnp.int32)
        key = jnp.where(bits < 0, bits ^ jnp.int32(0x7FFFFFFF), bits)
        key_scr[:, pl.ds(off, TK)] = key.T
        khi_scr[pl.ds(off, TK), :] = (key >> 16).astype(jnp.int16)
        klo_scr[pl.ds(off, TK), :] = ((key & 0xFFFF) - 32768).astype(jnp.int16)
        return carry

    lax.fori_loop(0, nchunks, score_chunk, 0)

    def tile16(row):
        return jnp.broadcast_to(row, (16, TQ)).astype(jnp.int16)

    def count16(ref, cmp, row):
        tile = tile16(row)

        def body(c, cnt):
            off = pl.multiple_of(c * TK, TK)
            x = ref[pl.ds(off, TK), :]
            hits = [cmp(x[j * 16:(j + 1) * 16], tile).astype(jnp.int16) for j in range(TK // 16)]
            while len(hits) > 1:
                hits = [a + b for a, b in zip(hits[::2], hits[1::2])]
            return cnt + hits[0]
        cnt = lax.fori_loop(0, nchunks, body, jnp.zeros((16, TQ), jnp.int16))
        return jnp.sum(cnt.astype(jnp.int32), axis=0, keepdims=True)

    def top_half_word(ref, wanted):
        def bit_step(i, t):
            cand = t + lax.shift_left(jnp.int32(1), jnp.int32(15) - i)
            cnt = count16(ref, lambda k, c: k >= c, cand)
            return jnp.where(cnt >= wanted, cand, t)
        return lax.fori_loop(0, 16, bit_step, jnp.full((1, TQ), -32768, jnp.int32))

    t_hi = top_half_word(khi_scr, topk)
    above = count16(khi_scr, lambda k, c: k > c, t_hi)
    t_hi_tile = tile16(t_hi)

    def keep_bucket(c, carry):
        rows = pl.ds(pl.multiple_of(c * TK, TK), TK)
        hi = khi_scr[rows, :]
        lo = klo_scr[rows, :]
        kept = [jnp.where(hi[j * 16:(j + 1) * 16] == t_hi_tile, lo[j * 16:(j + 1) * 16], jnp.int16(-32768))
                for j in range(TK // 16)]
        klo_scr[rows, :] = jnp.concatenate(kept, axis=0)
        return carry

    lax.fori_loop(0, nchunks, keep_bucket, 0)
    t_lo = top_half_word(klo_scr, topk - above)
    n_gt = above + count16(klo_scr, lambda k, c: k > c, t_lo)
    T = jnp.broadcast_to(t_hi * 65536 + (t_lo + 32768), (TQ, TQ)).T[:, 0:1]
    room = jnp.broadcast_to((topk - n_gt).astype(f32), (TQ, TQ)).T[:, 0:1]

    m_scr, acc_scr = head_scr[:A_HEADS], head_scr[A_HEADS:]
    for h in range(A_HEADS):
        m_scr[h][...] = jnp.full((TQ, 1), -jnp.inf, f32)
        acc_scr[h][...] = jnp.zeros((TQ, 2 * A_LAT), f32)

    def attend_chunk(c, ties_before):
        off = pl.multiple_of(c * TK, TK)
        keyc = key_scr[:, pl.ds(off, TK)]
        eq = keyc == T
        eqb = jnp.where(eq, 1.0, 0.0).astype(bf16)
        ties = ties_before
        admit = []
        for j in range(TK // 128):
            blk = slice(j * 128, (j + 1) * 128)
            cnt = jnp.dot(eqb[:, blk], tri_ref[...], preferred_element_type=f32)
            admit.append(eq[:, blk] & (cnt[:, :128] + ties < room))
            ties = ties + cnt[:, 128:]
        sel = (keyc > T) | jnp.concatenate(admit, axis=1)
        sel = sel & (colb + off <= qpos)
        kc = kaug_scr[pl.ds(off, TK), :]
        scores = lax.dot_general(qaug_scr[...], kc, (((1,), (1,)), ((), ())), preferred_element_type=f32)
        probs, alphas = [], []
        for h in range(A_HEADS):
            s = jnp.where(sel, scores[h * TQ:(h + 1) * TQ], -jnp.inf)
            m_old = m_scr[h][...]
            m_new = jnp.maximum(m_old, jnp.max(s, axis=1, keepdims=True))
            m_safe = jnp.where(m_new == -jnp.inf, 0.0, m_new)
            probs.append(jnp.exp(s - m_safe).astype(bf16))
            alphas.append(jnp.exp(m_old - m_safe))
            m_scr[h][...] = m_new
        for h in range(A_HEADS):
            acc_scr[h][...] = alphas[h] * acc_scr[h][...] + jnp.dot(probs[h], kc, preferred_element_type=f32)
        return ties

    lax.fori_loop(0, nchunks, attend_chunk, jnp.zeros((TQ, 128), f32))

    heads = []
    for h in range(A_HEADS):
        acc = acc_scr[h][...]
        row_sum = acc[:, A_LAT + SUM_LANE:A_LAT + SUM_LANE + 1]
        heads.append((acc[:, :A_LAT] / row_sum).astype(bf16))
    o = jnp.concatenate(heads, axis=1)
    o_ref[0] = jnp.dot(o, wbd_ref[...], preferred_element_type=f32)


def dsa_attention(q_lat, kv, iq, ik, iw, q_gain, k_gain, w_uv, *, TQ=128, TK=1024):
    B, S, _ = q_lat.shape
    topk = min(IDX_TOPK_MAX, S // 4)
    TK = min(TK, S)
    wbd = jnp.zeros((A_HEADS * A_LAT, A_WIDTH), f32)
    for h in range(A_HEADS):
        wbd = wbd.at[h * A_LAT:(h + 1) * A_LAT, h * A_VDIM:(h + 1) * A_VDIM].set(w_uv[h])
    blk = jnp.arange(128)
    tri = jnp.concatenate([blk[:, None] < blk[None, :], jnp.ones((128, 128), bool)], axis=1).astype(bf16)
    kern = functools.partial(_dsa_kernel, S=S, TQ=TQ, TK=TK, topk=topk)
    return pl.pallas_call(
        kern,
        grid=(B, S // TQ),
        in_specs=[
            pl.BlockSpec((1, TQ, A_HEADS * A_LAT), lambda b, i: (b, i, 0)),
            pl.BlockSpec((1, S, A_LAT), lambda b, i: (b, 0, 0)),
            pl.BlockSpec((1, TQ, IDX_HEADS * IDX_DIM), lambda b, i: (b, i, 0)),
            pl.BlockSpec((1, S, IDX_DIM), lambda b, i: (b, 0, 0)),
            pl.BlockSpec((1, IDX_HEADS, TQ), lambda b, i: (b, 0, i)),
            pl.BlockSpec((1, A_LAT), lambda b, i: (0, 0)),
            pl.BlockSpec((1, A_LAT), lambda b, i: (0, 0)),
            pl.BlockSpec((A_HEADS * A_LAT, A_WIDTH), lambda b, i: (0, 0)),
            pl.BlockSpec((128, 256), lambda b, i: (0, 0)),
        ],
        out_specs=pl.BlockSpec((1, TQ, A_WIDTH), lambda b, i: (b, i, 0)),
        out_shape=jax.ShapeDtypeStruct((B, S, A_WIDTH), f32),
        scratch_shapes=[
            pltpu.VMEM((S, 2 * A_LAT), bf16),
            pltpu.VMEM((S, IDX_DIM), bf16),
            pltpu.VMEM((TQ, S), jnp.int32),
            pltpu.VMEM((S, TQ), jnp.int16),
            pltpu.VMEM((S, TQ), jnp.int16),
            pltpu.VMEM((A_HEADS * TQ, 2 * A_LAT), bf16),
            *[pltpu.VMEM((TQ, 1), f32) for _ in range(A_HEADS)],
            *[pltpu.VMEM((TQ, 2 * A_LAT), f32) for _ in range(A_HEADS)],
        ],
        compiler_params=pltpu.CompilerParams(dimension_semantics=("arbitrary", "arbitrary")),
        name="dsa_attention",
    )(q_lat, kv, iq, ik, iw.transpose(0, 2, 1), q_gain.reshape(1, -1), k_gain.reshape(1, -1),
      wbd.astype(bf16), tri)


CONV_HALO = 8


def _split_bf16(x, terms):
    parts = []
    for _ in range(terms):
        p = x.astype(bf16)
        parts.append(p)
        x = x - p.astype(f32)
    return parts


def _dot_exact_lhs(mask_bf16, x):
    return sum(jnp.dot(mask_bf16, p, preferred_element_type=f32) for p in _split_bf16(x, 3))


def _dot_exact_rhs(x, mask_bf16):
    return sum(jnp.dot(p, mask_bf16, preferred_element_type=f32) for p in _split_bf16(x, 3))


def _dot_3pass(a, b):
    ah, al = _split_bf16(a, 2)
    bh, bl = _split_bf16(b, 2)
    return (jnp.dot(ah, bh, preferred_element_type=f32) + jnp.dot(ah, bl, preferred_element_type=f32)
            + jnp.dot(al, bh, preferred_element_type=f32))


def _softplus(x):
    return jnp.maximum(x, 0.0) + jnp.log1p(jnp.exp(-jnp.abs(x)))


def _gdn_kernel(q_ref, k_ref, v_ref, z_ref, a_ref, b_ref, at_ref, bt_ref, cw_ref, alog_ref, dtb_ref,
                gain_ref, o_ref, tail_scr, act_scr, *state_scr, SB):
    sj = pl.program_id(1)

    @pl.when(sj == 0)
    def _reset():
        tail_scr[...] = jnp.zeros(tail_scr.shape, f32)
        for st_ref in state_scr:
            st_ref[...] = jnp.zeros((B_DIM, B_DIM), f32)

    cw = cw_ref[...]
    for idx, x_ref in enumerate((q_ref, k_ref, v_ref)):
        x = x_ref[0]
        xc = jnp.concatenate([tail_scr[idx], x], axis=0)
        y = jnp.zeros((SB, B_WIDTH), f32)
        for j in range(CONV_W):
            lo = CONV_HALO - (CONV_W - 1) + j
            y = y + cw[j:j + 1, idx * B_WIDTH:(idx + 1) * B_WIDTH] * xc[lo:lo + SB]
        tail_scr[idx] = x[SB - CONV_HALO:SB]
        act_scr[idx] = y * jax.nn.sigmoid(y)

    ri = lax.broadcasted_iota(jnp.int32, (CHUNK, CHUNK), 0)
    ci = lax.broadcasted_iota(jnp.int32, (CHUNK, CHUNK), 1)
    tril = ri >= ci
    strict = ri > ci
    tril_b = jnp.where(tril, 1.0, 0.0).astype(bf16)
    triu_b = jnp.where(ri <= ci, 1.0, 0.0).astype(bf16)
    eye = jnp.where(ri == ci, 1.0, 0.0)
    nt = (((1,), (1,)), ((), ()))

    def chunk(c, carry):
        rows = pl.ds(pl.multiple_of(c * CHUNK, CHUNK), CHUNK)
        H = range(B_HEADS)
        lanes = [slice(h * B_DIM, (h + 1) * B_DIM) for h in H]
        lp = dict(preferred_element_type=f32)
        q = [act_scr[0, rows, lanes[h]] for h in H]
        k = [act_scr[1, rows, lanes[h]] for h in H]
        v = [act_scr[2, rows, lanes[h]] for h in H]
        q = [x * lax.rsqrt(jnp.sum(x * x, -1, keepdims=True) + EPS) * (B_DIM ** -0.5) for x in q]
        k = [x * lax.rsqrt(jnp.sum(x * x, -1, keepdims=True) + EPS) for x in k]
        neg_rate = [-jnp.exp(alog_ref[0:1, h:h + 1]) for h in H]
        dtb = [dtb_ref[0:1, h:h + 1] for h in H]
        g_col = [neg_rate[h] * _softplus(a_ref[0, rows, h:h + 1] + dtb[h]) for h in H]
        g_row = [neg_rate[h] * _softplus(at_ref[0, h, pl.ds(c, 1), :] + dtb[h]) for h in H]
        beta = [jax.nn.sigmoid(b_ref[0, rows, h:h + 1]) for h in H]
        Gc = [_dot_exact_lhs(tril_b, jnp.broadcast_to(g_col[h], (CHUNK, CHUNK))) for h in H]
        Gr = [_dot_exact_rhs(jnp.broadcast_to(g_row[h], (CHUNK, CHUNK)), triu_b) for h in H]
        G = [x[:, 0:1] for x in Gc]
        G_last = [x[CHUNK - 1:CHUNK, 0:1] for x in Gc]
        decay = [jnp.exp(jnp.where(tril, Gc[h] - Gr[h], -jnp.inf)) for h in H]
        kb = [k[h] * beta[h] for h in H]
        kbf = [x.astype(bf16) for x in k]
        L = [jnp.where(strict, lax.dot_general(kb[h].astype(bf16), kbf[h], nt, **lp) * decay[h], 0.0) for h in H]
        P = [eye - x for x in L]
        M = [_dot_3pass(x, x) for x in L]
        for lvl in range(5):
            P = [P[h] + _dot_3pass(P[h], M[h]) for h in H]
            if lvl < 4:
                M = [_dot_3pass(x, x) for x in M]
        Tb = [x.astype(bf16) for x in P]
        eG = [jnp.exp(x) for x in G]
        u = [jnp.dot(Tb[h], (v[h] * beta[h]).astype(bf16), **lp) for h in H]
        w = [jnp.dot(Tb[h], (kb[h] * eG[h]).astype(bf16), **lp) for h in H]
        attn = [lax.dot_general(q[h].astype(bf16), kbf[h], nt, **lp) * decay[h] for h in H]
        q_dec = [q[h] * eG[h] for h in H]
        k_dec = [k[h] * jnp.exp(G_last[h] - G[h]) for h in H]
        st = [state_scr[h][...] for h in H]
        stb = [x.astype(bf16) for x in st]
        v_new = [u[h] - jnp.dot(w[h].astype(bf16), stb[h], **lp) for h in H]
        vnb = [x.astype(bf16) for x in v_new]
        o = [jnp.dot(q_dec[h].astype(bf16), stb[h], **lp) + jnp.dot(attn[h].astype(bf16), vnb[h], **lp) for h in H]
        for h in H:
            state_scr[h][...] = st[h] * jnp.exp(G_last[h]) + lax.dot_general(
                k_dec[h].astype(bf16), vnb[h], (((0,), (0,)), ((), ())), **lp)
        for h in H:
            zh = z_ref[0, rows, lanes[h]]
            on = o[h] * lax.rsqrt(jnp.mean(o[h] * o[h], -1, keepdims=True) + EPS) * gain_ref[...]
            o_ref[0, rows, lanes[h]] = on * (zh * jax.nn.sigmoid(zh))
        return carry

    lax.fori_loop(0, SB // CHUNK, chunk, 0)


def gated_deltanet(qkvz, a, b, conv_w, a_log, dt_bias, norm_gain, *, SB=512):
    B, S, _ = qkvz.shape
    SB = min(SB, S)
    nch = S // CHUNK
    at = a.transpose(0, 2, 1).reshape(B, B_HEADS, nch, CHUNK)
    bt = b.transpose(0, 2, 1).reshape(B, B_HEADS, nch, CHUNK)
    wide = pl.BlockSpec((1, SB, B_WIDTH), lambda bi, j: (bi, j, 0))
    part = [pl.BlockSpec((1, SB, B_WIDTH), functools.partial(lambda bi, j, n: (bi, j, n), n=n))
            for n in range(4)]
    narrow = pl.BlockSpec((1, SB, B_HEADS), lambda bi, j: (bi, j, 0))
    rowwise = pl.BlockSpec((1, B_HEADS, SB // CHUNK, CHUNK), lambda bi, j: (bi, 0, j, 0))
    return pl.pallas_call(
        functools.partial(_gdn_kernel, SB=SB),
        grid=(B, S // SB),
        in_specs=[*part, narrow, narrow, rowwise, rowwise,
                  pl.BlockSpec((CONV_W, 3 * B_WIDTH), lambda bi, j: (0, 0)),
                  pl.BlockSpec((1, B_HEADS), lambda bi, j: (0, 0)),
                  pl.BlockSpec((1, B_HEADS), lambda bi, j: (0, 0)),
                  pl.BlockSpec((1, B_DIM), lambda bi, j: (0, 0))],
        out_specs=wide,
        out_shape=jax.ShapeDtypeStruct((B, S, B_WIDTH), f32),
        scratch_shapes=[pltpu.VMEM((3, CONV_HALO, B_WIDTH), f32),
                        pltpu.VMEM((3, SB, B_WIDTH), f32),
                        *[pltpu.VMEM((B_DIM, B_DIM), f32) for _ in range(B_HEADS)]],
        compiler_params=pltpu.CompilerParams(dimension_semantics=("arbitrary", "arbitrary")),
        name="gated_deltanet",
    )(qkvz, qkvz, qkvz, qkvz, a, b, at, bt, conv_w,
      a_log.reshape(1, -1), dt_bias.reshape(1, -1), norm_gain.reshape(1, -1))


PEER_SLOTS = PEER_HEADS * PEER_TOPK
PEER_HALF = PEER_QDIM // 2
ROW_WORDS = D_MODEL // 2
ROW_SUBL = ROW_WORDS // 128
STAGE_STRIDE = PEER_SLOTS + 8


def _top16_rows(s, iota_rows, fill, vals_scr, idx_scr, payload=None, pay_scr=None):
    for r in range(PEER_TOPK):
        m = jnp.max(s, axis=0, keepdims=True)
        first = jnp.min(jnp.where(s == m, iota_rows, fill), axis=0, keepdims=True)
        taken = iota_rows == first
        vals_scr[r:r + 1, :] = m
        if payload is None:
            idx_scr[r:r + 1, :] = first.astype(jnp.int32)
        else:
            pay_scr[r:r + 1, :] = jnp.max(jnp.where(taken, payload, -1.0), axis=0,
                                          keepdims=True).astype(jnp.int32)
        s = jnp.where(taken, -jnp.inf, s)


def _peer_route_kernel(h_ref, wq_ref, sk_ref, off_ref, gate_ref,
                       v1_scr, i1_scr, v2_scr, i2_scr, cv_scr, ce_scr, eid_scr, gate_scr, *, T):
    q = jnp.dot(h_ref[...].astype(bf16), wq_ref[...], preferred_element_type=f32).astype(bf16)
    gate_scr[...] = jnp.zeros(gate_scr.shape, f32)
    kiota = lax.broadcasted_iota(jnp.int32, (PEER_NKEYS, T), 0).astype(f32)
    sub8 = lax.broadcasted_iota(jnp.int32, (8, T), 0).astype(f32)
    for p in range(PEER_HEADS):
        for half, (vs, is_) in enumerate(((v1_scr, i1_scr), (v2_scr, i2_scr))):
            g = 2 * p + half
            s = lax.dot_general(sk_ref[g], q[:, g * PEER_HALF:(g + 1) * PEER_HALF],
                                (((1,), (1,)), ((), ())), preferred_element_type=f32)
            _top16_rows(s, kiota, PEER_NKEYS, vs, is_)
        v1 = v1_scr[...]
        v2 = v2_scr[...]
        e1 = (i1_scr[...] * PEER_NKEYS).astype(f32)
        e2 = i2_scr[...].astype(f32)
        cv = [v1[0:1] + v2[0:8], v1[0:1] + v2[8:16]]
        ce = [e1[0:1] + e2[0:8], e1[0:1] + e2[8:16]]
        cf = [sub8, sub8 + 8]
        for a in range(1, 8):
            cv.append(v1[a:a + 1] + v2[0:8])
            ce.append(e1[a:a + 1] + e2[0:8])
            cf.append(sub8 + a * PEER_TOPK)
        cv.append(v1[8:16] + v2[0:1])
        ce.append(e1[8:16] + e2[0:1])
        cf.append((sub8 + 8) * PEER_TOPK)
        cand = jnp.concatenate(cv, axis=0)
        cexp = jnp.concatenate(ce, axis=0)
        cflat = jnp.concatenate(cf, axis=0)
        _top16_rows(cand, cflat, PEER_TOPK * PEER_TOPK, cv_scr, None, payload=cexp, pay_scr=ce_scr)
        top = cv_scr[...]
        ex = jnp.exp(top - top[0:1])
        gate_scr[pl.ds(2 * p * PEER_TOPK + 1, PEER_TOPK, stride=2), :] = ex / jnp.sum(ex, axis=0, keepdims=True)
        eid_scr[p * PEER_TOPK:(p + 1) * PEER_TOPK, :] = ce_scr[...] * ROW_SUBL
    off_ref[...] = eid_scr[...]
    gate_ref[...] = gate_scr[...].T


def peer_route(h, w_query, sub_keys, *, T=128):
    N, D = h.shape
    sk = sub_keys.reshape(PEER_HEADS * 2, PEER_NKEYS, PEER_HALF).astype(bf16)
    kern = functools.partial(_peer_route_kernel, T=T)
    return pl.pallas_call(
        kern,
        grid=(N // T,),
        in_specs=[
            pl.BlockSpec((T, D), lambda i: (i, 0)),
            pl.BlockSpec((D, PEER_HEADS * PEER_QDIM), lambda i: (0, 0)),
            pl.BlockSpec((PEER_HEADS * 2, PEER_NKEYS, PEER_HALF), lambda i: (0, 0, 0)),
        ],
        out_specs=[pl.BlockSpec((PEER_SLOTS, T), lambda i: (0, i)),
                   pl.BlockSpec((T, 2 * PEER_SLOTS), lambda i: (i, 0))],
        out_shape=[jax.ShapeDtypeStruct((PEER_SLOTS, N), jnp.int32),
                   jax.ShapeDtypeStruct((N, 2 * PEER_SLOTS), f32)],
        scratch_shapes=[
            pltpu.VMEM((PEER_TOPK, T), f32), pltpu.VMEM((PEER_TOPK, T), jnp.int32),
            pltpu.VMEM((PEER_TOPK, T), f32), pltpu.VMEM((PEER_TOPK, T), jnp.int32),
            pltpu.VMEM((PEER_TOPK, T), f32), pltpu.VMEM((PEER_TOPK, T), jnp.int32),
            pltpu.VMEM((PEER_SLOTS, T), jnp.int32), pltpu.VMEM((2 * PEER_SLOTS, T), f32),
        ],
        compiler_params=pltpu.CompilerParams(dimension_semantics=("arbitrary",)),
        name="peer_route",
    )(h, w_query.astype(bf16), sk)


def pack_table(tab):
    bits = lax.bitcast_convert_type(tab.astype(bf16), jnp.uint16).astype(jnp.uint32)
    words = (bits[:, :ROW_WORDS] << 16) | bits[:, ROW_WORDS:]
    return lax.bitcast_convert_type(words, jnp.int32).reshape(tab.shape[0] * ROW_SUBL, 128)


GROUP = 8


def _fetch_slots(off_ref, t0, tab_ref, stage, first, last):
    for k in range(first, last):
        offs = off_ref.at[k, pl.ds(t0, GROUP)]
        for i in range(GROUP):
            off = pl.multiple_of(offs[i], ROW_SUBL)
            stage[i, pl.ds(k, ROW_SUBL, stride=STAGE_STRIDE), :] = tab_ref[pl.ds(off, ROW_SUBL), :]


def _staged_rows(stage, i):
    chunks = [stage[i, c * STAGE_STRIDE:c * STAGE_STRIDE + PEER_SLOTS, :] for c in range(ROW_SUBL)]
    return pltpu.bitcast(jnp.concatenate(chunks, axis=1), bf16)


def _pipelined_groups(off_ref, tab_ref, stage_a, stage_b, start, step, finish, T):
    ngroups = T // GROUP
    per_step = PEER_SLOTS // GROUP
    _fetch_slots(off_ref, 0, tab_ref, stage_a, 0, PEER_SLOTS)

    def run(g, stage, g_next, stage_next):
        t_next = pl.multiple_of(g_next * GROUP, GROUP)
        carry = start(g)
        for i in range(GROUP):
            carry = step(i, carry, stage)
            _fetch_slots(off_ref, t_next, tab_ref, stage_next, i * per_step, (i + 1) * per_step)
        finish(g, carry)

    def pair(j, carry):
        g0 = 2 * j
        run(g0, stage_a, g0 + 1, stage_b)
        run(g0 + 1, stage_b, jnp.minimum(g0 + 2, ngroups - 1), stage_a)
        return carry

    lax.fori_loop(0, ngroups // 2, pair, 0)


def _peer_u_kernel(off_ref, h_ref, gate_ref, tab_ref, w_ref, stage_a, stage_b, *, T):
    row16 = lax.broadcasted_iota(jnp.int32, (2 * GROUP, 2 * PEER_SLOTS), 0) & (GROUP - 1)
    odd = (lax.broadcasted_iota(jnp.int32, (GROUP, 2 * PEER_SLOTS), 1) & 1) == 1

    def start(g):
        h8 = h_ref[pl.ds(pl.multiple_of(g * GROUP, GROUP), GROUP), :]
        h16 = jnp.concatenate([h8[:, :ROW_WORDS], h8[:, ROW_WORDS:]], axis=0).astype(bf16)
        return h16, jnp.zeros((2 * GROUP, 2 * PEER_SLOTS), f32)

    def step(i, carry, stage):
        h16, acc = carry
        res = lax.dot_general(h16, _staged_rows(stage, i), (((1,), (1,)), ((), ())),
                              preferred_element_type=f32)
        return h16, jnp.where(row16 == i, res, acc)

    def finish(g, carry):
        t0 = pl.multiple_of(g * GROUP, GROUP)
        acc = carry[1]
        dots = acc[0:GROUP] + pltpu.roll(acc[GROUP:], 1, axis=1)
        gelu = 0.5 * dots * (1.0 + lax.erf(dots * (2.0 ** -0.5)))
        w_ref[pl.ds(t0, GROUP), :] = jnp.where(odd, gelu * gate_ref[pl.ds(t0, GROUP), :], 0.0)

    _pipelined_groups(off_ref, tab_ref, stage_a, stage_b, start, step, finish, T)


def _peer_v_kernel(off_ref, w_ref, x_ref, g2_ref, tab_ref, o_ref, stage_a, stage_b, *, T):
    row16 = lax.broadcasted_iota(jnp.int32, (2 * GROUP, ROW_WORDS), 0) & (GROUP - 1)

    def start(g):
        w8 = w_ref[pl.ds(pl.multiple_of(g * GROUP, GROUP), GROUP), :]
        w16 = jnp.concatenate([w8, pltpu.roll(w8, 2 * PEER_SLOTS - 1, axis=1)], axis=0).astype(bf16)
        return w16, jnp.zeros((2 * GROUP, ROW_WORDS), f32)

    def step(i, carry, stage):
        w16, acc = carry
        res = jnp.dot(w16, _staged_rows(stage, i), preferred_element_type=f32)
        return w16, jnp.where(row16 == i, res, acc)

    def finish(g, carry):
        t0 = pl.multiple_of(g * GROUP, GROUP)
        acc = carry[1]
        y = jnp.concatenate([acc[0:GROUP], acc[GROUP:]], axis=1)
        o_ref[pl.ds(t0, GROUP), :] = x_ref[pl.ds(t0, GROUP), :] + g2_ref[0] * y

    _pipelined_groups(off_ref, tab_ref, stage_a, stage_b, start, step, finish, T)


_TABLE_VMEM_LIMIT = 56 * 1024 * 1024


def _table_spec():
    return pl.BlockSpec((N_EXPERTS * ROW_SUBL, 128), lambda i: (0, 0), pipeline_mode=pl.Buffered(1))


def _stage_scratch():
    return [pltpu.VMEM((GROUP, ROW_SUBL * STAGE_STRIDE, 128), jnp.int32) for _ in range(2)]


def _offset_spec(T):
    return pl.BlockSpec((PEER_SLOTS, T), lambda i: (0, i), memory_space=pltpu.SMEM)


def peer_u(off, h, gate, u_pk, *, T=256):
    N, D = h.shape
    return pl.pallas_call(
        functools.partial(_peer_u_kernel, T=T),
        grid=(N // T,),
        in_specs=[
            _offset_spec(T),
            pl.BlockSpec((T, D), lambda i: (i, 0)),
            pl.BlockSpec((T, 2 * PEER_SLOTS), lambda i: (i, 0)),
            _table_spec(),
        ],
        out_specs=pl.BlockSpec((T, 2 * PEER_SLOTS), lambda i: (i, 0)),
        out_shape=jax.ShapeDtypeStruct((N, 2 * PEER_SLOTS), f32),
        scratch_shapes=_stage_scratch(),
        compiler_params=pltpu.CompilerParams(dimension_semantics=("arbitrary",),
                                             vmem_limit_bytes=_TABLE_VMEM_LIMIT),
        name="peer_u",
    )(off, h, gate, u_pk)


def peer_v(off, w, x, g2, v_pk, *, T=256):
    N, D = x.shape
    per_batch = N // g2.shape[0]
    return pl.pallas_call(
        functools.partial(_peer_v_kernel, T=T),
        grid=(N // T,),
        in_specs=[
            _offset_spec(T),
            pl.BlockSpec((T, 2 * PEER_SLOTS), lambda i: (i, 0)),
            pl.BlockSpec((T, D), lambda i: (i, 0)),
            pl.BlockSpec((1, 1, D), lambda i: (i * T // per_batch, 0, 0)),
            _table_spec(),
        ],
        out_specs=pl.BlockSpec((T, D), lambda i: (i, 0)),
        out_shape=jax.ShapeDtypeStruct((N, D), f32),
        scratch_shapes=_stage_scratch(),
        compiler_params=pltpu.CompilerParams(dimension_semantics=("arbitrary",),
                                             vmem_limit_bytes=_TABLE_VMEM_LIMIT),
        name="peer_v",
    )(off, w, x, g2.reshape(g2.shape[0], 1, D), v_pk)


def kernel(x, c, w_ada, b_ada, w_in, q_gain, k_gain, w_uv, conv_w, a_log, dt_bias,
           gdn_gain, w_out, w_query, sub_keys, u_tab, v_tab):
    B, S, D = x.shape
    xf = x.reshape(B * S, D)
    for l in range(DEPTH):
        sh1, sc1, g1, sh2, sc2, g2 = jnp.split(adaln(c, w_ada[l], b_ada[l]), 6, axis=-1)
        aq, akv, iq, small, qkvz = in_projection(xf, sh1, sc1, w_in[l])
        small = small.reshape(B, S, -1)
        ik, iw, ba, bb = (small[..., lo:hi] for lo, hi in (SMALL_IK, SMALL_IW, SMALL_BA, SMALL_BB))
        ya = dsa_attention(aq.reshape(B, S, -1), akv.reshape(B, S, -1), iq.reshape(B, S, -1), ik, iw,
                           q_gain[l], k_gain[l], w_uv[l])
        yb = gated_deltanet(qkvz.reshape(B, S, -1), ba, bb, conv_w[l], a_log[l], dt_bias[l], gdn_gain[l])
        x1, h2 = out_projection(ya.reshape(B * S, -1), yb.reshape(B * S, -1), xf, g1, sh2, sc2, w_out[l])
        off, gate = peer_route(h2, w_query[l], sub_keys[l])
        w = peer_u(off, h2, gate, pack_table(u_tab[l]))
        xf = peer_v(off, w, x1, g2, pack_table(v_tab[l]))
    return xf.reshape(B, S, D)
```

```python
import functools
import math
import jax
import jax.numpy as jnp
from jax import lax
import numpy as np
from jax.experimental import pallas as pl
from jax.experimental.pallas import tpu as pltpu

D_MODEL = 1024
BATCH = 8
SEQ = 4096
DEPTH = 1

EPS = 1e-6
A_HEADS = 8
A_LAT = 128
A_VDIM = 64
IDX_HEADS = 4
IDX_DIM = 64
IDX_TOPK_MAX = 256
Q_BLOCK = 128
B_HEADS = 4
B_DIM = 128
CONV_W = 4
CHUNK = 64
PEER_HEADS = 8
PEER_NKEYS = 128
PEER_QDIM = 256
PEER_TOPK = 16
PEER_BLOCK = 128
N_EXPERTS = PEER_NKEYS * PEER_NKEYS

A_WIDTH = A_HEADS * A_VDIM
B_WIDTH = B_HEADS * B_DIM
MIX_WIDTH = A_WIDTH + B_WIDTH
IN_SPLITS = (A_HEADS * A_LAT, A_LAT, IDX_HEADS * IDX_DIM, IDX_DIM, IDX_HEADS,
             B_WIDTH, B_WIDTH, B_WIDTH, B_WIDTH, B_HEADS, B_HEADS)
IN_WIDTH = sum(IN_SPLITS)


f32 = jnp.float32
bf16 = jnp.bfloat16
INT_MIN = -2 ** 31
SUM_LANE = 2
_PROJ_VMEM_LIMIT = 48 * 1024 * 1024


def _rms_modulate(x, shift, scale):
    xn = x * lax.rsqrt(jnp.mean(x * x, -1, keepdims=True) + EPS)
    return xn * (1.0 + scale) + shift


def _adaln_kernel(c_ref, w_ref, b_ref, o_ref):
    c = c_ref[...]
    s = (c * jax.nn.sigmoid(c)).astype(bf16)
    o_ref[...] = jnp.dot(s, w_ref[...].astype(bf16), preferred_element_type=f32) + b_ref[...]


def adaln(c, w, b, *, tn=512):
    B, D = c.shape
    N = w.shape[1]
    return pl.pallas_call(
        _adaln_kernel,
        grid=(N // tn,),
        in_specs=[pl.BlockSpec((B, D), lambda j: (0, 0)),
                  pl.BlockSpec((D, tn), lambda j: (0, j)),
                  pl.BlockSpec((1, tn), lambda j: (0, j))],
        out_specs=pl.BlockSpec((B, tn), lambda j: (0, j)),
        out_shape=jax.ShapeDtypeStruct((B, N), f32),
        name="adaln",
    )(c, w, b.reshape(1, N))


SMALL_IK = (0, IDX_DIM)
SMALL_IW = (IDX_DIM, IDX_DIM + IDX_HEADS)
SMALL_BA = (IDX_DIM + IDX_HEADS, IDX_DIM + IDX_HEADS + B_HEADS)
SMALL_BB = (IDX_DIM + IDX_HEADS + B_HEADS, IDX_DIM + IDX_HEADS + 2 * B_HEADS)


def _inproj_kernel(x_ref, sh_ref, sc_ref, wq_ref, wkv_ref, wiq_ref, wsm_ref, wb_ref,
                   oq_ref, okv_ref, oiq_ref, osm_ref, ob_ref):
    h = _rms_modulate(x_ref[...], sh_ref[0], sc_ref[0]).astype(bf16)
    for w_ref, o_ref in ((wq_ref, oq_ref), (wkv_ref, okv_ref), (wiq_ref, oiq_ref),
                         (wsm_ref, osm_ref), (wb_ref, ob_ref)):
        o_ref[...] = jnp.dot(h, w_ref[...], preferred_element_type=f32)


def in_projection(x, shift, scale, w_in, *, tm=512):
    N, D = x.shape
    per_batch = N // shift.shape[0]
    o = [int(v) for v in np.cumsum((0,) + IN_SPLITS)]
    wb = w_in.astype(bf16)
    w_q, w_kv, w_iq = wb[:, o[0]:o[1]], wb[:, o[1]:o[2]], wb[:, o[2]:o[3]]
    w_small = jnp.concatenate([wb[:, o[3]:o[5]], wb[:, o[9]:o[11]],
                               jnp.zeros((D, 128 - SMALL_BB[1]), bf16)], axis=1)
    w_b = wb[:, o[5]:o[9]]
    weights = (w_q, w_kv, w_iq, w_small, w_b)
    mod = pl.BlockSpec((1, 1, D), lambda i: (i * tm // per_batch, 0, 0))
    return pl.pallas_call(
        _inproj_kernel,
        grid=(N // tm,),
        in_specs=[pl.BlockSpec((tm, D), lambda i: (i, 0)), mod, mod]
                 + [pl.BlockSpec(w.shape, lambda i: (0, 0), pipeline_mode=pl.Buffered(1)) for w in weights],
        out_specs=[pl.BlockSpec((tm, w.shape[1]), lambda i: (i, 0)) for w in weights],
        out_shape=[jax.ShapeDtypeStruct((N, w.shape[1]), f32) for w in weights],
        compiler_params=pltpu.CompilerParams(dimension_semantics=("arbitrary",),
                                             vmem_limit_bytes=_PROJ_VMEM_LIMIT),
        name="in_projection",
    )(x, shift[:, None, :], scale[:, None, :], *weights)


def _outproj_kernel(ya_ref, yb_ref, x_ref, g1_ref, sh_ref, sc_ref, wa_ref, wb_ref, x1_ref, h2_ref):
    mix = (jnp.dot(ya_ref[...].astype(bf16), wa_ref[...], preferred_element_type=f32)
           + jnp.dot(yb_ref[...].astype(bf16), wb_ref[...], preferred_element_type=f32))
    x1 = x_ref[...] + g1_ref[0] * mix
    x1_ref[...] = x1
    h2_ref[...] = _rms_modulate(x1, sh_ref[0], sc_ref[0])


def out_projection(ya, yb, x, gate, shift, scale, w_out, *, tm=512):
    N, D = x.shape
    per_batch = N // gate.shape[0]
    wb = w_out.astype(bf16)
    w_a, w_b = wb[:A_WIDTH], wb[A_WIDTH:]
    mod = pl.BlockSpec((1, 1, D), lambda i: (i * tm // per_batch, 0, 0))
    row = pl.BlockSpec((tm, D), lambda i: (i, 0))
    half = pl.BlockSpec((tm, A_WIDTH), lambda i: (i, 0))
    return pl.pallas_call(
        _outproj_kernel,
        grid=(N // tm,),
        in_specs=[half, half, row, mod, mod, mod,
                  pl.BlockSpec(w_a.shape, lambda i: (0, 0)), pl.BlockSpec(w_b.shape, lambda i: (0, 0))],
        out_specs=[row, row],
        out_shape=[jax.ShapeDtypeStruct((N, D), f32), jax.ShapeDtypeStruct((N, D), f32)],
        compiler_params=pltpu.CompilerParams(dimension_semantics=("arbitrary",),
                                             vmem_limit_bytes=_PROJ_VMEM_LIMIT),
        name="out_projection",
    )(ya, yb, x, gate[:, None, :], shift[:, None, :], scale[:, None, :], w_a, w_b)


def _dsa_kernel(q_ref, kv_ref, iq_ref, ik_ref, iwt_ref, qg_ref, kg_ref, wbd_ref, tri_ref, o_ref,
                kaug_scr, ikb_scr, key_scr, khi_scr, klo_scr, qaug_scr, *head_scr, S, TQ, TK, topk):
    qi = pl.program_id(1)
    scale = A_LAT ** -0.5
    idx_scale = (IDX_DIM ** -0.5) * (IDX_HEADS ** -0.5)

    @pl.when(qi == 0)
    def _prep_keys():
        kv = kv_ref[0]
        kn = kv * lax.rsqrt(jnp.mean(kv * kv, -1, keepdims=True) + EPS) * kg_ref[...]
        pos = lax.broadcasted_iota(jnp.int32, (S, A_LAT), 0)
        lane = lax.broadcasted_iota(jnp.int32, (S, A_LAT), 1)
        hi = (pos >> 6).astype(f32)
        lo = (pos & 63).astype(f32)
        extra = jnp.where(lane == 0, hi, jnp.where(lane == 1, lo, jnp.where(lane == SUM_LANE, 1.0, 0.0)))
        kaug_scr[:, :A_LAT] = kn.astype(bf16)
        kaug_scr[:, A_LAT:] = extra.astype(bf16)
        ikb_scr[...] = ik_ref[0].astype(bf16)

    q = q_ref[0]
    lane = lax.broadcasted_iota(jnp.int32, (TQ, A_LAT), 1)
    for h in range(A_HEADS):
        qh = q[:, h * A_LAT:(h + 1) * A_LAT]
        qn = qh * lax.rsqrt(jnp.mean(qh * qh, -1, keepdims=True) + EPS) * qg_ref[...] * scale
        slope = 2.0 ** (-8.0 * (h + 1) / A_HEADS)
        extra = jnp.where(lane == 0, slope * 64.0, jnp.where(lane == 1, slope, 0.0))
        qaug_scr[h * TQ:(h + 1) * TQ, :A_LAT] = qn.astype(bf16)
        qaug_scr[h * TQ:(h + 1) * TQ, A_LAT:] = extra.astype(bf16)

    nchunks = (qi * TQ + TQ + TK - 1) // TK
    qpos = qi * TQ + lax.broadcasted_iota(jnp.int32, (TQ, 1), 0)
    qpos_row = qi * TQ + lax.broadcasted_iota(jnp.int32, (1, TQ), 1)
    colb = lax.broadcasted_iota(jnp.int32, (TQ, TK), 1)
    rowb = lax.broadcasted_iota(jnp.int32, (TK, TQ), 0)
    iq = iq_ref[0]
    iqb = [iq[:, h * IDX_DIM:(h + 1) * IDX_DIM].astype(bf16) for h in range(IDX_HEADS)]
    iwt = iwt_ref[0]

    def score_chunk(c, carry):
        off = pl.multiple_of(c * TK, TK)
        ikc = ikb_scr[pl.ds(off, TK), :]
        acc = jnp.zeros((TK, TQ), f32)
        for h in range(IDX_HEADS):
            lg = lax.dot_general(ikc, iqb[h], (((1,), (1,)), ((), ())), preferred_element_type=f32)
            acc = acc + jnp.maximum(lg, 0.0) * iwt[h:h + 1, :]
        sc = acc * idx_scale
        sc = jnp.where(rowb + off <= qpos_row, sc, -jnp.inf)
        bits = pltpu.bitcast(sc, jnp.int32)
        key = jnp.where(bits < 0, bits ^ jnp.int32(0x7FFFFFFF), bits)
        key_scr[:, pl.ds(off, TK)] = key.T
        khi_scr[pl.ds(off, TK), :] = (key >> 16).astype(jnp.int16)
        klo_scr[pl.ds(off, TK), :] = ((key & 0xFFFF) - 32768).astype(jnp.int16)
        return carry

    lax.fori_loop(0, nchunks, score_chunk, 0)

    def tile16(row):
        return jnp.broadcast_to(row, (16, TQ)).astype(jnp.int16)

    def threshold_search(n):
        rows_n = n * TK

        def count16(ref, cmp, row):
            tile = tile16(row)
            x = ref[0:rows_n, :]
            hits = [cmp(x[j * 16:(j + 1) * 16], tile).astype(jnp.int16) for j in range(rows_n // 16)]
            while len(hits) > 1:
                hits = [a + b for a, b in zip(hits[::2], hits[1::2])]
            return jnp.sum(hits[0].astype(jnp.int32), axis=0, keepdims=True)

        def top_half_word(ref, wanted):
            def bit_step(i, t):
                cand = t + lax.shift_left(jnp.int32(1), jnp.int32(15) - i)
                cnt = count16(ref, lambda k, c: k >= c, cand)
                return jnp.where(cnt >= wanted, cand, t)
            return lax.fori_loop(0, 16, bit_step, jnp.full((1, TQ), -32768, jnp.int32))

        t_hi = top_half_word(khi_scr, topk)
        above = count16(khi_scr, lambda k, c: k > c, t_hi)
        t_hi_tile = tile16(t_hi)
        hi = khi_scr[0:rows_n, :]
        lo = klo_scr[0:rows_n, :]
        kept = [jnp.where(hi[j * 16:(j + 1) * 16] == t_hi_tile, lo[j * 16:(j + 1) * 16], jnp.int16(-32768))
                for j in range(rows_n // 16)]
        klo_scr[0:rows_n, :] = jnp.concatenate(kept, axis=0)
        t_lo = top_half_word(klo_scr, topk - above)
        n_gt = above + count16(klo_scr, lambda k, c: k > c, t_lo)
        return t_hi, t_lo, n_gt

    t_hi, t_lo, n_gt = lax.switch(nchunks - 1,
                                  [functools.partial(threshold_search, n) for n in range(1, S // TK + 1)])
    T = jnp.broadcast_to(t_hi * 65536 + (t_lo + 32768), (TQ, TQ)).T[:, 0:1]
    room = jnp.broadcast_to((topk - n_gt).astype(f32), (TQ, TQ)).T[:, 0:1]

    m_scr, acc_scr = head_scr[:A_HEADS], head_scr[A_HEADS:]
    for h in range(A_HEADS):
        m_scr[h][...] = jnp.full((TQ, 1), -jnp.inf, f32)
        acc_scr[h][...] = jnp.zeros((TQ, 2 * A_LAT), f32)

    def attend_chunk(c, ties_before):
        off = pl.multiple_of(c * TK, TK)
        keyc = key_scr[:, pl.ds(off, TK)]
        eq = keyc == T
        eqb = jnp.where(eq, 1.0, 0.0).astype(bf16)
        ties = ties_before
        admit = []
        for j in range(TK // 128):
            blk = slice(j * 128, (j + 1) * 128)
            cnt = jnp.dot(eqb[:, blk], tri_ref[...], preferred_element_type=f32)
            admit.append(eq[:, blk] & (cnt[:, :128] + ties < room))
            ties = ties + cnt[:, 128:]
        sel = (keyc > T) | jnp.concatenate(admit, axis=1)
        sel = sel & (colb + off <= qpos)
        kc = kaug_scr[pl.ds(off, TK), :]
        scores = lax.dot_general(qaug_scr[...], kc, (((1,), (1,)), ((), ())), preferred_element_type=f32)
        probs, alphas = [], []
        for h in range(A_HEADS):
            s = jnp.where(sel, scores[h * TQ:(h + 1) * TQ], -jnp.inf)
            m_old = m_scr[h][...]
            m_new = jnp.maximum(m_old, jnp.max(s, axis=1, keepdims=True))
            m_safe = jnp.where(m_new == -jnp.inf, 0.0, m_new)
            probs.append(jnp.exp(s - m_safe).astype(bf16))
            alphas.append(jnp.exp(m_old - m_safe))
            m_scr[h][...] = m_new
        for h in range(A_HEADS):
            acc_scr[h][...] = alphas[h] * acc_scr[h][...] + jnp.dot(probs[h], kc, preferred_element_type=f32)
        return ties

    lax.fori_loop(0, nchunks, attend_chunk, jnp.zeros((TQ, 128), f32))

    heads = []
    for h in range(A_HEADS):
        acc = acc_scr[h][...]
        row_sum = acc[:, A_LAT + SUM_LANE:A_LAT + SUM_LANE + 1]
        heads.append((acc[:, :A_LAT] / row_sum).astype(bf16))
    o = jnp.concatenate(heads, axis=1)
    o_ref[0] = jnp.dot(o, wbd_ref[...], preferred_element_type=f32)


def dsa_attention(q_lat, kv, iq, ik, iw, q_gain, k_gain, w_uv, *, TQ=128, TK=1024):
    B, S, _ = q_lat.shape
    topk = min(IDX_TOPK_MAX, S // 4)
    TK = min(TK, S)
    wbd = jnp.zeros((A_HEADS * A_LAT, A_WIDTH), f32)
    for h in range(A_HEADS):
        wbd = wbd.at[h * A_LAT:(h + 1) * A_LAT, h * A_VDIM:(h + 1) * A_VDIM].set(w_uv[h])
    blk = jnp.arange(128)
    tri = jnp.concatenate([blk[:, None] < blk[None, :], jnp.ones((128, 128), bool)], axis=1).astype(bf16)
    kern = functools.partial(_dsa_kernel, S=S, TQ=TQ, TK=TK, topk=topk)
    return pl.pallas_call(
        kern,
        grid=(B, S // TQ),
        in_specs=[
            pl.BlockSpec((1, TQ, A_HEADS * A_LAT), lambda b, i: (b, i, 0)),
            pl.BlockSpec((1, S, A_LAT), lambda b, i: (b, 0, 0)),
            pl.BlockSpec((1, TQ, IDX_HEADS * IDX_DIM), lambda b, i: (b, i, 0)),
            pl.BlockSpec((1, S, IDX_DIM), lambda b, i: (b, 0, 0)),
            pl.BlockSpec((1, IDX_HEADS, TQ), lambda b, i: (b, 0, i)),
            pl.BlockSpec((1, A_LAT), lambda b, i: (0, 0)),
            pl.BlockSpec((1, A_LAT), lambda b, i: (0, 0)),
            pl.BlockSpec((A_HEADS * A_LAT, A_WIDTH), lambda b, i: (0, 0)),
            pl.BlockSpec((128, 256), lambda b, i: (0, 0)),
        ],
        out_specs=pl.BlockSpec((1, TQ, A_WIDTH), lambda b, i: (b, i, 0)),
        out_shape=jax.ShapeDtypeStruct((B, S, A_WIDTH), f32),
        scratch_shapes=[
            pltpu.VMEM((S, 2 * A_LAT), bf16),
            pltpu.VMEM((S, IDX_DIM), bf16),
            pltpu.VMEM((TQ, S), jnp.int32),
            pltpu.VMEM((S, TQ), jnp.int16),
            pltpu.VMEM((S, TQ), jnp.int16),
            pltpu.VMEM((A_HEADS * TQ, 2 * A_LAT), bf16),
            *[pltpu.VMEM((TQ, 1), f32) for _ in range(A_HEADS)],
            *[pltpu.VMEM((TQ, 2 * A_LAT), f32) for _ in range(A_HEADS)],
        ],
        compiler_params=pltpu.CompilerParams(dimension_semantics=("arbitrary", "arbitrary")),
        name="dsa_attention",
    )(q_lat, kv, iq, ik, iw.transpose(0, 2, 1), q_gain.reshape(1, -1), k_gain.reshape(1, -1),
      wbd.astype(bf16), tri)


CONV_HALO = 8


def _split_bf16(x, terms):
    parts = []
    for _ in range(terms):
        p = x.astype(bf16)
        parts.append(p)
        x = x - p.astype(f32)
    return parts


def _dot_exact_lhs(mask_bf16, x):
    return sum(jnp.dot(mask_bf16, p, preferred_element_type=f32) for p in _split_bf16(x, 3))


def _dot_exact_rhs(x, mask_bf16):
    return sum(jnp.dot(p, mask_bf16, preferred_element_type=f32) for p in _split_bf16(x, 3))


def _dot_3pass(a, b):
    ah, al = _split_bf16(a, 2)
    bh, bl = _split_bf16(b, 2)
    return (jnp.dot(ah, bh, preferred_element_type=f32) + jnp.dot(ah, bl, preferred_element_type=f32)
            + jnp.dot(al, bh, preferred_element_type=f32))


def _softplus(x):
    return jnp.maximum(x, 0.0) + jnp.log1p(jnp.exp(-jnp.abs(x)))


def _gdn_kernel(q_ref, k_ref, v_ref, z_ref, a_ref, b_ref, at_ref, bt_ref, cw_ref, alog_ref, dtb_ref,
                gain_ref, o_ref, tail_scr, act_scr, *state_scr, SB):
    sj = pl.program_id(1)

    @pl.when(sj == 0)
    def _reset():
        tail_scr[...] = jnp.zeros(tail_scr.shape, f32)
        for st_ref in state_scr:
            st_ref[...] = jnp.zeros((B_DIM, B_DIM), f32)

    cw = cw_ref[...]
    for idx, x_ref in enumerate((q_ref, k_ref, v_ref)):
        x = x_ref[0]
        xc = jnp.concatenate([tail_scr[idx], x], axis=0)
        y = jnp.zeros((SB, B_WIDTH), f32)
        for j in range(CONV_W):
            lo = CONV_HALO - (CONV_W - 1) + j
            y = y + cw[j:j + 1, idx * B_WIDTH:(idx + 1) * B_WIDTH] * xc[lo:lo + SB]
        tail_scr[idx] = x[SB - CONV_HALO:SB]
        act_scr[idx] = y * jax.nn.sigmoid(y)

    ri = lax.broadcasted_iota(jnp.int32, (CHUNK, CHUNK), 0)
    ci = lax.broadcasted_iota(jnp.int32, (CHUNK, CHUNK), 1)
    tril = ri >= ci
    strict = ri > ci
    tril_b = jnp.where(tril, 1.0, 0.0).astype(bf16)
    triu_b = jnp.where(ri <= ci, 1.0, 0.0).astype(bf16)
    eye = jnp.where(ri == ci, 1.0, 0.0)
    nt = (((1,), (1,)), ((), ()))

    def chunk(c, carry):
        rows = pl.ds(pl.multiple_of(c * CHUNK, CHUNK), CHUNK)
        H = range(B_HEADS)
        lanes = [slice(h * B_DIM, (h + 1) * B_DIM) for h in H]
        lp = dict(preferred_element_type=f32)
        q = [act_scr[0, rows, lanes[h]] for h in H]
        k = [act_scr[1, rows, lanes[h]] for h in H]
        v = [act_scr[2, rows, lanes[h]] for h in H]
        q = [x * lax.rsqrt(jnp.sum(x * x, -1, keepdims=True) + EPS) * (B_DIM ** -0.5) for x in q]
        k = [x * lax.rsqrt(jnp.sum(x * x, -1, keepdims=True) + EPS) for x in k]
        neg_rate = [-jnp.exp(alog_ref[0:1, h:h + 1]) for h in H]
        dtb = [dtb_ref[0:1, h:h + 1] for h in H]
        g_col = [neg_rate[h] * _softplus(a_ref[0, rows, h:h + 1] + dtb[h]) for h in H]
        g_row = [neg_rate[h] * _softplus(at_ref[0, h, pl.ds(c, 1), :] + dtb[h]) for h in H]
        beta = [jax.nn.sigmoid(b_ref[0, rows, h:h + 1]) for h in H]
        Gc = [_dot_exact_lhs(tril_b, jnp.broadcast_to(g_col[h], (CHUNK, CHUNK))) for h in H]
        Gr = [_dot_exact_rhs(jnp.broadcast_to(g_row[h], (CHUNK, CHUNK)), triu_b) for h in H]
        G = [x[:, 0:1] for x in Gc]
        G_last = [x[CHUNK - 1:CHUNK, 0:1] for x in Gc]
        decay = [jnp.exp(jnp.where(tril, Gc[h] - Gr[h], -jnp.inf)) for h in H]
        kb = [k[h] * beta[h] for h in H]
        kbf = [x.astype(bf16) for x in k]
        L = [jnp.where(strict, lax.dot_general(kb[h].astype(bf16), kbf[h], nt, **lp) * decay[h], 0.0) for h in H]
        P = [eye - x for x in L]
        M = [_dot_3pass(x, x) for x in L]
        for lvl in range(5):
            P = [P[h] + _dot_3pass(P[h], M[h]) for h in H]
            if lvl < 4:
                M = [_dot_3pass(x, x) for x in M]
        Tb = [x.astype(bf16) for x in P]
        eG = [jnp.exp(x) for x in G]
        u = [jnp.dot(Tb[h], (v[h] * beta[h]).astype(bf16), **lp) for h in H]
        w = [jnp.dot(Tb[h], (kb[h] * eG[h]).astype(bf16), **lp) for h in H]
        attn = [lax.dot_general(q[h].astype(bf16), kbf[h], nt, **lp) * decay[h] for h in H]
        q_dec = [q[h] * eG[h] for h in H]
        k_dec = [k[h] * jnp.exp(G_last[h] - G[h]) for h in H]
        st = [state_scr[h][...] for h in H]
        stb = [x.astype(bf16) for x in st]
        v_new = [u[h] - jnp.dot(w[h].astype(bf16), stb[h], **lp) for h in H]
        vnb = [x.astype(bf16) for x in v_new]
        o = [jnp.dot(q_dec[h].astype(bf16), stb[h], **lp) + jnp.dot(attn[h].astype(bf16), vnb[h], **lp) for h in H]
        for h in H:
            state_scr[h][...] = st[h] * jnp.exp(G_last[h]) + lax.dot_general(
                k_dec[h].astype(bf16), vnb[h], (((0,), (0,)), ((), ())), **lp)
        for h in H:
            zh = z_ref[0, rows, lanes[h]]
            on = o[h] * lax.rsqrt(jnp.mean(o[h] * o[h], -1, keepdims=True) + EPS) * gain_ref[...]
            o_ref[0, rows, lanes[h]] = on * (zh * jax.nn.sigmoid(zh))
        return carry

    lax.fori_loop(0, SB // CHUNK, chunk, 0)


def gated_deltanet(qkvz, a, b, conv_w, a_log, dt_bias, norm_gain, *, SB=512):
    B, S, _ = qkvz.shape
    SB = min(SB, S)
    nch = S // CHUNK
    at = a.transpose(0, 2, 1).reshape(B, B_HEADS, nch, CHUNK)
    bt = b.transpose(0, 2, 1).reshape(B, B_HEADS, nch, CHUNK)
    wide = pl.BlockSpec((1, SB, B_WIDTH), lambda bi, j: (bi, j, 0))
    part = [pl.BlockSpec((1, SB, B_WIDTH), functools.partial(lambda bi, j, n: (bi, j, n), n=n))
            for n in range(4)]
    narrow = pl.BlockSpec((1, SB, B_HEADS), lambda bi, j: (bi, j, 0))
    rowwise = pl.BlockSpec((1, B_HEADS, SB // CHUNK, CHUNK), lambda bi, j: (bi, 0, j, 0))
    return pl.pallas_call(
        functools.partial(_gdn_kernel, SB=SB),
        grid=(B, S // SB),
        in_specs=[*part, narrow, narrow, rowwise, rowwise,
                  pl.BlockSpec((CONV_W, 3 * B_WIDTH), lambda bi, j: (0, 0)),
                  pl.BlockSpec((1, B_HEADS), lambda bi, j: (0, 0)),
                  pl.BlockSpec((1, B_HEADS), lambda bi, j: (0, 0)),
                  pl.BlockSpec((1, B_DIM), lambda bi, j: (0, 0))],
        out_specs=wide,
        out_shape=jax.ShapeDtypeStruct((B, S, B_WIDTH), f32),
        scratch_shapes=[pltpu.VMEM((3, CONV_HALO, B_WIDTH), f32),
                        pltpu.VMEM((3, SB, B_WIDTH), f32),
                        *[pltpu.VMEM((B_DIM, B_DIM), f32) for _ in range(B_HEADS)]],
        compiler_params=pltpu.CompilerParams(dimension_semantics=("arbitrary", "arbitrary")),
        name="gated_deltanet",
    )(qkvz, qkvz, qkvz, qkvz, a, b, at, bt, conv_w,
      a_log.reshape(1, -1), dt_bias.reshape(1, -1), norm_gain.reshape(1, -1))


PEER_SLOTS = PEER_HEADS * PEER_TOPK
PEER_HALF = PEER_QDIM // 2
ROW_WORDS = D_MODEL // 2
ROW_SUBL = ROW_WORDS // 128
STAGE_STRIDE = PEER_SLOTS + 8


def _top16_rows(s, iota_rows, fill, vals_scr, idx_scr, payload=None, pay_scr=None):
    for r in range(PEER_TOPK):
        m = jnp.max(s, axis=0, keepdims=True)
        first = jnp.min(jnp.where(s == m, iota_rows, fill), axis=0, keepdims=True)
        taken = iota_rows == first
        vals_scr[r:r + 1, :] = m
        if payload is None:
            idx_scr[r:r + 1, :] = first.astype(jnp.int32)
        else:
            pay_scr[r:r + 1, :] = jnp.max(jnp.where(taken, payload, -1.0), axis=0,
                                          keepdims=True).astype(jnp.int32)
        s = jnp.where(taken, -jnp.inf, s)


def _peer_route_kernel(h_ref, wq_ref, sk_ref, off_ref, gate_ref,
                       v1_scr, i1_scr, v2_scr, i2_scr, cv_scr, ce_scr, eid_scr, gate_scr, *, T):
    q = jnp.dot(h_ref[...].astype(bf16), wq_ref[...], preferred_element_type=f32).astype(bf16)
    gate_scr[...] = jnp.zeros(gate_scr.shape, f32)
    kiota = lax.broadcasted_iota(jnp.int32, (PEER_NKEYS, T), 0).astype(f32)
    sub8 = lax.broadcasted_iota(jnp.int32, (8, T), 0).astype(f32)
    for p in range(PEER_HEADS):
        for half, (vs, is_) in enumerate(((v1_scr, i1_scr), (v2_scr, i2_scr))):
            g = 2 * p + half
            s = lax.dot_general(sk_ref[g], q[:, g * PEER_HALF:(g + 1) * PEER_HALF],
                                (((1,), (1,)), ((), ())), preferred_element_type=f32)
            _top16_rows(s, kiota, PEER_NKEYS, vs, is_)
        v1 = v1_scr[...]
        v2 = v2_scr[...]
        e1 = (i1_scr[...] * PEER_NKEYS).astype(f32)
        e2 = i2_scr[...].astype(f32)
        cv = [v1[0:1] + v2[0:8], v1[0:1] + v2[8:16]]
        ce = [e1[0:1] + e2[0:8], e1[0:1] + e2[8:16]]
        cf = [sub8, sub8 + 8]
        for a in range(1, 8):
            cv.append(v1[a:a + 1] + v2[0:8])
            ce.append(e1[a:a + 1] + e2[0:8])
            cf.append(sub8 + a * PEER_TOPK)
        cv.append(v1[8:16] + v2[0:1])
        ce.append(e1[8:16] + e2[0:1])
        cf.append((sub8 + 8) * PEER_TOPK)
        cand = jnp.concatenate(cv, axis=0)
        cexp = jnp.concatenate(ce, axis=0)
        cflat = jnp.concatenate(cf, axis=0)
        _top16_rows(cand, cflat, PEER_TOPK * PEER_TOPK, cv_scr, None, payload=cexp, pay_scr=ce_scr)
        top = cv_scr[...]
        ex = jnp.exp(top - top[0:1])
        gate_scr[pl.ds(2 * p * PEER_TOPK + 1, PEER_TOPK, stride=2), :] = ex / jnp.sum(ex, axis=0, keepdims=True)
        eid_scr[p * PEER_TOPK:(p + 1) * PEER_TOPK, :] = ce_scr[...] * ROW_SUBL
    off_ref[...] = eid_scr[...]
    gate_ref[...] = gate_scr[...].T


def peer_route(h, w_query, sub_keys, *, T=128):
    N, D = h.shape
    sk = sub_keys.reshape(PEER_HEADS * 2, PEER_NKEYS, PEER_HALF).astype(bf16)
    kern = functools.partial(_peer_route_kernel, T=T)
    return pl.pallas_call(
        kern,
        grid=(N // T,),
        in_specs=[
            pl.BlockSpec((T, D), lambda i: (i, 0)),
            pl.BlockSpec((D, PEER_HEADS * PEER_QDIM), lambda i: (0, 0)),
            pl.BlockSpec((PEER_HEADS * 2, PEER_NKEYS, PEER_HALF), lambda i: (0, 0, 0)),
        ],
        out_specs=[pl.BlockSpec((PEER_SLOTS, T), lambda i: (0, i)),
                   pl.BlockSpec((T, 2 * PEER_SLOTS), lambda i: (i, 0))],
        out_shape=[jax.ShapeDtypeStruct((PEER_SLOTS, N), jnp.int32),
                   jax.ShapeDtypeStruct((N, 2 * PEER_SLOTS), f32)],
        scratch_shapes=[
            pltpu.VMEM((PEER_TOPK, T), f32), pltpu.VMEM((PEER_TOPK, T), jnp.int32),
            pltpu.VMEM((PEER_TOPK, T), f32), pltpu.VMEM((PEER_TOPK, T), jnp.int32),
            pltpu.VMEM((PEER_TOPK, T), f32), pltpu.VMEM((PEER_TOPK, T), jnp.int32),
            pltpu.VMEM((PEER_SLOTS, T), jnp.int32), pltpu.VMEM((2 * PEER_SLOTS, T), f32),
        ],
        compiler_params=pltpu.CompilerParams(dimension_semantics=("arbitrary",)),
        name="peer_route",
    )(h, w_query.astype(bf16), sk)


def pack_table(tab):
    bits = lax.bitcast_convert_type(tab.astype(bf16), jnp.uint16).astype(jnp.uint32)
    words = (bits[:, :ROW_WORDS] << 16) | bits[:, ROW_WORDS:]
    return lax.bitcast_convert_type(words, jnp.int32).reshape(tab.shape[0] * ROW_SUBL, 128)


GROUP = 8


def _fetch_slots(off_ref, t0, tab_ref, stage, first, last):
    for k in range(first, last):
        offs = off_ref.at[k, pl.ds(t0, GROUP)]
        for i in range(GROUP):
            off = pl.multiple_of(offs[i], ROW_SUBL)
            stage[i, pl.ds(k, ROW_SUBL, stride=STAGE_STRIDE), :] = tab_ref[pl.ds(off, ROW_SUBL), :]


def _staged_rows(stage, i):
    chunks = [stage[i, c * STAGE_STRIDE:c * STAGE_STRIDE + PEER_SLOTS, :] for c in range(ROW_SUBL)]
    return pltpu.bitcast(jnp.concatenate(chunks, axis=1), bf16)


def _pipelined_groups(off_ref, tab_ref, stage_a, stage_b, start, step, finish, T):
    ngroups = T // GROUP
    per_step = PEER_SLOTS // GROUP
    _fetch_slots(off_ref, 0, tab_ref, stage_a, 0, PEER_SLOTS)

    def run(g, stage, g_next, stage_next):
        t_next = pl.multiple_of(g_next * GROUP, GROUP)
        carry = start(g)
        for i in range(GROUP):
            carry = step(i, carry, stage)
            _fetch_slots(off_ref, t_next, tab_ref, stage_next, i * per_step, (i + 1) * per_step)
        finish(g, carry)

    def pair(j, carry):
        g0 = 2 * j
        run(g0, stage_a, g0 + 1, stage_b)
        run(g0 + 1, stage_b, jnp.minimum(g0 + 2, ngroups - 1), stage_a)
        return carry

    lax.fori_loop(0, ngroups // 2, pair, 0)


def _peer_u_kernel(off_ref, h_ref, gate_ref, tab_ref, w_ref, stage_a, stage_b, *, T):
    row16 = lax.broadcasted_iota(jnp.int32, (2 * GROUP, 2 * PEER_SLOTS), 0) & (GROUP - 1)
    odd = (lax.broadcasted_iota(jnp.int32, (GROUP, 2 * PEER_SLOTS), 1) & 1) == 1

    def start(g):
        h8 = h_ref[pl.ds(pl.multiple_of(g * GROUP, GROUP), GROUP), :]
        h16 = jnp.concatenate([h8[:, :ROW_WORDS], h8[:, ROW_WORDS:]], axis=0).astype(bf16)
        return h16, jnp.zeros((2 * GROUP, 2 * PEER_SLOTS), f32)

    def step(i, carry, stage):
        h16, acc = carry
        res = lax.dot_general(h16, _staged_rows(stage, i), (((1,), (1,)), ((), ())),
                              preferred_element_type=f32)
        return h16, jnp.where(row16 == i, res, acc)

    def finish(g, carry):
        t0 = pl.multiple_of(g * GROUP, GROUP)
        acc = carry[1]
        dots = acc[0:GROUP] + pltpu.roll(acc[GROUP:], 1, axis=1)
        gelu = 0.5 * dots * (1.0 + lax.erf(dots * (2.0 ** -0.5)))
        w_ref[pl.ds(t0, GROUP), :] = jnp.where(odd, gelu * gate_ref[pl.ds(t0, GROUP), :], 0.0)

    _pipelined_groups(off_ref, tab_ref, stage_a, stage_b, start, step, finish, T)


def _peer_v_kernel(off_ref, w_ref, x_ref, g2_ref, tab_ref, o_ref, stage_a, stage_b, *, T):
    row16 = lax.broadcasted_iota(jnp.int32, (2 * GROUP, ROW_WORDS), 0) & (GROUP - 1)

    def start(g):
        w8 = w_ref[pl.ds(pl.multiple_of(g * GROUP, GROUP), GROUP), :]
        w16 = jnp.concatenate([w8, pltpu.roll(w8, 2 * PEER_SLOTS - 1, axis=1)], axis=0).astype(bf16)
        return w16, jnp.zeros((2 * GROUP, ROW_WORDS), f32)

    def step(i, carry, stage):
        w16, acc = carry
        res = jnp.dot(w16, _staged_rows(stage, i), preferred_element_type=f32)
        return w16, jnp.where(row16 == i, res, acc)

    def finish(g, carry):
        t0 = pl.multiple_of(g * GROUP, GROUP)
        acc = carry[1]
        y = jnp.concatenate([acc[0:GROUP], acc[GROUP:]], axis=1)
        o_ref[pl.ds(t0, GROUP), :] = x_ref[pl.ds(t0, GROUP), :] + g2_ref[0] * y

    _pipelined_groups(off_ref, tab_ref, stage_a, stage_b, start, step, finish, T)


_TABLE_VMEM_LIMIT = 56 * 1024 * 1024


def _table_spec():
    return pl.BlockSpec((N_EXPERTS * ROW_SUBL, 128), lambda i: (0, 0), pipeline_mode=pl.Buffered(1))


def _stage_scratch():
    return [pltpu.VMEM((GROUP, ROW_SUBL * STAGE_STRIDE, 128), jnp.int32) for _ in range(2)]


def _offset_spec(T):
    return pl.BlockSpec((PEER_SLOTS, T), lambda i: (0, i), memory_space=pltpu.SMEM)


def peer_u(off, h, gate, u_pk, *, T=256):
    N, D = h.shape
    return pl.pallas_call(
        functools.partial(_peer_u_kernel, T=T),
        grid=(N // T,),
        in_specs=[
            _offset_spec(T),
            pl.BlockSpec((T, D), lambda i: (i, 0)),
            pl.BlockSpec((T, 2 * PEER_SLOTS), lambda i: (i, 0)),
            _table_spec(),
        ],
        out_specs=pl.BlockSpec((T, 2 * PEER_SLOTS), lambda i: (i, 0)),
        out_shape=jax.ShapeDtypeStruct((N, 2 * PEER_SLOTS), f32),
        scratch_shapes=_stage_scratch(),
        compiler_params=pltpu.CompilerParams(dimension_semantics=("arbitrary",),
                                             vmem_limit_bytes=_TABLE_VMEM_LIMIT),
        name="peer_u",
    )(off, h, gate, u_pk)


def peer_v(off, w, x, g2, v_pk, *, T=256):
    N, D = x.shape
    per_batch = N // g2.shape[0]
    return pl.pallas_call(
        functools.partial(_peer_v_kernel, T=T),
        grid=(N // T,),
        in_specs=[
            _offset_spec(T),
            pl.BlockSpec((T, 2 * PEER_SLOTS), lambda i: (i, 0)),
            pl.BlockSpec((T, D), lambda i: (i, 0)),
            pl.BlockSpec((1, 1, D), lambda i: (i * T // per_batch, 0, 0)),
            _table_spec(),
        ],
        out_specs=pl.BlockSpec((T, D), lambda i: (i, 0)),
        out_shape=jax.ShapeDtypeStruct((N, D), f32),
        scratch_shapes=_stage_scratch(),
        compiler_params=pltpu.CompilerParams(dimension_semantics=("arbitrary",),
                                             vmem_limit_bytes=_TABLE_VMEM_LIMIT),
        name="peer_v",
    )(off, w, x, g2.reshape(g2.shape[0], 1, D), v_pk)


def kernel(x, c, w_ada, b_ada, w_in, q_gain, k_gain, w_uv, conv_w, a_log, dt_bias,
           gdn_gain, w_out, w_query, sub_keys, u_tab, v_tab):
    B, S, D = x.shape
    xf = x.reshape(B * S, D)
    for l in range(DEPTH):
        sh1, sc1, g1, sh2, sc2, g2 = jnp.split(adaln(c, w_ada[l], b_ada[l]), 6, axis=-1)
        aq, akv, iq, small, qkvz = in_projection(xf, sh1, sc1, w_in[l])
        small = small.reshape(B, S, -1)
        ik, iw, ba, bb = (small[..., lo:hi] for lo, hi in (SMALL_IK, SMALL_IW, SMALL_BA, SMALL_BB))
        ya = dsa_attention(aq.reshape(B, S, -1), akv.reshape(B, S, -1), iq.reshape(B, S, -1), ik, iw,
                           q_gain[l], k_gain[l], w_uv[l])
        yb = gated_deltanet(qkvz.reshape(B, S, -1), ba, bb, conv_w[l], a_log[l], dt_bias[l], gdn_gain[l])
        x1, h2 = out_projection(ya.reshape(B * S, -1), yb.reshape(B * S, -1), xf, g1, sh2, sc2, w_out[l])
        off, gate = peer_route(h2, w_query[l], sub_keys[l])
        w = peer_u(off, h2, gate, pack_table(u_tab[l]))
        xf = peer_v(off, w, x1, g2, pack_table(v_tab[l]))
    return xf.reshape(B, S, D)
```

```python
import functools
import math
import jax
import jax.numpy as jnp
from jax import lax
import numpy as np
from jax.experimental import pallas as pl
from jax.experimental.pallas import tpu as pltpu

D_MODEL = 1024
BATCH = 8
SEQ = 4096
DEPTH = 1

EPS = 1e-6
A_HEADS = 8
A_LAT = 128
A_VDIM = 64
IDX_HEADS = 4
IDX_DIM = 64
IDX_TOPK_MAX = 256
Q_BLOCK = 128
B_HEADS = 4
B_DIM = 128
CONV_W = 4
CHUNK = 64
PEER_HEADS = 8
PEER_NKEYS = 128
PEER_QDIM = 256
PEER_TOPK = 16
PEER_BLOCK = 128
N_EXPERTS = PEER_NKEYS * PEER_NKEYS

A_WIDTH = A_HEADS * A_VDIM
B_WIDTH = B_HEADS * B_DIM
MIX_WIDTH = A_WIDTH + B_WIDTH
IN_SPLITS = (A_HEADS * A_LAT, A_LAT, IDX_HEADS * IDX_DIM, IDX_DIM, IDX_HEADS,
             B_WIDTH, B_WIDTH, B_WIDTH, B_WIDTH, B_HEADS, B_HEADS)
IN_WIDTH = sum(IN_SPLITS)


f32 = jnp.float32
bf16 = jnp.bfloat16
INT_MIN = -2 ** 31
SUM_LANE = 2
_PROJ_VMEM_LIMIT = 48 * 1024 * 1024


def _rms_modulate(x, shift, scale):
    xn = x * lax.rsqrt(jnp.mean(x * x, -1, keepdims=True) + EPS)
    return xn * (1.0 + scale) + shift


def _adaln_kernel(c_ref, w_ref, b_ref, o_ref):
    c = c_ref[...]
    s = (c * jax.nn.sigmoid(c)).astype(bf16)
    o_ref[...] = jnp.dot(s, w_ref[...].astype(bf16), preferred_element_type=f32) + b_ref[...]


def adaln(c, w, b, *, tn=512):
    B, D = c.shape
    N = w.shape[1]
    return pl.pallas_call(
        _adaln_kernel,
        grid=(N // tn,),
        in_specs=[pl.BlockSpec((B, D), lambda j: (0, 0)),
                  pl.BlockSpec((D, tn), lambda j: (0, j)),
                  pl.BlockSpec((1, tn), lambda j: (0, j))],
        out_specs=pl.BlockSpec((B, tn), lambda j: (0, j)),
        out_shape=jax.ShapeDtypeStruct((B, N), f32),
        name="adaln",
    )(c, w, b.reshape(1, N))


SMALL_IK = (0, IDX_DIM)
SMALL_IW = (IDX_DIM, IDX_DIM + IDX_HEADS)
SMALL_BA = (IDX_DIM + IDX_HEADS, IDX_DIM + IDX_HEADS + B_HEADS)
SMALL_BB = (IDX_DIM + IDX_HEADS + B_HEADS, IDX_DIM + IDX_HEADS + 2 * B_HEADS)


def _inproj_kernel(x_ref, sh_ref, sc_ref, wq_ref, wkv_ref, wiq_ref, wsm_ref, wb_ref,
                   oq_ref, okv_ref, oiq_ref, osm_ref, ob_ref):
    h = _rms_modulate(x_ref[...], sh_ref[0], sc_ref[0]).astype(bf16)
    for w_ref, o_ref in ((wq_ref, oq_ref), (wkv_ref, okv_ref), (wiq_ref, oiq_ref),
                         (wsm_ref, osm_ref), (wb_ref, ob_ref)):
        o_ref[...] = jnp.dot(h, w_ref[...], preferred_element_type=f32)


def in_projection(x, shift, scale, w_in, *, tm=512):
    N, D = x.shape
    per_batch = N // shift.shape[0]
    o = [int(v) for v in np.cumsum((0,) + IN_SPLITS)]
    wb = w_in.astype(bf16)
    w_q, w_kv, w_iq = wb[:, o[0]:o[1]], wb[:, o[1]:o[2]], wb[:, o[2]:o[3]]
    w_small = jnp.concatenate([wb[:, o[3]:o[5]], wb[:, o[9]:o[11]],
                               jnp.zeros((D, 128 - SMALL_BB[1]), bf16)], axis=1)
    w_b = wb[:, o[5]:o[9]]
    weights = (w_q, w_kv, w_iq, w_small, w_b)
    mod = pl.BlockSpec((1, 1, D), lambda i: (i * tm // per_batch, 0, 0))
    return pl.pallas_call(
        _inproj_kernel,
        grid=(N // tm,),
        in_specs=[pl.BlockSpec((tm, D), lambda i: (i, 0)), mod, mod]
                 + [pl.BlockSpec(w.shape, lambda i: (0, 0), pipeline_mode=pl.Buffered(1)) for w in weights],
        out_specs=[pl.BlockSpec((tm, w.shape[1]), lambda i: (i, 0)) for w in weights],
        out_shape=[jax.ShapeDtypeStruct((N, w.shape[1]), f32) for w in weights],
        compiler_params=pltpu.CompilerParams(dimension_semantics=("arbitrary",),
                                             vmem_limit_bytes=_PROJ_VMEM_LIMIT),
        name="in_projection",
    )(x, shift[:, None, :], scale[:, None, :], *weights)


def _outproj_kernel(ya_ref, yb_ref, x_ref, g1_ref, sh_ref, sc_ref, wa_ref, wb_ref, x1_ref, h2_ref):
    mix = (jnp.dot(ya_ref[...].astype(bf16), wa_ref[...], preferred_element_type=f32)
           + jnp.dot(yb_ref[...].astype(bf16), wb_ref[...], preferred_element_type=f32))
    x1 = x_ref[...] + g1_ref[0] * mix
    x1_ref[...] = x1
    h2_ref[...] = _rms_modulate(x1, sh_ref[0], sc_ref[0])


def out_projection(ya, yb, x, gate, shift, scale, w_out, *, tm=512):
    N, D = x.shape
    per_batch = N // gate.shape[0]
    wb = w_out.astype(bf16)
    w_a, w_b = wb[:A_WIDTH], wb[A_WIDTH:]
    mod = pl.BlockSpec((1, 1, D), lambda i: (i * tm // per_batch, 0, 0))
    row = pl.BlockSpec((tm, D), lambda i: (i, 0))
    half = pl.BlockSpec((tm, A_WIDTH), lambda i: (i, 0))
    return pl.pallas_call(
        _outproj_kernel,
        grid=(N // tm,),
        in_specs=[half, half, row, mod, mod, mod,
                  pl.BlockSpec(w_a.shape, lambda i: (0, 0)), pl.BlockSpec(w_b.shape, lambda i: (0, 0))],
        out_specs=[row, row],
        out_shape=[jax.ShapeDtypeStruct((N, D), f32), jax.ShapeDtypeStruct((N, D), f32)],
        compiler_params=pltpu.CompilerParams(dimension_semantics=("arbitrary",),
                                             vmem_limit_bytes=_PROJ_VMEM_LIMIT),
        name="out_projection",
    )(ya, yb, x, gate[:, None, :], shift[:, None, :], scale[:, None, :], w_a, w_b)


def _dsa_kernel(q_ref, kv_ref, iq_ref, ik_ref, iwt_ref, qg_ref, kg_ref, wbd_ref, tri_ref, o_ref,
                kaug_scr, ikb_scr, key_scr, khi_scr, klo_scr, qaug_scr, *head_scr, S, TQ, TK, topk):
    qi = pl.program_id(1)
    scale = A_LAT ** -0.5
    idx_scale = (IDX_DIM ** -0.5) * (IDX_HEADS ** -0.5)

    @pl.when(qi == 0)
    def _prep_keys():
        kv = kv_ref[0]
        kn = kv * lax.rsqrt(jnp.mean(kv * kv, -1, keepdims=True) + EPS) * kg_ref[...]
        pos = lax.broadcasted_iota(jnp.int32, (S, A_LAT), 0)
        lane = lax.broadcasted_iota(jnp.int32, (S, A_LAT), 1)
        hi = (pos >> 6).astype(f32)
        lo = (pos & 63).astype(f32)
        extra = jnp.where(lane == 0, hi, jnp.where(lane == 1, lo, jnp.where(lane == SUM_LANE, 1.0, 0.0)))
        kaug_scr[:, :A_LAT] = kn.astype(bf16)
        kaug_scr[:, A_LAT:] = extra.astype(bf16)
        ikb_scr[...] = ik_ref[0].astype(bf16)

    q = q_ref[0]
    lane = lax.broadcasted_iota(jnp.int32, (TQ, A_LAT), 1)
    for h in range(A_HEADS):
        qh = q[:, h * A_LAT:(h + 1) * A_LAT]
        qn = qh * lax.rsqrt(jnp.mean(qh * qh, -1, keepdims=True) + EPS) * qg_ref[...] * scale
        slope = 2.0 ** (-8.0 * (h + 1) / A_HEADS)
        extra = jnp.where(lane == 0, slope * 64.0, jnp.where(lane == 1, slope, 0.0))
        qaug_scr[h * TQ:(h + 1) * TQ, :A_LAT] = qn.astype(bf16)
        qaug_scr[h * TQ:(h + 1) * TQ, A_LAT:] = extra.astype(bf16)

    nchunks = (qi * TQ + TQ + TK - 1) // TK
    qpos = qi * TQ + lax.broadcasted_iota(jnp.int32, (TQ, 1), 0)
    qpos_row = qi * TQ + lax.broadcasted_iota(jnp.int32, (1, TQ), 1)
    colb = lax.broadcasted_iota(jnp.int32, (TQ, TK), 1)
    rowb = lax.broadcasted_iota(jnp.int32, (TK, TQ), 0)
    iq = iq_ref[0]
    iqb = [iq[:, h * IDX_DIM:(h + 1) * IDX_DIM].astype(bf16) for h in range(IDX_HEADS)]
    iwt = iwt_ref[0]

    def score_chunk(c, carry):
        off = pl.multiple_of(c * TK, TK)
        ikc = ikb_scr[pl.ds(off, TK), :]
        acc = jnp.zeros((TK, TQ), f32)
        for h in range(IDX_HEADS):
            lg = lax.dot_general(ikc, iqb[h], (((1,), (1,)), ((), ())), preferred_element_type=f32)
            acc = acc + jnp.maximum(lg, 0.0) * iwt[h:h + 1, :]
        sc = acc * idx_scale
        sc = jnp.where(rowb + off <= qpos_row, sc, -jnp.inf)
        bits = pltpu.bitcast(sc, jnp.int32)
        key = jnp.where(bits < 0, bits ^ jnp.int32(0x7FFFFFFF), bits)
        key_scr[:, pl.ds(off, TK)] = key.T
        khi_scr[pl.ds(off, TK), :] = (key >> 16).astype(jnp.int16)
        klo_scr[pl.ds(off, TK), :] = ((key & 0xFFFF) - 32768).astype(jnp.int16)
        return carry

    lax.fori_loop(0, nchunks, score_chunk, 0)

    def tile16(row):
        return jnp.broadcast_to(row, (16, TQ)).astype(jnp.int16)

    def threshold_search(n):
        rows_n = n * TK

        def count16(ref, cmp, row):
            tile = tile16(row)
            x = ref[0:rows_n, :]
            hits = [cmp(x[j * 16:(j + 1) * 16], tile).astype(jnp.int16) for j in range(rows_n // 16)]
            while len(hits) > 1:
                hits = [a + b for a, b in zip(hits[::2], hits[1::2])] + hits[len(hits) & ~1:]
            return jnp.sum(hits[0].astype(jnp.int32), axis=0, keepdims=True)

        def top_half_word(ref, wanted):
            def bit_step(i, t):
                cand = t + lax.shift_left(jnp.int32(1), jnp.int32(15) - i)
                cnt = count16(ref, lambda k, c: k >= c, cand)
                return jnp.where(cnt >= wanted, cand, t)
            return lax.fori_loop(0, 16, bit_step, jnp.full((1, TQ), -32768, jnp.int32))

        t_hi = top_half_word(khi_scr, topk)
        above = count16(khi_scr, lambda k, c: k > c, t_hi)
        t_hi_tile = tile16(t_hi)
        hi = khi_scr[0:rows_n, :]
        lo = klo_scr[0:rows_n, :]
        kept = [jnp.where(hi[j * 16:(j + 1) * 16] == t_hi_tile, lo[j * 16:(j + 1) * 16], jnp.int16(-32768))
                for j in range(rows_n // 16)]
        klo_scr[0:rows_n, :] = jnp.concatenate(kept, axis=0)
        t_lo = top_half_word(klo_scr, topk - above)
        n_gt = above + count16(klo_scr, lambda k, c: k > c, t_lo)
        return t_hi, t_lo, n_gt

    t_hi, t_lo, n_gt = lax.switch(nchunks - 1,
                                  [functools.partial(threshold_search, n) for n in range(1, S // TK + 1)])
    T = jnp.broadcast_to(t_hi * 65536 + (t_lo + 32768), (TQ, TQ)).T[:, 0:1]
    room = jnp.broadcast_to((topk - n_gt).astype(f32), (TQ, TQ)).T[:, 0:1]

    m_scr, acc_scr = head_scr[:A_HEADS], head_scr[A_HEADS:]
    for h in range(A_HEADS):
        m_scr[h][...] = jnp.full((TQ, 1), -jnp.inf, f32)
        acc_scr[h][...] = jnp.zeros((TQ, 2 * A_LAT), f32)

    def attend_chunk(c, ties_before):
        off = pl.multiple_of(c * TK, TK)
        keyc = key_scr[:, pl.ds(off, TK)]
        eq = keyc == T
        eqb = jnp.where(eq, 1.0, 0.0).astype(bf16)
        ties = ties_before
        admit = []
        for j in range(TK // 128):
            blk = slice(j * 128, (j + 1) * 128)
            cnt = jnp.dot(eqb[:, blk], tri_ref[...], preferred_element_type=f32)
            admit.append(eq[:, blk] & (cnt[:, :128] + ties < room))
            ties = ties + cnt[:, 128:]
        sel = (keyc > T) | jnp.concatenate(admit, axis=1)
        sel = sel & (colb + off <= qpos)
        kc = kaug_scr[pl.ds(off, TK), :]
        scores = lax.dot_general(qaug_scr[...], kc, (((1,), (1,)), ((), ())), preferred_element_type=f32)
        probs, alphas = [], []
        for h in range(A_HEADS):
            s = jnp.where(sel, scores[h * TQ:(h + 1) * TQ], -jnp.inf)
            m_old = m_scr[h][...]
            m_new = jnp.maximum(m_old, jnp.max(s, axis=1, keepdims=True))
            m_safe = jnp.where(m_new == -jnp.inf, 0.0, m_new)
            probs.append(jnp.exp(s - m_safe).astype(bf16))
            alphas.append(jnp.exp(m_old - m_safe))
            m_scr[h][...] = m_new
        for h in range(A_HEADS):
            acc_scr[h][...] = alphas[h] * acc_scr[h][...] + jnp.dot(probs[h], kc, preferred_element_type=f32)
        return ties

    lax.fori_loop(0, nchunks, attend_chunk, jnp.zeros((TQ, 128), f32))

    heads = []
    for h in range(A_HEADS):
        acc = acc_scr[h][...]
        row_sum = acc[:, A_LAT + SUM_LANE:A_LAT + SUM_LANE + 1]
        heads.append((acc[:, :A_LAT] / row_sum).astype(bf16))
    o = jnp.concatenate(heads, axis=1)
    o_ref[0] = jnp.dot(o, wbd_ref[...], preferred_element_type=f32)


def dsa_attention(q_lat, kv, iq, ik, iw, q_gain, k_gain, w_uv, *, TQ=128, TK=1024):
    B, S, _ = q_lat.shape
    topk = min(IDX_TOPK_MAX, S // 4)
    TK = min(TK, S)
    wbd = jnp.zeros((A_HEADS * A_LAT, A_WIDTH), f32)
    for h in range(A_HEADS):
        wbd = wbd.at[h * A_LAT:(h + 1) * A_LAT, h * A_VDIM:(h + 1) * A_VDIM].set(w_uv[h])
    blk = jnp.arange(128)
    tri = jnp.concatenate([blk[:, None] < blk[None, :], jnp.ones((128, 128), bool)], axis=1).astype(bf16)
    kern = functools.partial(_dsa_kernel, S=S, TQ=TQ, TK=TK, topk=topk)
    return pl.pallas_call(
        kern,
        grid=(B, S // TQ),
        in_specs=[
            pl.BlockSpec((1, TQ, A_HEADS * A_LAT), lambda b, i: (b, i, 0)),
            pl.BlockSpec((1, S, A_LAT), lambda b, i: (b, 0, 0)),
            pl.BlockSpec((1, TQ, IDX_HEADS * IDX_DIM), lambda b, i: (b, i, 0)),
            pl.BlockSpec((1, S, IDX_DIM), lambda b, i: (b, 0, 0)),
            pl.BlockSpec((1, IDX_HEADS, TQ), lambda b, i: (b, 0, i)),
            pl.BlockSpec((1, A_LAT), lambda b, i: (0, 0)),
            pl.BlockSpec((1, A_LAT), lambda b, i: (0, 0)),
            pl.BlockSpec((A_HEADS * A_LAT, A_WIDTH), lambda b, i: (0, 0)),
            pl.BlockSpec((128, 256), lambda b, i: (0, 0)),
        ],
        out_specs=pl.BlockSpec((1, TQ, A_WIDTH), lambda b, i: (b, i, 0)),
        out_shape=jax.ShapeDtypeStruct((B, S, A_WIDTH), f32),
        scratch_shapes=[
            pltpu.VMEM((S, 2 * A_LAT), bf16),
            pltpu.VMEM((S, IDX_DIM), bf16),
            pltpu.VMEM((TQ, S), jnp.int32),
            pltpu.VMEM((S, TQ), jnp.int16),
            pltpu.VMEM((S, TQ), jnp.int16),
            pltpu.VMEM((A_HEADS * TQ, 2 * A_LAT), bf16),
            *[pltpu.VMEM((TQ, 1), f32) for _ in range(A_HEADS)],
            *[pltpu.VMEM((TQ, 2 * A_LAT), f32) for _ in range(A_HEADS)],
        ],
        compiler_params=pltpu.CompilerParams(dimension_semantics=("arbitrary", "arbitrary")),
        name="dsa_attention",
    )(q_lat, kv, iq, ik, iw.transpose(0, 2, 1), q_gain.reshape(1, -1), k_gain.reshape(1, -1),
      wbd.astype(bf16), tri)


CONV_HALO = 8


def _split_bf16(x, terms):
    parts = []
    for _ in range(terms):
        p = x.astype(bf16)
        parts.append(p)
        x = x - p.astype(f32)
    return parts


def _dot_exact_lhs(mask_bf16, x):
    return sum(jnp.dot(mask_bf16, p, preferred_element_type=f32) for p in _split_bf16(x, 3))


def _dot_exact_rhs(x, mask_bf16):
    return sum(jnp.dot(p, mask_bf16, preferred_element_type=f32) for p in _split_bf16(x, 3))


def _dot_3pass(a, b):
    ah, al = _split_bf16(a, 2)
    bh, bl = _split_bf16(b, 2)
    return (jnp.dot(ah, bh, preferred_element_type=f32) + jnp.dot(ah, bl, preferred_element_type=f32)
            + jnp.dot(al, bh, preferred_element_type=f32))


def _softplus(x):
    return jnp.maximum(x, 0.0) + jnp.log1p(jnp.exp(-jnp.abs(x)))


def _gdn_kernel(q_ref, k_ref, v_ref, z_ref, a_ref, b_ref, at_ref, bt_ref, cw_ref, alog_ref, dtb_ref,
                gain_ref, o_ref, tail_scr, act_scr, *state_scr, SB):
    sj = pl.program_id(1)

    @pl.when(sj == 0)
    def _reset():
        tail_scr[...] = jnp.zeros(tail_scr.shape, f32)
        for st_ref in state_scr:
            st_ref[...] = jnp.zeros((B_DIM, B_DIM), f32)

    cw = cw_ref[...]
    for idx, x_ref in enumerate((q_ref, k_ref, v_ref)):
        x = x_ref[0]
        xc = jnp.concatenate([tail_scr[idx], x], axis=0)
        y = jnp.zeros((SB, B_WIDTH), f32)
        for j in range(CONV_W):
            lo = CONV_HALO - (CONV_W - 1) + j
            y = y + cw[j:j + 1, idx * B_WIDTH:(idx + 1) * B_WIDTH] * xc[lo:lo + SB]
        tail_scr[idx] = x[SB - CONV_HALO:SB]
        act_scr[idx] = y * jax.nn.sigmoid(y)

    ri = lax.broadcasted_iota(jnp.int32, (CHUNK, CHUNK), 0)
    ci = lax.broadcasted_iota(jnp.int32, (CHUNK, CHUNK), 1)
    tril = ri >= ci
    strict = ri > ci
    tril_b = jnp.where(tril, 1.0, 0.0).astype(bf16)
    triu_b = jnp.where(ri <= ci, 1.0, 0.0).astype(bf16)
    eye = jnp.where(ri == ci, 1.0, 0.0)
    nt = (((1,), (1,)), ((), ()))

    def chunk(c, carry):
        rows = pl.ds(pl.multiple_of(c * CHUNK, CHUNK), CHUNK)
        H = range(B_HEADS)
        lanes = [slice(h * B_DIM, (h + 1) * B_DIM) for h in H]
        lp = dict(preferred_element_type=f32)
        q = [act_scr[0, rows, lanes[h]] for h in H]
        k = [act_scr[1, rows, lanes[h]] for h in H]
        v = [act_scr[2, rows, lanes[h]] for h in H]
        q = [x * lax.rsqrt(jnp.sum(x * x, -1, keepdims=True) + EPS) * (B_DIM ** -0.5) for x in q]
        k = [x * lax.rsqrt(jnp.sum(x * x, -1, keepdims=True) + EPS) for x in k]
        neg_rate = [-jnp.exp(alog_ref[0:1, h:h + 1]) for h in H]
        dtb = [dtb_ref[0:1, h:h + 1] for h in H]
        g_col = [neg_rate[h] * _softplus(a_ref[0, rows, h:h + 1] + dtb[h]) for h in H]
        g_row = [neg_rate[h] * _softplus(at_ref[0, h, pl.ds(c, 1), :] + dtb[h]) for h in H]
        beta = [jax.nn.sigmoid(b_ref[0, rows, h:h + 1]) for h in H]
        Gc = [_dot_exact_lhs(tril_b, jnp.broadcast_to(g_col[h], (CHUNK, CHUNK))) for h in H]
        Gr = [_dot_exact_rhs(jnp.broadcast_to(g_row[h], (CHUNK, CHUNK)), triu_b) for h in H]
        G = [x[:, 0:1] for x in Gc]
        G_last = [x[CHUNK - 1:CHUNK, 0:1] for x in Gc]
        decay = [jnp.exp(jnp.where(tril, Gc[h] - Gr[h], -jnp.inf)) for h in H]
        kb = [k[h] * beta[h] for h in H]
        kbf = [x.astype(bf16) for x in k]
        L = [jnp.where(strict, lax.dot_general(kb[h].astype(bf16), kbf[h], nt, **lp) * decay[h], 0.0) for h in H]
        P = [eye - x for x in L]
        M = [_dot_3pass(x, x) for x in L]
        for lvl in range(5):
            P = [P[h] + _dot_3pass(P[h], M[h]) for h in H]
            if lvl < 4:
                M = [_dot_3pass(x, x) for x in M]
        Tb = [x.astype(bf16) for x in P]
        eG = [jnp.exp(x) for x in G]
        u = [jnp.dot(Tb[h], (v[h] * beta[h]).astype(bf16), **lp) for h in H]
        w = [jnp.dot(Tb[h], (kb[h] * eG[h]).astype(bf16), **lp) for h in H]
        attn = [lax.dot_general(q[h].astype(bf16), kbf[h], nt, **lp) * decay[h] for h in H]
        q_dec = [q[h] * eG[h] for h in H]
        k_dec = [k[h] * jnp.exp(G_last[h] - G[h]) for h in H]
        st = [state_scr[h][...] for h in H]
        stb = [x.astype(bf16) for x in st]
        v_new = [u[h] - jnp.dot(w[h].astype(bf16), stb[h], **lp) for h in H]
        vnb = [x.astype(bf16) for x in v_new]
        o = [jnp.dot(q_dec[h].astype(bf16), stb[h], **lp) + jnp.dot(attn[h].astype(bf16), vnb[h], **lp) for h in H]
        for h in H:
            state_scr[h][...] = st[h] * jnp.exp(G_last[h]) + lax.dot_general(
                k_dec[h].astype(bf16), vnb[h], (((0,), (0,)), ((), ())), **lp)
        for h in H:
            zh = z_ref[0, rows, lanes[h]]
            on = o[h] * lax.rsqrt(jnp.mean(o[h] * o[h], -1, keepdims=True) + EPS) * gain_ref[...]
            o_ref[0, rows, lanes[h]] = on * (zh * jax.nn.sigmoid(zh))
        return carry

    lax.fori_loop(0, SB // CHUNK, chunk, 0)


def gated_deltanet(qkvz, a, b, conv_w, a_log, dt_bias, norm_gain, *, SB=512):
    B, S, _ = qkvz.shape
    SB = min(SB, S)
    nch = S // CHUNK
    at = a.transpose(0, 2, 1).reshape(B, B_HEADS, nch, CHUNK)
    bt = b.transpose(0, 2, 1).reshape(B, B_HEADS, nch, CHUNK)
    wide = pl.BlockSpec((1, SB, B_WIDTH), lambda bi, j: (bi, j, 0))
    part = [pl.BlockSpec((1, SB, B_WIDTH), functools.partial(lambda bi, j, n: (bi, j, n), n=n))
            for n in range(4)]
    narrow = pl.BlockSpec((1, SB, B_HEADS), lambda bi, j: (bi, j, 0))
    rowwise = pl.BlockSpec((1, B_HEADS, SB // CHUNK, CHUNK), lambda bi, j: (bi, 0, j, 0))
    return pl.pallas_call(
        functools.partial(_gdn_kernel, SB=SB),
        grid=(B, S // SB),
        in_specs=[*part, narrow, narrow, rowwise, rowwise,
                  pl.BlockSpec((CONV_W, 3 * B_WIDTH), lambda bi, j: (0, 0)),
                  pl.BlockSpec((1, B_HEADS), lambda bi, j: (0, 0)),
                  pl.BlockSpec((1, B_HEADS), lambda bi, j: (0, 0)),
                  pl.BlockSpec((1, B_DIM), lambda bi, j: (0, 0))],
        out_specs=wide,
        out_shape=jax.ShapeDtypeStruct((B, S, B_WIDTH), f32),
        scratch_shapes=[pltpu.VMEM((3, CONV_HALO, B_WIDTH), f32),
                        pltpu.VMEM((3, SB, B_WIDTH), f32),
                        *[pltpu.VMEM((B_DIM, B_DIM), f32) for _ in range(B_HEADS)]],
        compiler_params=pltpu.CompilerParams(dimension_semantics=("arbitrary", "arbitrary")),
        name="gated_deltanet",
    )(qkvz, qkvz, qkvz, qkvz, a, b, at, bt, conv_w,
      a_log.reshape(1, -1), dt_bias.reshape(1, -1), norm_gain.reshape(1, -1))


PEER_SLOTS = PEER_HEADS * PEER_TOPK
PEER_HALF = PEER_QDIM // 2
ROW_WORDS = D_MODEL // 2
ROW_SUBL = ROW_WORDS // 128
STAGE_STRIDE = PEER_SLOTS + 8


def _top16_rows(s, iota_rows, fill, vals_scr, idx_scr, payload=None, pay_scr=None):
    for r in range(PEER_TOPK):
        m = jnp.max(s, axis=0, keepdims=True)
        first = jnp.min(jnp.where(s == m, iota_rows, fill), axis=0, keepdims=True)
        taken = iota_rows == first
        vals_scr[r:r + 1, :] = m
        if payload is None:
            idx_scr[r:r + 1, :] = first.astype(jnp.int32)
        else:
            pay_scr[r:r + 1, :] = jnp.max(jnp.where(taken, payload, -1.0), axis=0,
                                          keepdims=True).astype(jnp.int32)
        s = jnp.where(taken, -jnp.inf, s)


def _peer_route_kernel(h_ref, wq_ref, sk_ref, off_ref, gate_ref,
                       v1_scr, i1_scr, v2_scr, i2_scr, cv_scr, ce_scr, eid_scr, gate_scr, *, T):
    q = jnp.dot(h_ref[...].astype(bf16), wq_ref[...], preferred_element_type=f32).astype(bf16)
    gate_scr[...] = jnp.zeros(gate_scr.shape, f32)
    kiota = lax.broadcasted_iota(jnp.int32, (PEER_NKEYS, T), 0).astype(f32)
    sub8 = lax.broadcasted_iota(jnp.int32, (8, T), 0).astype(f32)
    for p in range(PEER_HEADS):
        for half, (vs, is_) in enumerate(((v1_scr, i1_scr), (v2_scr, i2_scr))):
            g = 2 * p + half
            s = lax.dot_general(sk_ref[g], q[:, g * PEER_HALF:(g + 1) * PEER_HALF],
                                (((1,), (1,)), ((), ())), preferred_element_type=f32)
            _top16_rows(s, kiota, PEER_NKEYS, vs, is_)
        v1 = v1_scr[...]
        v2 = v2_scr[...]
        e1 = (i1_scr[...] * PEER_NKEYS).astype(f32)
        e2 = i2_scr[...].astype(f32)
        cv = [v1[0:1] + v2[0:8], v1[0:1] + v2[8:16]]
        ce = [e1[0:1] + e2[0:8], e1[0:1] + e2[8:16]]
        cf = [sub8, sub8 + 8]
        for a in range(1, 8):
            cv.append(v1[a:a + 1] + v2[0:8])
            ce.append(e1[a:a + 1] + e2[0:8])
            cf.append(sub8 + a * PEER_TOPK)
        cv.append(v1[8:16] + v2[0:1])
        ce.append(e1[8:16] + e2[0:1])
        cf.append((sub8 + 8) * PEER_TOPK)
        cand = jnp.concatenate(cv, axis=0)
        cexp = jnp.concatenate(ce, axis=0)
        cflat = jnp.concatenate(cf, axis=0)
        _top16_rows(cand, cflat, PEER_TOPK * PEER_TOPK, cv_scr, None, payload=cexp, pay_scr=ce_scr)
        top = cv_scr[...]
        ex = jnp.exp(top - top[0:1])
        gate_scr[pl.ds(2 * p * PEER_TOPK + 1, PEER_TOPK, stride=2), :] = ex / jnp.sum(ex, axis=0, keepdims=True)
        eid_scr[p * PEER_TOPK:(p + 1) * PEER_TOPK, :] = ce_scr[...] * ROW_SUBL
    off_ref[...] = eid_scr[...]
    gate_ref[...] = gate_scr[...].T


def peer_route(h, w_query, sub_keys, *, T=128):
    N, D = h.shape
    sk = sub_keys.reshape(PEER_HEADS * 2, PEER_NKEYS, PEER_HALF).astype(bf16)
    kern = functools.partial(_peer_route_kernel, T=T)
    return pl.pallas_call(
        kern,
        grid=(N // T,),
        in_specs=[
            pl.BlockSpec((T, D), lambda i: (i, 0)),
            pl.BlockSpec((D, PEER_HEADS * PEER_QDIM), lambda i: (0, 0)),
            pl.BlockSpec((PEER_HEADS * 2, PEER_NKEYS, PEER_HALF), lambda i: (0, 0, 0)),
        ],
        out_specs=[pl.BlockSpec((PEER_SLOTS, T), lambda i: (0, i)),
                   pl.BlockSpec((T, 2 * PEER_SLOTS), lambda i: (i, 0))],
        out_shape=[jax.ShapeDtypeStruct((PEER_SLOTS, N), jnp.int32),
                   jax.ShapeDtypeStruct((N, 2 * PEER_SLOTS), f32)],
        scratch_shapes=[
            pltpu.VMEM((PEER_TOPK, T), f32), pltpu.VMEM((PEER_TOPK, T), jnp.int32),
            pltpu.VMEM((PEER_TOPK, T), f32), pltpu.VMEM((PEER_TOPK, T), jnp.int32),
            pltpu.VMEM((PEER_TOPK, T), f32), pltpu.VMEM((PEER_TOPK, T), jnp.int32),
            pltpu.VMEM((PEER_SLOTS, T), jnp.int32), pltpu.VMEM((2 * PEER_SLOTS, T), f32),
        ],
        compiler_params=pltpu.CompilerParams(dimension_semantics=("arbitrary",)),
        name="peer_route",
    )(h, w_query.astype(bf16), sk)


def pack_table(tab):
    bits = lax.bitcast_convert_type(tab.astype(bf16), jnp.uint16).astype(jnp.uint32)
    words = (bits[:, :ROW_WORDS] << 16) | bits[:, ROW_WORDS:]
    return lax.bitcast_convert_type(words, jnp.int32).reshape(tab.shape[0] * ROW_SUBL, 128)


GROUP = 8


def _fetch_slots(off_ref, t0, tab_ref, stage, first, last):
    for k in range(first, last):
        offs = off_ref.at[k, pl.ds(t0, GROUP)]
        for i in range(GROUP):
            off = pl.multiple_of(offs[i], ROW_SUBL)
            stage[i, pl.ds(k, ROW_SUBL, stride=STAGE_STRIDE), :] = tab_ref[pl.ds(off, ROW_SUBL), :]


def _staged_rows(stage, i):
    chunks = [stage[i, c * STAGE_STRIDE:c * STAGE_STRIDE + PEER_SLOTS, :] for c in range(ROW_SUBL)]
    return pltpu.bitcast(jnp.concatenate(chunks, axis=1), bf16)


def _pipelined_groups(off_ref, tab_ref, stage_a, stage_b, start, step, finish, T):
    ngroups = T // GROUP
    per_step = PEER_SLOTS // GROUP
    _fetch_slots(off_ref, 0, tab_ref, stage_a, 0, PEER_SLOTS)

    def run(g, stage, g_next, stage_next):
        t_next = pl.multiple_of(g_next * GROUP, GROUP)
        carry = start(g)
        for i in range(GROUP):
            carry = step(i, carry, stage)
            _fetch_slots(off_ref, t_next, tab_ref, stage_next, i * per_step, (i + 1) * per_step)
        finish(g, carry)

    def pair(j, carry):
        g0 = 2 * j
        run(g0, stage_a, g0 + 1, stage_b)
        run(g0 + 1, stage_b, jnp.minimum(g0 + 2, ngroups - 1), stage_a)
        return carry

    lax.fori_loop(0, ngroups // 2, pair, 0)


def _peer_u_kernel(off_ref, h_ref, gate_ref, tab_ref, w_ref, stage_a, stage_b, *, T):
    row16 = lax.broadcasted_iota(jnp.int32, (2 * GROUP, 2 * PEER_SLOTS), 0) & (GROUP - 1)
    odd = (lax.broadcasted_iota(jnp.int32, (GROUP, 2 * PEER_SLOTS), 1) & 1) == 1

    def start(g):
        h8 = h_ref[pl.ds(pl.multiple_of(g * GROUP, GROUP), GROUP), :]
        h16 = jnp.concatenate([h8[:, :ROW_WORDS], h8[:, ROW_WORDS:]], axis=0).astype(bf16)
        return h16, jnp.zeros((2 * GROUP, 2 * PEER_SLOTS), f32)

    def step(i, carry, stage):
        h16, acc = carry
        res = lax.dot_general(h16, _staged_rows(stage, i), (((1,), (1,)), ((), ())),
                              preferred_element_type=f32)
        return h16, jnp.where(row16 == i, res, acc)

    def finish(g, carry):
        t0 = pl.multiple_of(g * GROUP, GROUP)
        acc = carry[1]
        dots = acc[0:GROUP] + pltpu.roll(acc[GROUP:], 1, axis=1)
        gelu = 0.5 * dots * (1.0 + lax.erf(dots * (2.0 ** -0.5)))
        w_ref[pl.ds(t0, GROUP), :] = jnp.where(odd, gelu * gate_ref[pl.ds(t0, GROUP), :], 0.0)

    _pipelined_groups(off_ref, tab_ref, stage_a, stage_b, start, step, finish, T)


def _peer_v_kernel(off_ref, w_ref, x_ref, g2_ref, tab_ref, o_ref, stage_a, stage_b, *, T):
    row16 = lax.broadcasted_iota(jnp.int32, (2 * GROUP, ROW_WORDS), 0) & (GROUP - 1)

    def start(g):
        w8 = w_ref[pl.ds(pl.multiple_of(g * GROUP, GROUP), GROUP), :]
        w16 = jnp.concatenate([w8, pltpu.roll(w8, 2 * PEER_SLOTS - 1, axis=1)], axis=0).astype(bf16)
        return w16, jnp.zeros((2 * GROUP, ROW_WORDS), f32)

    def step(i, carry, stage):
        w16, acc = carry
        res = jnp.dot(w16, _staged_rows(stage, i), preferred_element_type=f32)
        return w16, jnp.where(row16 == i, res, acc)

    def finish(g, carry):
        t0 = pl.multiple_of(g * GROUP, GROUP)
        acc = carry[1]
        y = jnp.concatenate([acc[0:GROUP], acc[GROUP:]], axis=1)
        o_ref[pl.ds(t0, GROUP), :] = x_ref[pl.ds(t0, GROUP), :] + g2_ref[0] * y

    _pipelined_groups(off_ref, tab_ref, stage_a, stage_b, start, step, finish, T)


_TABLE_VMEM_LIMIT = 56 * 1024 * 1024


def _table_spec():
    return pl.BlockSpec((N_EXPERTS * ROW_SUBL, 128), lambda i: (0, 0), pipeline_mode=pl.Buffered(1))


def _stage_scratch():
    return [pltpu.VMEM((GROUP, ROW_SUBL * STAGE_STRIDE, 128), jnp.int32) for _ in range(2)]


def _offset_spec(T):
    return pl.BlockSpec((PEER_SLOTS, T), lambda i: (0, i), memory_space=pltpu.SMEM)


def peer_u(off, h, gate, u_pk, *, T=256):
    N, D = h.shape
    return pl.pallas_call(
        functools.partial(_peer_u_kernel, T=T),
        grid=(N // T,),
        in_specs=[
            _offset_spec(T),
            pl.BlockSpec((T, D), lambda i: (i, 0)),
            pl.BlockSpec((T, 2 * PEER_SLOTS), lambda i: (i, 0)),
            _table_spec(),
        ],
        out_specs=pl.BlockSpec((T, 2 * PEER_SLOTS), lambda i: (i, 0)),
        out_shape=jax.ShapeDtypeStruct((N, 2 * PEER_SLOTS), f32),
        scratch_shapes=_stage_scratch(),
        compiler_params=pltpu.CompilerParams(dimension_semantics=("arbitrary",),
                                             vmem_limit_bytes=_TABLE_VMEM_LIMIT),
        name="peer_u",
    )(off, h, gate, u_pk)


def peer_v(off, w, x, g2, v_pk, *, T=256):
    N, D = x.shape
    per_batch = N // g2.shape[0]
    return pl.pallas_call(
        functools.partial(_peer_v_kernel, T=T),
        grid=(N // T,),
        in_specs=[
            _offset_spec(T),
            pl.BlockSpec((T, 2 * PEER_SLOTS), lambda i: (i, 0)),
            pl.BlockSpec((T, D), lambda i: (i, 0)),
            pl.BlockSpec((1, 1, D), lambda i: (i * T // per_batch, 0, 0)),
            _table_spec(),
        ],
        out_specs=pl.BlockSpec((T, D), lambda i: (i, 0)),
        out_shape=jax.ShapeDtypeStruct((N, D), f32),
        scratch_shapes=_stage_scratch(),
        compiler_params=pltpu.CompilerParams(dimension_semantics=("arbitrary",),
                                             vmem_limit_bytes=_TABLE_VMEM_LIMIT),
        name="peer_v",
    )(off, w, x, g2.reshape(g2.shape[0], 1, D), v_pk)


def kernel(x, c, w_ada, b_ada, w_in, q_gain, k_gain, w_uv, conv_w, a_log, dt_bias,
           gdn_gain, w_out, w_query, sub_keys, u_tab, v_tab):
    B, S, D = x.shape
    xf = x.reshape(B * S, D)
    for l in range(DEPTH):
        sh1, sc1, g1, sh2, sc2, g2 = jnp.split(adaln(c, w_ada[l], b_ada[l]), 6, axis=-1)
        aq, akv, iq, small, qkvz = in_projection(xf, sh1, sc1, w_in[l])
        small = small.reshape(B, S, -1)
        ik, iw, ba, bb = (small[..., lo:hi] for lo, hi in (SMALL_IK, SMALL_IW, SMALL_BA, SMALL_BB))
        ya = dsa_attention(aq.reshape(B, S, -1), akv.reshape(B, S, -1), iq.reshape(B, S, -1), ik, iw,
                           q_gain[l], k_gain[l], w_uv[l])
        yb = gated_deltanet(qkvz.reshape(B, S, -1), ba, bb, conv_w[l], a_log[l], dt_bias[l], gdn_gain[l])
        x1, h2 = out_projection(ya.reshape(B * S, -1), yb.reshape(B * S, -1), xf, g1, sh2, sc2, w_out[l])
        off, gate = peer_route(h2, w_query[l], sub_keys[l])
        w = peer_u(off, h2, gate, pack_table(u_tab[l]))
        xf = peer_v(off, w, x1, g2, pack_table(v_tab[l]))
    return xf.reshape(B, S, D)
```

```python
import functools
import math
import jax
import jax.numpy as jnp
from jax import lax
import numpy as np
from jax.experimental import pallas as pl
from jax.experimental.pallas import tpu as pltpu

D_MODEL = 1024
BATCH = 8
SEQ = 4096
DEPTH = 1

EPS = 1e-6
A_HEADS = 8
A_LAT = 128
A_VDIM = 64
IDX_HEADS = 4
IDX_DIM = 64
IDX_TOPK_MAX = 256
Q_BLOCK = 128
B_HEADS = 4
B_DIM = 128
CONV_W = 4
CHUNK = 64
PEER_HEADS = 8
PEER_NKEYS = 128
PEER_QDIM = 256
PEER_TOPK = 16
PEER_BLOCK = 128
N_EXPERTS = PEER_NKEYS * PEER_NKEYS

A_WIDTH = A_HEADS * A_VDIM
B_WIDTH = B_HEADS * B_DIM
MIX_WIDTH = A_WIDTH + B_WIDTH
IN_SPLITS = (A_HEADS * A_LAT, A_LAT, IDX_HEADS * IDX_DIM, IDX_DIM, IDX_HEADS,
             B_WIDTH, B_WIDTH, B_WIDTH, B_WIDTH, B_HEADS, B_HEADS)
IN_WIDTH = sum(IN_SPLITS)


f32 = jnp.float32
bf16 = jnp.bfloat16
INT_MIN = -2 ** 31
SUM_LANE = 2
_PROJ_VMEM_LIMIT = 48 * 1024 * 1024


def _rms_modulate(x, shift, scale):
    xn = x * lax.rsqrt(jnp.mean(x * x, -1, keepdims=True) + EPS)
    return xn * (1.0 + scale) + shift


def _adaln_kernel(c_ref, w_ref, b_ref, o_ref):
    c = c_ref[...]
    s = (c * jax.nn.sigmoid(c)).astype(bf16)
    o_ref[...] = jnp.dot(s, w_ref[...].astype(bf16), preferred_element_type=f32) + b_ref[...]


def adaln(c, w, b, *, tn=512):
    B, D = c.shape
    N = w.shape[1]
    return pl.pallas_call(
        _adaln_kernel,
        grid=(N // tn,),
        in_specs=[pl.BlockSpec((B, D), lambda j: (0, 0)),
                  pl.BlockSpec((D, tn), lambda j: (0, j)),
                  pl.BlockSpec((1, tn), lambda j: (0, j))],
        out_specs=pl.BlockSpec((B, tn), lambda j: (0, j)),
        out_shape=jax.ShapeDtypeStruct((B, N), f32),
        name="adaln",
    )(c, w, b.reshape(1, N))


SMALL_IK = (0, IDX_DIM)
SMALL_IW = (IDX_DIM, IDX_DIM + IDX_HEADS)
SMALL_BA = (IDX_DIM + IDX_HEADS, IDX_DIM + IDX_HEADS + B_HEADS)
SMALL_BB = (IDX_DIM + IDX_HEADS + B_HEADS, IDX_DIM + IDX_HEADS + 2 * B_HEADS)


def _inproj_kernel(x_ref, sh_ref, sc_ref, wq_ref, wkv_ref, wiq_ref, wsm_ref, wb_ref,
                   oq_ref, okv_ref, oiq_ref, osm_ref, ob_ref):
    h = _rms_modulate(x_ref[...], sh_ref[0], sc_ref[0]).astype(bf16)
    for w_ref, o_ref in ((wq_ref, oq_ref), (wkv_ref, okv_ref), (wiq_ref, oiq_ref),
                         (wsm_ref, osm_ref), (wb_ref, ob_ref)):
        o_ref[...] = jnp.dot(h, w_ref[...], preferred_element_type=f32)


def in_projection(x, shift, scale, w_in, *, tm=512):
    N, D = x.shape
    per_batch = N // shift.shape[0]
    o = [int(v) for v in np.cumsum((0,) + IN_SPLITS)]
    wb = w_in.astype(bf16)
    w_q, w_kv, w_iq = wb[:, o[0]:o[1]], wb[:, o[1]:o[2]], wb[:, o[2]:o[3]]
    w_small = jnp.concatenate([wb[:, o[3]:o[5]], wb[:, o[9]:o[11]],
                               jnp.zeros((D, 128 - SMALL_BB[1]), bf16)], axis=1)
    w_b = wb[:, o[5]:o[9]]
    weights = (w_q, w_kv, w_iq, w_small, w_b)
    mod = pl.BlockSpec((1, 1, D), lambda i: (i * tm // per_batch, 0, 0))
    return pl.pallas_call(
        _inproj_kernel,
        grid=(N // tm,),
        in_specs=[pl.BlockSpec((tm, D), lambda i: (i, 0)), mod, mod]
                 + [pl.BlockSpec(w.shape, lambda i: (0, 0), pipeline_mode=pl.Buffered(1)) for w in weights],
        out_specs=[pl.BlockSpec((tm, w.shape[1]), lambda i: (i, 0)) for w in weights],
        out_shape=[jax.ShapeDtypeStruct((N, w.shape[1]), f32) for w in weights],
        compiler_params=pltpu.CompilerParams(dimension_semantics=("arbitrary",),
                                             vmem_limit_bytes=_PROJ_VMEM_LIMIT),
        name="in_projection",
    )(x, shift[:, None, :], scale[:, None, :], *weights)


def _outproj_kernel(ya_ref, yb_ref, x_ref, g1_ref, sh_ref, sc_ref, wa_ref, wb_ref, x1_ref, h2_ref):
    mix = (jnp.dot(ya_ref[...].astype(bf16), wa_ref[...], preferred_element_type=f32)
           + jnp.dot(yb_ref[...].astype(bf16), wb_ref[...], preferred_element_type=f32))
    x1 = x_ref[...] + g1_ref[0] * mix
    x1_ref[...] = x1
    h2_ref[...] = _rms_modulate(x1, sh_ref[0], sc_ref[0])


def out_projection(ya, yb, x, gate, shift, scale, w_out, *, tm=512):
    N, D = x.shape
    per_batch = N // gate.shape[0]
    wb = w_out.astype(bf16)
    w_a, w_b = wb[:A_WIDTH], wb[A_WIDTH:]
    mod = pl.BlockSpec((1, 1, D), lambda i: (i * tm // per_batch, 0, 0))
    row = pl.BlockSpec((tm, D), lambda i: (i, 0))
    half = pl.BlockSpec((tm, A_WIDTH), lambda i: (i, 0))
    return pl.pallas_call(
        _outproj_kernel,
        grid=(N // tm,),
        in_specs=[half, half, row, mod, mod, mod,
                  pl.BlockSpec(w_a.shape, lambda i: (0, 0)), pl.BlockSpec(w_b.shape, lambda i: (0, 0))],
        out_specs=[row, row],
        out_shape=[jax.ShapeDtypeStruct((N, D), f32), jax.ShapeDtypeStruct((N, D), f32)],
        compiler_params=pltpu.CompilerParams(dimension_semantics=("arbitrary",),
                                             vmem_limit_bytes=_PROJ_VMEM_LIMIT),
        name="out_projection",
    )(ya, yb, x, gate[:, None, :], shift[:, None, :], scale[:, None, :], w_a, w_b)


def _dsa_kernel(q_ref, kv_ref, iq_ref, ik_ref, iwt_ref, qg_ref, kg_ref, wbd_ref, tri_ref, o_ref,
                kaug_scr, ikb_scr, key_scr, khi_scr, klo_scr, qaug_scr, *head_scr, S, TQ, TK, topk):
    qi = pl.program_id(1)
    scale = A_LAT ** -0.5
    idx_scale = (IDX_DIM ** -0.5) * (IDX_HEADS ** -0.5)

    @pl.when(qi == 0)
    def _prep_keys():
        kv = kv_ref[0]
        kn = kv * lax.rsqrt(jnp.mean(kv * kv, -1, keepdims=True) + EPS) * kg_ref[...]
        pos = lax.broadcasted_iota(jnp.int32, (S, A_LAT), 0)
        lane = lax.broadcasted_iota(jnp.int32, (S, A_LAT), 1)
        hi = (pos >> 6).astype(f32)
        lo = (pos & 63).astype(f32)
        extra = jnp.where(lane == 0, hi, jnp.where(lane == 1, lo, jnp.where(lane == SUM_LANE, 1.0, 0.0)))
        kaug_scr[:, :A_LAT] = kn.astype(bf16)
        kaug_scr[:, A_LAT:] = extra.astype(bf16)
        ikb_scr[...] = ik_ref[0].astype(bf16)

    q = q_ref[0]
    lane = lax.broadcasted_iota(jnp.int32, (TQ, A_LAT), 1)
    for h in range(A_HEADS):
        qh = q[:, h * A_LAT:(h + 1) * A_LAT]
        qn = qh * lax.rsqrt(jnp.mean(qh * qh, -1, keepdims=True) + EPS) * qg_ref[...] * scale
        slope = 2.0 ** (-8.0 * (h + 1) / A_HEADS)
        extra = jnp.where(lane == 0, slope * 64.0, jnp.where(lane == 1, slope, 0.0))
        qaug_scr[h * TQ:(h + 1) * TQ, :A_LAT] = qn.astype(bf16)
        qaug_scr[h * TQ:(h + 1) * TQ, A_LAT:] = extra.astype(bf16)

    nchunks = (qi * TQ + TQ + TK - 1) // TK
    qpos = qi * TQ + lax.broadcasted_iota(jnp.int32, (TQ, 1), 0)
    qpos_row = qi * TQ + lax.broadcasted_iota(jnp.int32, (1, TQ), 1)
    colb = lax.broadcasted_iota(jnp.int32, (TQ, TK), 1)
    rowb = lax.broadcasted_iota(jnp.int32, (TK, TQ), 0)
    iq = iq_ref[0]
    iqb = [iq[:, h * IDX_DIM:(h + 1) * IDX_DIM].astype(bf16) for h in range(IDX_HEADS)]
    iwt = iwt_ref[0]

    def score_chunk(c, carry):
        off = pl.multiple_of(c * TK, TK)
        ikc = ikb_scr[pl.ds(off, TK), :]
        acc = jnp.zeros((TK, TQ), f32)
        for h in range(IDX_HEADS):
            lg = lax.dot_general(ikc, iqb[h], (((1,), (1,)), ((), ())), preferred_element_type=f32)
            acc = acc + jnp.maximum(lg, 0.0) * iwt[h:h + 1, :]
        sc = acc * idx_scale
        sc = jnp.where(rowb + off <= qpos_row, sc, -jnp.inf)
        bits = pltpu.bitcast(sc, jnp.int32)
        key = jnp.where(bits < 0, bits ^ jnp.int32(0x7FFFFFFF), bits)
        key_scr[:, pl.ds(off, TK)] = key.T
        khi_scr[pl.ds(off, TK), :] = (key >> 16).astype(jnp.int16)
        klo_scr[pl.ds(off, TK), :] = ((key & 0xFFFF) - 32768).astype(jnp.int16)
        return carry

    lax.fori_loop(0, nchunks, score_chunk, 0)

    def tile16(row):
        return jnp.broadcast_to(row, (16, TQ)).astype(jnp.int16)

    def threshold_search(n):
        rows_n = n * TK

        def count16(ref, cmp, row):
            tile = tile16(row)
            x = ref[0:rows_n, :]
            hits = [cmp(x[j * 16:(j + 1) * 16], tile).astype(jnp.int16) for j in range(rows_n // 16)]
            while len(hits) > 1:
                hits = [a + b for a, b in zip(hits[::2], hits[1::2])] + hits[len(hits) & ~1:]
            return jnp.sum(hits[0].astype(jnp.int32), axis=0, keepdims=True)

        def top_half_word(ref, wanted):
            def bit_step(i, t):
                cand = t + lax.shift_left(jnp.int32(1), jnp.int32(15) - i)
                cnt = count16(ref, lambda k, c: k >= c, cand)
                return jnp.where(cnt >= wanted, cand, t)
            return lax.fori_loop(0, 16, bit_step, jnp.full((1, TQ), -32768, jnp.int32))

        t_hi = top_half_word(khi_scr, topk)
        above = count16(khi_scr, lambda k, c: k > c, t_hi)
        t_hi_tile = tile16(t_hi)
        hi = khi_scr[0:rows_n, :]
        lo = klo_scr[0:rows_n, :]
        kept = [jnp.where(hi[j * 16:(j + 1) * 16] == t_hi_tile, lo[j * 16:(j + 1) * 16], jnp.int16(-32768))
                for j in range(rows_n // 16)]
        klo_scr[0:rows_n, :] = jnp.concatenate(kept, axis=0)
        t_lo = top_half_word(klo_scr, topk - above)
        n_gt = above + count16(klo_scr, lambda k, c: k > c, t_lo)
        return t_hi, t_lo, n_gt

    t_hi, t_lo, n_gt = lax.switch(nchunks - 1,
                                  [functools.partial(threshold_search, n) for n in range(1, S // TK + 1)])
    T = jnp.broadcast_to(t_hi * 65536 + (t_lo + 32768), (TQ, TQ)).T[:, 0:1]
    room = jnp.broadcast_to((topk - n_gt).astype(f32), (TQ, TQ)).T[:, 0:1]

    m_scr, acc_scr = head_scr[:A_HEADS], head_scr[A_HEADS:]
    for h in range(A_HEADS):
        m_scr[h][...] = jnp.full((TQ, 1), -jnp.inf, f32)
        acc_scr[h][...] = jnp.zeros((TQ, 2 * A_LAT), f32)

    def attend_chunk(c, ties_before):
        off = pl.multiple_of(c * TK, TK)
        keyc = key_scr[:, pl.ds(off, TK)]
        eq = keyc == T
        eqb = jnp.where(eq, 1.0, 0.0).astype(bf16)
        ties = ties_before
        admit = []
        for j in range(TK // 128):
            blk = slice(j * 128, (j + 1) * 128)
            cnt = jnp.dot(eqb[:, blk], tri_ref[...], preferred_element_type=f32)
            admit.append(eq[:, blk] & (cnt[:, :128] + ties < room))
            ties = ties + cnt[:, 128:]
        sel = (keyc > T) | jnp.concatenate(admit, axis=1)
        sel = sel & (colb + off <= qpos)
        kc = kaug_scr[pl.ds(off, TK), :]
        scores = lax.dot_general(qaug_scr[...], kc, (((1,), (1,)), ((), ())), preferred_element_type=f32)
        probs, alphas = [], []
        for h in range(A_HEADS):
            s = jnp.where(sel, scores[h * TQ:(h + 1) * TQ], -jnp.inf)
            m_old = m_scr[h][...]
            m_new = jnp.maximum(m_old, jnp.max(s, axis=1, keepdims=True))
            m_safe = jnp.where(m_new == -jnp.inf, 0.0, m_new)
            probs.append(jnp.exp(s - m_safe).astype(bf16))
            alphas.append(jnp.exp(m_old - m_safe))
            m_scr[h][...] = m_new
        for h in range(A_HEADS):
            acc_scr[h][...] = alphas[h] * acc_scr[h][...] + jnp.dot(probs[h], kc, preferred_element_type=f32)
        return ties

    lax.fori_loop(0, nchunks, attend_chunk, jnp.zeros((TQ, 128), f32))

    heads = []
    for h in range(A_HEADS):
        acc = acc_scr[h][...]
        row_sum = acc[:, A_LAT + SUM_LANE:A_LAT + SUM_LANE + 1]
        heads.append((acc[:, :A_LAT] / row_sum).astype(bf16))
    o = jnp.concatenate(heads, axis=1)
    o_ref[0] = jnp.dot(o, wbd_ref[...], preferred_element_type=f32)


def dsa_attention(q_lat, kv, iq, ik, iw, q_gain, k_gain, w_uv, *, TQ=128, TK=1024):
    B, S, _ = q_lat.shape
    topk = min(IDX_TOPK_MAX, S // 4)
    TK = min(TK, S)
    wbd = jnp.zeros((A_HEADS * A_LAT, A_WIDTH), f32)
    for h in range(A_HEADS):
        wbd = wbd.at[h * A_LAT:(h + 1) * A_LAT, h * A_VDIM:(h + 1) * A_VDIM].set(w_uv[h])
    blk = jnp.arange(128)
    tri = jnp.concatenate([blk[:, None] < blk[None, :], jnp.ones((128, 128), bool)], axis=1).astype(bf16)
    kern = functools.partial(_dsa_kernel, S=S, TQ=TQ, TK=TK, topk=topk)
    return pl.pallas_call(
        kern,
        grid=(B, S // TQ),
        in_specs=[
            pl.BlockSpec((1, TQ, A_HEADS * A_LAT), lambda b, i: (b, i, 0)),
            pl.BlockSpec((1, S, A_LAT), lambda b, i: (b, 0, 0)),
            pl.BlockSpec((1, TQ, IDX_HEADS * IDX_DIM), lambda b, i: (b, i, 0)),
            pl.BlockSpec((1, S, IDX_DIM), lambda b, i: (b, 0, 0)),
            pl.BlockSpec((1, IDX_HEADS, TQ), lambda b, i: (b, 0, i)),
            pl.BlockSpec((1, A_LAT), lambda b, i: (0, 0)),
            pl.BlockSpec((1, A_LAT), lambda b, i: (0, 0)),
            pl.BlockSpec((A_HEADS * A_LAT, A_WIDTH), lambda b, i: (0, 0)),
            pl.BlockSpec((128, 256), lambda b, i: (0, 0)),
        ],
        out_specs=pl.BlockSpec((1, TQ, A_WIDTH), lambda b, i: (b, i, 0)),
        out_shape=jax.ShapeDtypeStruct((B, S, A_WIDTH), f32),
        scratch_shapes=[
            pltpu.VMEM((S, 2 * A_LAT), bf16),
            pltpu.VMEM((S, IDX_DIM), bf16),
            pltpu.VMEM((TQ, S), jnp.int32),
            pltpu.VMEM((S, TQ), jnp.int16),
            pltpu.VMEM((S, TQ), jnp.int16),
            pltpu.VMEM((A_HEADS * TQ, 2 * A_LAT), bf16),
            *[pltpu.VMEM((TQ, 1), f32) for _ in range(A_HEADS)],
            *[pltpu.VMEM((TQ, 2 * A_LAT), f32) for _ in range(A_HEADS)],
        ],
        compiler_params=pltpu.CompilerParams(dimension_semantics=("arbitrary", "arbitrary")),
        name="dsa_attention",
    )(q_lat, kv, iq, ik, iw.transpose(0, 2, 1), q_gain.reshape(1, -1), k_gain.reshape(1, -1),
      wbd.astype(bf16), tri)


CONV_HALO = 8


def _split_bf16(x, terms):
    parts = []
    for _ in range(terms):
        p = x.astype(bf16)
        parts.append(p)
        x = x - p.astype(f32)
    return parts


def _dot_exact_lhs(mask_bf16, x):
    return sum(jnp.dot(mask_bf16, p, preferred_element_type=f32) for p in _split_bf16(x, 3))


def _dot_exact_rhs(x, mask_bf16):
    return sum(jnp.dot(p, mask_bf16, preferred_element_type=f32) for p in _split_bf16(x, 3))


def _dot_3pass(a, b):
    ah, al = _split_bf16(a, 2)
    bh, bl = _split_bf16(b, 2)
    return (jnp.dot(ah, bh, preferred_element_type=f32) + jnp.dot(ah, bl, preferred_element_type=f32)
            + jnp.dot(al, bh, preferred_element_type=f32))


def _softplus(x):
    return jnp.maximum(x, 0.0) + jnp.log1p(jnp.exp(-jnp.abs(x)))


def _gdn_kernel(q_ref, k_ref, v_ref, z_ref, a_ref, b_ref, at_ref, bt_ref, cw_ref, alog_ref, dtb_ref,
                gain_ref, o_ref, tail_scr, act_scr, *state_scr, SB):
    sj = pl.program_id(1)

    @pl.when(sj == 0)
    def _reset():
        tail_scr[...] = jnp.zeros(tail_scr.shape, f32)
        for st_ref in state_scr:
            st_ref[...] = jnp.zeros((B_DIM, B_DIM), f32)

    cw = cw_ref[...]
    for idx, x_ref in enumerate((q_ref, k_ref, v_ref)):
        x = x_ref[0]
        xc = jnp.concatenate([tail_scr[idx], x], axis=0)
        y = jnp.zeros((SB, B_WIDTH), f32)
        for j in range(CONV_W):
            lo = CONV_HALO - (CONV_W - 1) + j
            y = y + cw[j:j + 1, idx * B_WIDTH:(idx + 1) * B_WIDTH] * xc[lo:lo + SB]
        tail_scr[idx] = x[SB - CONV_HALO:SB]
        act_scr[idx] = y * jax.nn.sigmoid(y)

    ri = lax.broadcasted_iota(jnp.int32, (CHUNK, CHUNK), 0)
    ci = lax.broadcasted_iota(jnp.int32, (CHUNK, CHUNK), 1)
    tril = ri >= ci
    strict = ri > ci
    tril_b = jnp.where(tril, 1.0, 0.0).astype(bf16)
    triu_b = jnp.where(ri <= ci, 1.0, 0.0).astype(bf16)
    eye = jnp.where(ri == ci, 1.0, 0.0)
    nt = (((1,), (1,)), ((), ()))

    def chunk_pair(j, carry):
        H = range(B_HEADS)
        lanes = [slice(h * B_DIM, (h + 1) * B_DIM) for h in H]
        lp = dict(preferred_element_type=f32)
        chunks = (2 * j, 2 * j + 1)
        rows = [pl.ds(pl.multiple_of(c * CHUNK, CHUNK), CHUNK) for c in chunks]
        I = [(ci, h) for ci in range(2) for h in H]
        q = [act_scr[0, rows[ci], lanes[h]] for ci, h in I]
        k = [act_scr[1, rows[ci], lanes[h]] for ci, h in I]
        v = [act_scr[2, rows[ci], lanes[h]] for ci, h in I]
        q = [x * lax.rsqrt(jnp.sum(x * x, -1, keepdims=True) + EPS) * (B_DIM ** -0.5) for x in q]
        k = [x * lax.rsqrt(jnp.sum(x * x, -1, keepdims=True) + EPS) for x in k]
        neg_rate = [-jnp.exp(alog_ref[0:1, h:h + 1]) for h in H]
        dtb = [dtb_ref[0:1, h:h + 1] for h in H]
        g_col = [neg_rate[h] * _softplus(a_ref[0, rows[ci], h:h + 1] + dtb[h]) for ci, h in I]
        g_row = [neg_rate[h] * _softplus(at_ref[0, h, pl.ds(chunks[ci], 1), :] + dtb[h]) for ci, h in I]
        beta = [jax.nn.sigmoid(b_ref[0, rows[ci], h:h + 1]) for ci, h in I]
        N = range(len(I))
        Gc = [_dot_exact_lhs(tril_b, jnp.broadcast_to(g_col[n], (CHUNK, CHUNK))) for n in N]
        Gr = [_dot_exact_rhs(jnp.broadcast_to(g_row[n], (CHUNK, CHUNK)), triu_b) for n in N]
        G = [x[:, 0:1] for x in Gc]
        G_last = [x[CHUNK - 1:CHUNK, 0:1] for x in Gc]
        decay = [jnp.exp(jnp.where(tril, Gc[n] - Gr[n], -jnp.inf)) for n in N]
        kb = [k[n] * beta[n] for n in N]
        kbf = [x.astype(bf16) for x in k]
        L = [jnp.where(strict, lax.dot_general(kb[n].astype(bf16), kbf[n], nt, **lp) * decay[n], 0.0) for n in N]
        P = [eye - x for x in L]
        M = [_dot_3pass(x, x) for x in L]
        for lvl in range(5):
            P = [P[n] + _dot_3pass(P[n], M[n]) for n in N]
            if lvl < 4:
                M = [_dot_3pass(x, x) for x in M]
        Tb = [x.astype(bf16) for x in P]
        eG = [jnp.exp(x) for x in G]
        u = [jnp.dot(Tb[n], (v[n] * beta[n]).astype(bf16), **lp) for n in N]
        w = [jnp.dot(Tb[n], (kb[n] * eG[n]).astype(bf16), **lp) for n in N]
        attn = [lax.dot_general(q[n].astype(bf16), kbf[n], nt, **lp) * decay[n] for n in N]
        q_dec = [q[n] * eG[n] for n in N]
        k_dec = [k[n] * jnp.exp(G_last[n] - G[n]) for n in N]
        st = [state_scr[h][...] for h in H]
        for ci in range(2):
            ns = [ci * B_HEADS + h for h in H]
            stb = [x.astype(bf16) for x in st]
            v_new = [u[n] - jnp.dot(w[n].astype(bf16), stb[h], **lp) for h, n in zip(H, ns)]
            vnb = [x.astype(bf16) for x in v_new]
            o = [jnp.dot(q_dec[n].astype(bf16), stb[h], **lp) + jnp.dot(attn[n].astype(bf16), vnb[h], **lp)
                 for h, n in zip(H, ns)]
            st = [st[h] * jnp.exp(G_last[n]) + lax.dot_general(
                k_dec[n].astype(bf16), vnb[h], (((0,), (0,)), ((), ())), **lp) for h, n in zip(H, ns)]
            for h in H:
                zh = z_ref[0, rows[ci], lanes[h]]
                on = o[h] * lax.rsqrt(jnp.mean(o[h] * o[h], -1, keepdims=True) + EPS) * gain_ref[...]
                o_ref[0, rows[ci], lanes[h]] = on * (zh * jax.nn.sigmoid(zh))
        for h in H:
            state_scr[h][...] = st[h]
        return carry

    lax.fori_loop(0, SB // (2 * CHUNK), chunk_pair, 0)


def gated_deltanet(qkvz, a, b, conv_w, a_log, dt_bias, norm_gain, *, SB=512):
    B, S, _ = qkvz.shape
    SB = min(SB, S)
    nch = S // CHUNK
    at = a.transpose(0, 2, 1).reshape(B, B_HEADS, nch, CHUNK)
    bt = b.transpose(0, 2, 1).reshape(B, B_HEADS, nch, CHUNK)
    wide = pl.BlockSpec((1, SB, B_WIDTH), lambda bi, j: (bi, j, 0))
    part = [pl.BlockSpec((1, SB, B_WIDTH), functools.partial(lambda bi, j, n: (bi, j, n), n=n))
            for n in range(4)]
    narrow = pl.BlockSpec((1, SB, B_HEADS), lambda bi, j: (bi, j, 0))
    rowwise = pl.BlockSpec((1, B_HEADS, SB // CHUNK, CHUNK), lambda bi, j: (bi, 0, j, 0))
    return pl.pallas_call(
        functools.partial(_gdn_kernel, SB=SB),
        grid=(B, S // SB),
        in_specs=[*part, narrow, narrow, rowwise, rowwise,
                  pl.BlockSpec((CONV_W, 3 * B_WIDTH), lambda bi, j: (0, 0)),
                  pl.BlockSpec((1, B_HEADS), lambda bi, j: (0, 0)),
                  pl.BlockSpec((1, B_HEADS), lambda bi, j: (0, 0)),
                  pl.BlockSpec((1, B_DIM), lambda bi, j: (0, 0))],
        out_specs=wide,
        out_shape=jax.ShapeDtypeStruct((B, S, B_WIDTH), f32),
        scratch_shapes=[pltpu.VMEM((3, CONV_HALO, B_WIDTH), f32),
                        pltpu.VMEM((3, SB, B_WIDTH), f32),
                        *[pltpu.VMEM((B_DIM, B_DIM), f32) for _ in range(B_HEADS)]],
        compiler_params=pltpu.CompilerParams(dimension_semantics=("arbitrary", "arbitrary")),
        name="gated_deltanet",
    )(qkvz, qkvz, qkvz, qkvz, a, b, at, bt, conv_w,
      a_log.reshape(1, -1), dt_bias.reshape(1, -1), norm_gain.reshape(1, -1))


PEER_SLOTS = PEER_HEADS * PEER_TOPK
PEER_HALF = PEER_QDIM // 2
ROW_WORDS = D_MODEL // 2
ROW_SUBL = ROW_WORDS // 128
STAGE_STRIDE = PEER_SLOTS + 8


def _top16_rows(s, iota_rows, fill, vals_scr, idx_scr, payload=None, pay_scr=None):
    for r in range(PEER_TOPK):
        m = jnp.max(s, axis=0, keepdims=True)
        first = jnp.min(jnp.where(s == m, iota_rows, fill), axis=0, keepdims=True)
        taken = iota_rows == first
        vals_scr[r:r + 1, :] = m
        if payload is None:
            idx_scr[r:r + 1, :] = first.astype(jnp.int32)
        else:
            pay_scr[r:r + 1, :] = jnp.max(jnp.where(taken, payload, -1.0), axis=0,
                                          keepdims=True).astype(jnp.int32)
        s = jnp.where(taken, -jnp.inf, s)


def _peer_route_kernel(h_ref, wq_ref, sk_ref, off_ref, gate_ref,
                       v1_scr, i1_scr, v2_scr, i2_scr, cv_scr, ce_scr, eid_scr, gate_scr, *, T):
    q = jnp.dot(h_ref[...].astype(bf16), wq_ref[...], preferred_element_type=f32).astype(bf16)
    gate_scr[...] = jnp.zeros(gate_scr.shape, f32)
    kiota = lax.broadcasted_iota(jnp.int32, (PEER_NKEYS, T), 0).astype(f32)
    sub8 = lax.broadcasted_iota(jnp.int32, (8, T), 0).astype(f32)
    for p in range(PEER_HEADS):
        for half, (vs, is_) in enumerate(((v1_scr, i1_scr), (v2_scr, i2_scr))):
            g = 2 * p + half
            s = lax.dot_general(sk_ref[g], q[:, g * PEER_HALF:(g + 1) * PEER_HALF],
                                (((1,), (1,)), ((), ())), preferred_element_type=f32)
            _top16_rows(s, kiota, PEER_NKEYS, vs, is_)
        v1 = v1_scr[...]
        v2 = v2_scr[...]
        e1 = (i1_scr[...] * PEER_NKEYS).astype(f32)
        e2 = i2_scr[...].astype(f32)
        cv = [v1[0:1] + v2[0:8], v1[0:1] + v2[8:16]]
        ce = [e1[0:1] + e2[0:8], e1[0:1] + e2[8:16]]
        cf = [sub8, sub8 + 8]
        for a in range(1, 8):
            cv.append(v1[a:a + 1] + v2[0:8])
            ce.append(e1[a:a + 1] + e2[0:8])
            cf.append(sub8 + a * PEER_TOPK)
        cv.append(v1[8:16] + v2[0:1])
        ce.append(e1[8:16] + e2[0:1])
        cf.append((sub8 + 8) * PEER_TOPK)
        cand = jnp.concatenate(cv, axis=0)
        cexp = jnp.concatenate(ce, axis=0)
        cflat = jnp.concatenate(cf, axis=0)
        _top16_rows(cand, cflat, PEER_TOPK * PEER_TOPK, cv_scr, None, payload=cexp, pay_scr=ce_scr)
        top = cv_scr[...]
        ex = jnp.exp(top - top[0:1])
        gate_scr[pl.ds(2 * p * PEER_TOPK + 1, PEER_TOPK, stride=2), :] = ex / jnp.sum(ex, axis=0, keepdims=True)
        eid_scr[p * PEER_TOPK:(p + 1) * PEER_TOPK, :] = ce_scr[...] * ROW_SUBL
    off_ref[...] = eid_scr[...]
    gate_ref[...] = gate_scr[...].T


def peer_route(h, w_query, sub_keys, *, T=128):
    N, D = h.shape
    sk = sub_keys.reshape(PEER_HEADS * 2, PEER_NKEYS, PEER_HALF).astype(bf16)
    kern = functools.partial(_peer_route_kernel, T=T)
    return pl.pallas_call(
        kern,
        grid=(N // T,),
        in_specs=[
            pl.BlockSpec((T, D), lambda i: (i, 0)),
            pl.BlockSpec((D, PEER_HEADS * PEER_QDIM), lambda i: (0, 0)),
            pl.BlockSpec((PEER_HEADS * 2, PEER_NKEYS, PEER_HALF), lambda i: (0, 0, 0)),
        ],
        out_specs=[pl.BlockSpec((PEER_SLOTS, T), lambda i: (0, i)),
                   pl.BlockSpec((T, 2 * PEER_SLOTS), lambda i: (i, 0))],
        out_shape=[jax.ShapeDtypeStruct((PEER_SLOTS, N), jnp.int32),
                   jax.ShapeDtypeStruct((N, 2 * PEER_SLOTS), f32)],
        scratch_shapes=[
            pltpu.VMEM((PEER_TOPK, T), f32), pltpu.VMEM((PEER_TOPK, T), jnp.int32),
            pltpu.VMEM((PEER_TOPK, T), f32), pltpu.VMEM((PEER_TOPK, T), jnp.int32),
            pltpu.VMEM((PEER_TOPK, T), f32), pltpu.VMEM((PEER_TOPK, T), jnp.int32),
            pltpu.VMEM((PEER_SLOTS, T), jnp.int32), pltpu.VMEM((2 * PEER_SLOTS, T), f32),
        ],
        compiler_params=pltpu.CompilerParams(dimension_semantics=("arbitrary",)),
        name="peer_route",
    )(h, w_query.astype(bf16), sk)


def pack_table(tab):
    bits = lax.bitcast_convert_type(tab.astype(bf16), jnp.uint16).astype(jnp.uint32)
    words = (bits[:, :ROW_WORDS] << 16) | bits[:, ROW_WORDS:]
    return lax.bitcast_convert_type(words, jnp.int32).reshape(tab.shape[0] * ROW_SUBL, 128)


GROUP = 8


def _fetch_slots(off_ref, t0, tab_ref, stage, first, last):
    for k in range(first, last):
        offs = off_ref.at[k, pl.ds(t0, GROUP)]
        for i in range(GROUP):
            off = pl.multiple_of(offs[i], ROW_SUBL)
            stage[i, pl.ds(k, ROW_SUBL, stride=STAGE_STRIDE), :] = tab_ref[pl.ds(off, ROW_SUBL), :]


def _staged_rows(stage, i):
    chunks = [stage[i, c * STAGE_STRIDE:c * STAGE_STRIDE + PEER_SLOTS, :] for c in range(ROW_SUBL)]
    return pltpu.bitcast(jnp.concatenate(chunks, axis=1), bf16)


def _pipelined_groups(off_ref, tab_ref, stage_a, stage_b, start, step, finish, T):
    ngroups = T // GROUP
    per_step = PEER_SLOTS // GROUP
    _fetch_slots(off_ref, 0, tab_ref, stage_a, 0, PEER_SLOTS)

    def run(g, stage, g_next, stage_next):
        t_next = pl.multiple_of(g_next * GROUP, GROUP)
        carry = start(g)
        for i in range(GROUP):
            carry = step(i, carry, stage)
            _fetch_slots(off_ref, t_next, tab_ref, stage_next, i * per_step, (i + 1) * per_step)
        finish(g, carry)

    def pair(j, carry):
        g0 = 2 * j
        run(g0, stage_a, g0 + 1, stage_b)
        run(g0 + 1, stage_b, jnp.minimum(g0 + 2, ngroups - 1), stage_a)
        return carry

    lax.fori_loop(0, ngroups // 2, pair, 0)


def _peer_u_kernel(off_ref, h_ref, gate_ref, tab_ref, w_ref, stage_a, stage_b, *, T):
    row16 = lax.broadcasted_iota(jnp.int32, (2 * GROUP, 2 * PEER_SLOTS), 0) & (GROUP - 1)
    odd = (lax.broadcasted_iota(jnp.int32, (GROUP, 2 * PEER_SLOTS), 1) & 1) == 1

    def start(g):
        h8 = h_ref[pl.ds(pl.multiple_of(g * GROUP, GROUP), GROUP), :]
        h16 = jnp.concatenate([h8[:, :ROW_WORDS], h8[:, ROW_WORDS:]], axis=0).astype(bf16)
        return h16, jnp.zeros((2 * GROUP, 2 * PEER_SLOTS), f32)

    def step(i, carry, stage):
        h16, acc = carry
        res = lax.dot_general(h16, _staged_rows(stage, i), (((1,), (1,)), ((), ())),
                              preferred_element_type=f32)
        return h16, jnp.where(row16 == i, res, acc)

    def finish(g, carry):
        t0 = pl.multiple_of(g * GROUP, GROUP)
        acc = carry[1]
        dots = acc[0:GROUP] + pltpu.roll(acc[GROUP:], 1, axis=1)
        gelu = 0.5 * dots * (1.0 + lax.erf(dots * (2.0 ** -0.5)))
        w_ref[pl.ds(t0, GROUP), :] = jnp.where(odd, gelu * gate_ref[pl.ds(t0, GROUP), :], 0.0)

    _pipelined_groups(off_ref, tab_ref, stage_a, stage_b, start, step, finish, T)


def _peer_v_kernel(off_ref, w_ref, x_ref, g2_ref, tab_ref, o_ref, stage_a, stage_b, *, T):
    row16 = lax.broadcasted_iota(jnp.int32, (2 * GROUP, ROW_WORDS), 0) & (GROUP - 1)

    def start(g):
        w8 = w_ref[pl.ds(pl.multiple_of(g * GROUP, GROUP), GROUP), :]
        w16 = jnp.concatenate([w8, pltpu.roll(w8, 2 * PEER_SLOTS - 1, axis=1)], axis=0).astype(bf16)
        return w16, jnp.zeros((2 * GROUP, ROW_WORDS), f32)

    def step(i, carry, stage):
        w16, acc = carry
        res = jnp.dot(w16, _staged_rows(stage, i), preferred_element_type=f32)
        return w16, jnp.where(row16 == i, res, acc)

    def finish(g, carry):
        t0 = pl.multiple_of(g * GROUP, GROUP)
        acc = carry[1]
        y = jnp.concatenate([acc[0:GROUP], acc[GROUP:]], axis=1)
        o_ref[pl.ds(t0, GROUP), :] = x_ref[pl.ds(t0, GROUP), :] + g2_ref[0] * y

    _pipelined_groups(off_ref, tab_ref, stage_a, stage_b, start, step, finish, T)


_TABLE_VMEM_LIMIT = 56 * 1024 * 1024


def _table_spec():
    return pl.BlockSpec((N_EXPERTS * ROW_SUBL, 128), lambda i: (0, 0), pipeline_mode=pl.Buffered(1))


def _stage_scratch():
    return [pltpu.VMEM((GROUP, ROW_SUBL * STAGE_STRIDE, 128), jnp.int32) for _ in range(2)]


def _offset_spec(T):
    return pl.BlockSpec((PEER_SLOTS, T), lambda i: (0, i), memory_space=pltpu.SMEM)


def peer_u(off, h, gate, u_pk, *, T=256):
    N, D = h.shape
    return pl.pallas_call(
        functools.partial(_peer_u_kernel, T=T),
        grid=(N // T,),
        in_specs=[
            _offset_spec(T),
            pl.BlockSpec((T, D), lambda i: (i, 0)),
            pl.BlockSpec((T, 2 * PEER_SLOTS), lambda i: (i, 0)),
            _table_spec(),
        ],
        out_specs=pl.BlockSpec((T, 2 * PEER_SLOTS), lambda i: (i, 0)),
        out_shape=jax.ShapeDtypeStruct((N, 2 * PEER_SLOTS), f32),
        scratch_shapes=_stage_scratch(),
        compiler_params=pltpu.CompilerParams(dimension_semantics=("arbitrary",),
                                             vmem_limit_bytes=_TABLE_VMEM_LIMIT),
        name="peer_u",
    )(off, h, gate, u_pk)


def peer_v(off, w, x, g2, v_pk, *, T=256):
    N, D = x.shape
    per_batch = N // g2.shape[0]
    return pl.pallas_call(
        functools.partial(_peer_v_kernel, T=T),
        grid=(N // T,),
        in_specs=[
            _offset_spec(T),
            pl.BlockSpec((T, 2 * PEER_SLOTS), lambda i: (i, 0)),
            pl.BlockSpec((T, D), lambda i: (i, 0)),
            pl.BlockSpec((1, 1, D), lambda i: (i * T // per_batch, 0, 0)),
            _table_spec(),
        ],
        out_specs=pl.BlockSpec((T, D), lambda i: (i, 0)),
        out_shape=jax.ShapeDtypeStruct((N, D), f32),
        scratch_shapes=_stage_scratch(),
        compiler_params=pltpu.CompilerParams(dimension_semantics=("arbitrary",),
                                             vmem_limit_bytes=_TABLE_VMEM_LIMIT),
        name="peer_v",
    )(off, w, x, g2.reshape(g2.shape[0], 1, D), v_pk)


def kernel(x, c, w_ada, b_ada, w_in, q_gain, k_gain, w_uv, conv_w, a_log, dt_bias,
           gdn_gain, w_out, w_query, sub_keys, u_tab, v_tab):
    B, S, D = x.shape
    xf = x.reshape(B * S, D)
    for l in range(DEPTH):
        sh1, sc1, g1, sh2, sc2, g2 = jnp.split(adaln(c, w_ada[l], b_ada[l]), 6, axis=-1)
        aq, akv, iq, small, qkvz = in_projection(xf, sh1, sc1, w_in[l])
        small = small.reshape(B, S, -1)
        ik, iw, ba, bb = (small[..., lo:hi] for lo, hi in (SMALL_IK, SMALL_IW, SMALL_BA, SMALL_BB))
        ya = dsa_attention(aq.reshape(B, S, -1), akv.reshape(B, S, -1), iq.reshape(B, S, -1), ik, iw,
                           q_gain[l], k_gain[l], w_uv[l])
        yb = gated_deltanet(qkvz.reshape(B, S, -1), ba, bb, conv_w[l], a_log[l], dt_bias[l], gdn_gain[l])
        x1, h2 = out_projection(ya.reshape(B * S, -1), yb.reshape(B * S, -1), xf, g1, sh2, sc2, w_out[l])
        off, gate = peer_route(h2, w_query[l], sub_keys[l])
        w = peer_u(off, h2, gate, pack_table(u_tab[l]))
        xf = peer_v(off, w, x1, g2, pack_table(v_tab[l]))
    return xf.reshape(B, S, D)
```

```python
import functools
import math
import jax
import jax.numpy as jnp
from jax import lax
import numpy as np
from jax.experimental import pallas as pl
from jax.experimental.pallas import tpu as pltpu

D_MODEL = 1024
BATCH = 8
SEQ = 4096
DEPTH = 1

EPS = 1e-6
A_HEADS = 8
A_LAT = 128
A_VDIM = 64
IDX_HEADS = 4
IDX_DIM = 64
IDX_TOPK_MAX = 256
Q_BLOCK = 128
B_HEADS = 4
B_DIM = 128
CONV_W = 4
CHUNK = 64
PEER_HEADS = 8
PEER_NKEYS = 128
PEER_QDIM = 256
PEER_TOPK = 16
PEER_BLOCK = 128
N_EXPERTS = PEER_NKEYS * PEER_NKEYS

A_WIDTH = A_HEADS * A_VDIM
B_WIDTH = B_HEADS * B_DIM
MIX_WIDTH = A_WIDTH + B_WIDTH
IN_SPLITS = (A_HEADS * A_LAT, A_LAT, IDX_HEADS * IDX_DIM, IDX_DIM, IDX_HEADS,
             B_WIDTH, B_WIDTH, B_WIDTH, B_WIDTH, B_HEADS, B_HEADS)
IN_WIDTH = sum(IN_SPLITS)


f32 = jnp.float32
bf16 = jnp.bfloat16
INT_MIN = -2 ** 31
SUM_LANE = 2
_PROJ_VMEM_LIMIT = 48 * 1024 * 1024


def _rms_modulate(x, shift, scale):
    xn = x * lax.rsqrt(jnp.mean(x * x, -1, keepdims=True) + EPS)
    return xn * (1.0 + scale) + shift


def _adaln_kernel(c_ref, w_ref, b_ref, o_ref):
    c = c_ref[...]
    s = (c * jax.nn.sigmoid(c)).astype(bf16)
    o_ref[...] = jnp.dot(s, w_ref[...].astype(bf16), preferred_element_type=f32) + b_ref[...]


def adaln(c, w, b, *, tn=512):
    B, D = c.shape
    N = w.shape[1]
    return pl.pallas_call(
        _adaln_kernel,
        grid=(N // tn,),
        in_specs=[pl.BlockSpec((B, D), lambda j: (0, 0)),
                  pl.BlockSpec((D, tn), lambda j: (0, j)),
                  pl.BlockSpec((1, tn), lambda j: (0, j))],
        out_specs=pl.BlockSpec((B, tn), lambda j: (0, j)),
        out_shape=jax.ShapeDtypeStruct((B, N), f32),
        name="adaln",
    )(c, w, b.reshape(1, N))


SMALL_IK = (0, IDX_DIM)
SMALL_IW = (IDX_DIM, IDX_DIM + IDX_HEADS)
SMALL_BA = (IDX_DIM + IDX_HEADS, IDX_DIM + IDX_HEADS + B_HEADS)
SMALL_BB = (IDX_DIM + IDX_HEADS + B_HEADS, IDX_DIM + IDX_HEADS + 2 * B_HEADS)


def _inproj_kernel(x_ref, sh_ref, sc_ref, wq_ref, wkv_ref, wiq_ref, wsm_ref, wb_ref,
                   oq_ref, okv_ref, oiq_ref, osm_ref, ob_ref):
    h = _rms_modulate(x_ref[...], sh_ref[0], sc_ref[0]).astype(bf16)
    for w_ref, o_ref in ((wq_ref, oq_ref), (wkv_ref, okv_ref), (wiq_ref, oiq_ref),
                         (wsm_ref, osm_ref), (wb_ref, ob_ref)):
        o_ref[...] = jnp.dot(h, w_ref[...], preferred_element_type=f32)


def in_projection(x, shift, scale, w_in, *, tm=512):
    N, D = x.shape
    per_batch = N // shift.shape[0]
    o = [int(v) for v in np.cumsum((0,) + IN_SPLITS)]
    wb = w_in.astype(bf16)
    w_q, w_kv, w_iq = wb[:, o[0]:o[1]], wb[:, o[1]:o[2]], wb[:, o[2]:o[3]]
    w_small = jnp.concatenate([wb[:, o[3]:o[5]], wb[:, o[9]:o[11]],
                               jnp.zeros((D, 128 - SMALL_BB[1]), bf16)], axis=1)
    w_b = wb[:, o[5]:o[9]]
    weights = (w_q, w_kv, w_iq, w_small, w_b)
    mod = pl.BlockSpec((1, 1, D), lambda i: (i * tm // per_batch, 0, 0))
    return pl.pallas_call(
        _inproj_kernel,
        grid=(N // tm,),
        in_specs=[pl.BlockSpec((tm, D), lambda i: (i, 0)), mod, mod]
                 + [pl.BlockSpec(w.shape, lambda i: (0, 0), pipeline_mode=pl.Buffered(1)) for w in weights],
        out_specs=[pl.BlockSpec((tm, w.shape[1]), lambda i: (i, 0)) for w in weights],
        out_shape=[jax.ShapeDtypeStruct((N, w.shape[1]), f32) for w in weights],
        compiler_params=pltpu.CompilerParams(dimension_semantics=("arbitrary",),
                                             vmem_limit_bytes=_PROJ_VMEM_LIMIT),
        name="in_projection",
    )(x, shift[:, None, :], scale[:, None, :], *weights)


def _outproj_kernel(ya_ref, yb_ref, x_ref, g1_ref, sh_ref, sc_ref, wa_ref, wb_ref, x1_ref, h2_ref):
    mix = (jnp.dot(ya_ref[...].astype(bf16), wa_ref[...], preferred_element_type=f32)
           + jnp.dot(yb_ref[...].astype(bf16), wb_ref[...], preferred_element_type=f32))
    x1 = x_ref[...] + g1_ref[0] * mix
    x1_ref[...] = x1
    h2_ref[...] = _rms_modulate(x1, sh_ref[0], sc_ref[0])


def out_projection(ya, yb, x, gate, shift, scale, w_out, *, tm=512):
    N, D = x.shape
    per_batch = N // gate.shape[0]
    wb = w_out.astype(bf16)
    w_a, w_b = wb[:A_WIDTH], wb[A_WIDTH:]
    mod = pl.BlockSpec((1, 1, D), lambda i: (i * tm // per_batch, 0, 0))
    row = pl.BlockSpec((tm, D), lambda i: (i, 0))
    half = pl.BlockSpec((tm, A_WIDTH), lambda i: (i, 0))
    return pl.pallas_call(
        _outproj_kernel,
        grid=(N // tm,),
        in_specs=[half, half, row, mod, mod, mod,
                  pl.BlockSpec(w_a.shape, lambda i: (0, 0)), pl.BlockSpec(w_b.shape, lambda i: (0, 0))],
        out_specs=[row, row],
        out_shape=[jax.ShapeDtypeStruct((N, D), f32), jax.ShapeDtypeStruct((N, D), f32)],
        compiler_params=pltpu.CompilerParams(dimension_semantics=("arbitrary",),
                                             vmem_limit_bytes=_PROJ_VMEM_LIMIT),
        name="out_projection",
    )(ya, yb, x, gate[:, None, :], shift[:, None, :], scale[:, None, :], w_a, w_b)


def _dsa_kernel(q_ref, kv_ref, iq_ref, ik_ref, iwt_ref, qg_ref, kg_ref, wbd_ref, tri_ref, o_ref,
                kaug_scr, ikb_scr, key_scr, khi_scr, klo_scr, qaug_scr, *head_scr, S, TQ, TK, topk):
    qi = pl.program_id(1)
    scale = A_LAT ** -0.5
    idx_scale = (IDX_DIM ** -0.5) * (IDX_HEADS ** -0.5)

    @pl.when(qi == 0)
    def _prep_keys():
        kv = kv_ref[0]
        kn = kv * lax.rsqrt(jnp.mean(kv * kv, -1, keepdims=True) + EPS) * kg_ref[...]
        pos = lax.broadcasted_iota(jnp.int32, (S, A_LAT), 0)
        lane = lax.broadcasted_iota(jnp.int32, (S, A_LAT), 1)
        hi = (pos >> 6).astype(f32)
        lo = (pos & 63).astype(f32)
        extra = jnp.where(lane == 0, hi, jnp.where(lane == 1, lo, jnp.where(lane == SUM_LANE, 1.0, 0.0)))
        kaug_scr[:, :A_LAT] = kn.astype(bf16)
        kaug_scr[:, A_LAT:] = extra.astype(bf16)
        ikb_scr[...] = ik_ref[0].astype(bf16)

    q = q_ref[0]
    lane = lax.broadcasted_iota(jnp.int32, (TQ, A_LAT), 1)
    for h in range(A_HEADS):
        qh = q[:, h * A_LAT:(h + 1) * A_LAT]
        qn = qh * lax.rsqrt(jnp.mean(qh * qh, -1, keepdims=True) + EPS) * qg_ref[...] * scale
        slope = 2.0 ** (-8.0 * (h + 1) / A_HEADS)
        extra = jnp.where(lane == 0, slope * 64.0, jnp.where(lane == 1, slope, 0.0))
        qaug_scr[h * TQ:(h + 1) * TQ, :A_LAT] = qn.astype(bf16)
        qaug_scr[h * TQ:(h + 1) * TQ, A_LAT:] = extra.astype(bf16)

    nchunks = (qi * TQ + TQ + TK - 1) // TK
    qpos = qi * TQ + lax.broadcasted_iota(jnp.int32, (TQ, 1), 0)
    qpos_row = qi * TQ + lax.broadcasted_iota(jnp.int32, (1, TQ), 1)
    colb = lax.broadcasted_iota(jnp.int32, (TQ, TK), 1)
    rowb = lax.broadcasted_iota(jnp.int32, (TK, TQ), 0)
    iq = iq_ref[0]
    iqb = [iq[:, h * IDX_DIM:(h + 1) * IDX_DIM].astype(bf16) for h in range(IDX_HEADS)]
    iwt = iwt_ref[0]

    def score_chunk(c, carry):
        off = pl.multiple_of(c * TK, TK)
        ikc = ikb_scr[pl.ds(off, TK), :]
        acc = jnp.zeros((TK, TQ), f32)
        for h in range(IDX_HEADS):
            lg = lax.dot_general(ikc, iqb[h], (((1,), (1,)), ((), ())), preferred_element_type=f32)
            acc = acc + jnp.maximum(lg, 0.0) * iwt[h:h + 1, :]
        sc = acc * idx_scale
        sc = jnp.where(rowb + off <= qpos_row, sc, -jnp.inf)
        bits = pltpu.bitcast(sc, jnp.int32)
        key = jnp.where(bits < 0, bits ^ jnp.int32(0x7FFFFFFF), bits)
        key_scr[:, pl.ds(off, TK)] = key.T
        khi_scr[pl.ds(off, TK), :] = (key >> 16).astype(jnp.int16)
        klo_scr[pl.ds(off, TK), :] = ((key & 0xFFFF) - 32768).astype(jnp.int16)
        return carry

    lax.fori_loop(0, nchunks, score_chunk, 0)

    def tile16(row):
        return jnp.broadcast_to(row, (16, TQ)).astype(jnp.int16)

    def threshold_search(n):
        rows_n = n * TK

        def count16(ref, cmp, row):
            tile = tile16(row)
            x = ref[0:rows_n, :]
            hits = [cmp(x[j * 16:(j + 1) * 16], tile).astype(jnp.int16) for j in range(rows_n // 16)]
            while len(hits) > 1:
                hits = [a + b for a, b in zip(hits[::2], hits[1::2])] + hits[len(hits) & ~1:]
            return jnp.sum(hits[0].astype(jnp.int32), axis=0, keepdims=True)

        def top_half_word(ref, wanted):
            def bit_step(i, t):
                cand = t + lax.shift_left(jnp.int32(1), jnp.int32(15) - i)
                cnt = count16(ref, lambda k, c: k >= c, cand)
                return jnp.where(cnt >= wanted, cand, t)
            return lax.fori_loop(0, 16, bit_step, jnp.full((1, TQ), -32768, jnp.int32))

        t_hi = top_half_word(khi_scr, topk)
        above = count16(khi_scr, lambda k, c: k > c, t_hi)
        t_hi_tile = tile16(t_hi)
        hi = khi_scr[0:rows_n, :]
        lo = klo_scr[0:rows_n, :]
        kept = [jnp.where(hi[j * 16:(j + 1) * 16] == t_hi_tile, lo[j * 16:(j + 1) * 16], jnp.int16(-32768))
                for j in range(rows_n // 16)]
        klo_scr[0:rows_n, :] = jnp.concatenate(kept, axis=0)
        t_lo = top_half_word(klo_scr, topk - above)
        n_gt = above + count16(klo_scr, lambda k, c: k > c, t_lo)
        return t_hi, t_lo, n_gt

    t_hi, t_lo, n_gt = lax.switch(nchunks - 1,
                                  [functools.partial(threshold_search, n) for n in range(1, S // TK + 1)])
    T = jnp.broadcast_to(t_hi * 65536 + (t_lo + 32768), (TQ, TQ)).T[:, 0:1]
    room = jnp.broadcast_to((topk - n_gt).astype(f32), (TQ, TQ)).T[:, 0:1]

    m_scr, acc_scr = head_scr[:A_HEADS], head_scr[A_HEADS:]
    for h in range(A_HEADS):
        m_scr[h][...] = jnp.full((TQ, 1), -jnp.inf, f32)
        acc_scr[h][...] = jnp.zeros((TQ, 2 * A_LAT), f32)

    def attend_chunk(c, ties_before):
        off = pl.multiple_of(c * TK, TK)
        keyc = key_scr[:, pl.ds(off, TK)]
        eq = keyc == T
        eqb = jnp.where(eq, 1.0, 0.0).astype(bf16)
        ties = ties_before
        admit = []
        for j in range(TK // 128):
            blk = slice(j * 128, (j + 1) * 128)
            cnt = jnp.dot(eqb[:, blk], tri_ref[...], preferred_element_type=f32)
            admit.append(eq[:, blk] & (cnt[:, :128] + ties < room))
            ties = ties + cnt[:, 128:]
        sel = (keyc > T) | jnp.concatenate(admit, axis=1)
        sel = sel & (colb + off <= qpos)
        kc = kaug_scr[pl.ds(off, TK), :]
        scores = lax.dot_general(qaug_scr[...], kc, (((1,), (1,)), ((), ())), preferred_element_type=f32)
        probs, alphas = [], []
        for h in range(A_HEADS):
            s = jnp.where(sel, scores[h * TQ:(h + 1) * TQ], -jnp.inf)
            m_old = m_scr[h][...]
            m_new = jnp.maximum(m_old, jnp.max(s, axis=1, keepdims=True))
            m_safe = jnp.where(m_new == -jnp.inf, 0.0, m_new)
            probs.append(jnp.exp(s - m_safe).astype(bf16))
            alphas.append(jnp.exp(m_old - m_safe))
            m_scr[h][...] = m_new
        for h in range(A_HEADS):
            acc_scr[h][...] = alphas[h] * acc_scr[h][...] + jnp.dot(probs[h], kc, preferred_element_type=f32)
        return ties

    lax.fori_loop(0, nchunks, attend_chunk, jnp.zeros((TQ, 128), f32))

    heads = []
    for h in range(A_HEADS):
        acc = acc_scr[h][...]
        row_sum = acc[:, A_LAT + SUM_LANE:A_LAT + SUM_LANE + 1]
        heads.append((acc[:, :A_LAT] / row_sum).astype(bf16))
    o = jnp.concatenate(heads, axis=1)
    o_ref[0] = jnp.dot(o, wbd_ref[...], preferred_element_type=f32)


def dsa_attention(q_lat, kv, iq, ik, iw, q_gain, k_gain, w_uv, *, TQ=128, TK=1024):
    B, S, _ = q_lat.shape
    topk = min(IDX_TOPK_MAX, S // 4)
    TK = min(TK, S)
    wbd = jnp.zeros((A_HEADS * A_LAT, A_WIDTH), f32)
    for h in range(A_HEADS):
        wbd = wbd.at[h * A_LAT:(h + 1) * A_LAT, h * A_VDIM:(h + 1) * A_VDIM].set(w_uv[h])
    blk = jnp.arange(128)
    tri = jnp.concatenate([blk[:, None] < blk[None, :], jnp.ones((128, 128), bool)], axis=1).astype(bf16)
    kern = functools.partial(_dsa_kernel, S=S, TQ=TQ, TK=TK, topk=topk)
    return pl.pallas_call(
        kern,
        grid=(B, S // TQ),
        in_specs=[
            pl.BlockSpec((1, TQ, A_HEADS * A_LAT), lambda b, i: (b, i, 0)),
            pl.BlockSpec((1, S, A_LAT), lambda b, i: (b, 0, 0)),
            pl.BlockSpec((1, TQ, IDX_HEADS * IDX_DIM), lambda b, i: (b, i, 0)),
            pl.BlockSpec((1, S, IDX_DIM), lambda b, i: (b, 0, 0)),
            pl.BlockSpec((1, IDX_HEADS, TQ), lambda b, i: (b, 0, i)),
            pl.BlockSpec((1, A_LAT), lambda b, i: (0, 0)),
            pl.BlockSpec((1, A_LAT), lambda b, i: (0, 0)),
            pl.BlockSpec((A_HEADS * A_LAT, A_WIDTH), lambda b, i: (0, 0)),
            pl.BlockSpec((128, 256), lambda b, i: (0, 0)),
        ],
        out_specs=pl.BlockSpec((1, TQ, A_WIDTH), lambda b, i: (b, i, 0)),
        out_shape=jax.ShapeDtypeStruct((B, S, A_WIDTH), f32),
        scratch_shapes=[
            pltpu.VMEM((S, 2 * A_LAT), bf16),
            pltpu.VMEM((S, IDX_DIM), bf16),
            pltpu.VMEM((TQ, S), jnp.int32),
            pltpu.VMEM((S, TQ), jnp.int16),
            pltpu.VMEM((S, TQ), jnp.int16),
            pltpu.VMEM((A_HEADS * TQ, 2 * A_LAT), bf16),
            *[pltpu.VMEM((TQ, 1), f32) for _ in range(A_HEADS)],
            *[pltpu.VMEM((TQ, 2 * A_LAT), f32) for _ in range(A_HEADS)],
        ],
        compiler_params=pltpu.CompilerParams(dimension_semantics=("arbitrary", "arbitrary")),
        name="dsa_attention",
    )(q_lat, kv, iq, ik, iw.transpose(0, 2, 1), q_gain.reshape(1, -1), k_gain.reshape(1, -1),
      wbd.astype(bf16), tri)


CONV_HALO = 8


def _split_bf16(x, terms):
    parts = []
    for _ in range(terms):
        p = x.astype(bf16)
        parts.append(p)
        x = x - p.astype(f32)
    return parts


def _dot_exact_lhs(mask_bf16, x):
    return sum(jnp.dot(mask_bf16, p, preferred_element_type=f32) for p in _split_bf16(x, 3))


def _dot_exact_rhs(x, mask_bf16):
    return sum(jnp.dot(p, mask_bf16, preferred_element_type=f32) for p in _split_bf16(x, 3))


def _dot_3pass(a, b):
    ah, al = _split_bf16(a, 2)
    bh, bl = _split_bf16(b, 2)
    return (jnp.dot(ah, bh, preferred_element_type=f32) + jnp.dot(ah, bl, preferred_element_type=f32)
            + jnp.dot(al, bh, preferred_element_type=f32))


def _softplus(x):
    return jnp.maximum(x, 0.0) + jnp.log1p(jnp.exp(-jnp.abs(x)))


def _gdn_kernel(q_ref, k_ref, v_ref, z_ref, a_ref, b_ref, at_ref, bt_ref, cw_ref, alog_ref, dtb_ref,
                gain_ref, o_ref, tail_scr, act_scr, *state_scr, SB):
    sj = pl.program_id(1)

    @pl.when(sj == 0)
    def _reset():
        tail_scr[...] = jnp.zeros(tail_scr.shape, f32)
        for st_ref in state_scr:
            st_ref[...] = jnp.zeros((B_DIM, B_DIM), f32)

    cw = cw_ref[...]
    for idx, x_ref in enumerate((q_ref, k_ref, v_ref)):
        x = x_ref[0]
        xc = jnp.concatenate([tail_scr[idx], x], axis=0)
        y = jnp.zeros((SB, B_WIDTH), f32)
        for j in range(CONV_W):
            lo = CONV_HALO - (CONV_W - 1) + j
            y = y + cw[j:j + 1, idx * B_WIDTH:(idx + 1) * B_WIDTH] * xc[lo:lo + SB]
        tail_scr[idx] = x[SB - CONV_HALO:SB]
        act_scr[idx] = y * jax.nn.sigmoid(y)

    ri = lax.broadcasted_iota(jnp.int32, (CHUNK, CHUNK), 0)
    ci = lax.broadcasted_iota(jnp.int32, (CHUNK, CHUNK), 1)
    tril = ri >= ci
    strict = ri > ci
    tril_b = jnp.where(tril, 1.0, 0.0).astype(bf16)
    triu_b = jnp.where(ri <= ci, 1.0, 0.0).astype(bf16)
    eye = jnp.where(ri == ci, 1.0, 0.0)
    nt = (((1,), (1,)), ((), ()))

    def chunk_pair(j, carry):
        H = range(B_HEADS)
        lanes = [slice(h * B_DIM, (h + 1) * B_DIM) for h in H]
        lp = dict(preferred_element_type=f32)
        chunks = (2 * j, 2 * j + 1)
        rows = [pl.ds(pl.multiple_of(c * CHUNK, CHUNK), CHUNK) for c in chunks]
        I = [(ci, h) for ci in range(2) for h in H]
        q = [act_scr[0, rows[ci], lanes[h]] for ci, h in I]
        k = [act_scr[1, rows[ci], lanes[h]] for ci, h in I]
        v = [act_scr[2, rows[ci], lanes[h]] for ci, h in I]
        q = [x * lax.rsqrt(jnp.sum(x * x, -1, keepdims=True) + EPS) * (B_DIM ** -0.5) for x in q]
        k = [x * lax.rsqrt(jnp.sum(x * x, -1, keepdims=True) + EPS) for x in k]
        neg_rate = [-jnp.exp(alog_ref[0:1, h:h + 1]) for h in H]
        dtb = [dtb_ref[0:1, h:h + 1] for h in H]
        g_col = [neg_rate[h] * _softplus(a_ref[0, rows[ci], h:h + 1] + dtb[h]) for ci, h in I]
        g_row = [neg_rate[h] * _softplus(at_ref[0, h, pl.ds(chunks[ci], 1), :] + dtb[h]) for ci, h in I]
        beta = [jax.nn.sigmoid(b_ref[0, rows[ci], h:h + 1]) for ci, h in I]
        N = range(len(I))
        Gc = [_dot_exact_lhs(tril_b, jnp.broadcast_to(g_col[n], (CHUNK, CHUNK))) for n in N]
        Gr = [_dot_exact_rhs(jnp.broadcast_to(g_row[n], (CHUNK, CHUNK)), triu_b) for n in N]
        G = [x[:, 0:1] for x in Gc]
        G_last = [x[CHUNK - 1:CHUNK, 0:1] for x in Gc]
        decay = [jnp.exp(jnp.where(tril, Gc[n] - Gr[n], -jnp.inf)) for n in N]
        kb = [k[n] * beta[n] for n in N]
        kbf = [x.astype(bf16) for x in k]
        L = [jnp.where(strict, lax.dot_general(kb[n].astype(bf16), kbf[n], nt, **lp) * decay[n], 0.0) for n in N]
        P = [eye - x for x in L]
        M = [_dot_3pass(x, x) for x in L]
        for lvl in range(5):
            P = [P[n] + _dot_3pass(P[n], M[n]) for n in N]
            if lvl < 4:
                M = [_dot_3pass(x, x) for x in M]
        Tb = [x.astype(bf16) for x in P]
        eG = [jnp.exp(x) for x in G]
        u = [jnp.dot(Tb[n], (v[n] * beta[n]).astype(bf16), **lp) for n in N]
        w = [jnp.dot(Tb[n], (kb[n] * eG[n]).astype(bf16), **lp) for n in N]
        attn = [lax.dot_general(q[n].astype(bf16), kbf[n], nt, **lp) * decay[n] for n in N]
        q_dec = [q[n] * eG[n] for n in N]
        k_dec = [k[n] * jnp.exp(G_last[n] - G[n]) for n in N]
        st = [state_scr[h][...] for h in H]
        for ci in range(2):
            ns = [ci * B_HEADS + h for h in H]
            stb = [x.astype(bf16) for x in st]
            v_new = [u[n] - jnp.dot(w[n].astype(bf16), stb[h], **lp) for h, n in zip(H, ns)]
            vnb = [x.astype(bf16) for x in v_new]
            o = [jnp.dot(q_dec[n].astype(bf16), stb[h], **lp) + jnp.dot(attn[n].astype(bf16), vnb[h], **lp)
                 for h, n in zip(H, ns)]
            st = [st[h] * jnp.exp(G_last[n]) + lax.dot_general(
                k_dec[n].astype(bf16), vnb[h], (((0,), (0,)), ((), ())), **lp) for h, n in zip(H, ns)]
            for h in H:
                zh = z_ref[0, rows[ci], lanes[h]]
                on = o[h] * lax.rsqrt(jnp.mean(o[h] * o[h], -1, keepdims=True) + EPS) * gain_ref[...]
                o_ref[0, rows[ci], lanes[h]] = on * (zh * jax.nn.sigmoid(zh))
        for h in H:
            state_scr[h][...] = st[h]
        return carry

    lax.fori_loop(0, SB // (2 * CHUNK), chunk_pair, 0)


def gated_deltanet(qkvz, a, b, conv_w, a_log, dt_bias, norm_gain, *, SB=512):
    B, S, _ = qkvz.shape
    SB = min(SB, S)
    nch = S // CHUNK
    at = a.transpose(0, 2, 1).reshape(B, B_HEADS, nch, CHUNK)
    bt = b.transpose(0, 2, 1).reshape(B, B_HEADS, nch, CHUNK)
    wide = pl.BlockSpec((1, SB, B_WIDTH), lambda bi, j: (bi, j, 0))
    part = [pl.BlockSpec((1, SB, B_WIDTH), functools.partial(lambda bi, j, n: (bi, j, n), n=n))
            for n in range(4)]
    narrow = pl.BlockSpec((1, SB, B_HEADS), lambda bi, j: (bi, j, 0))
    rowwise = pl.BlockSpec((1, B_HEADS, SB // CHUNK, CHUNK), lambda bi, j: (bi, 0, j, 0))
    return pl.pallas_call(
        functools.partial(_gdn_kernel, SB=SB),
        grid=(B, S // SB),
        in_specs=[*part, narrow, narrow, rowwise, rowwise,
                  pl.BlockSpec((CONV_W, 3 * B_WIDTH), lambda bi, j: (0, 0)),
                  pl.BlockSpec((1, B_HEADS), lambda bi, j: (0, 0)),
                  pl.BlockSpec((1, B_HEADS), lambda bi, j: (0, 0)),
                  pl.BlockSpec((1, B_DIM), lambda bi, j: (0, 0))],
        out_specs=wide,
        out_shape=jax.ShapeDtypeStruct((B, S, B_WIDTH), f32),
        scratch_shapes=[pltpu.VMEM((3, CONV_HALO, B_WIDTH), f32),
                        pltpu.VMEM((3, SB, B_WIDTH), f32),
                        *[pltpu.VMEM((B_DIM, B_DIM), f32) for _ in range(B_HEADS)]],
        compiler_params=pltpu.CompilerParams(dimension_semantics=("arbitrary", "arbitrary")),
        name="gated_deltanet",
    )(qkvz, qkvz, qkvz, qkvz, a, b, at, bt, conv_w,
      a_log.reshape(1, -1), dt_bias.reshape(1, -1), norm_gain.reshape(1, -1))


PEER_SLOTS = PEER_HEADS * PEER_TOPK
PEER_HALF = PEER_QDIM // 2
ROW_WORDS = D_MODEL // 2
ROW_SUBL = ROW_WORDS // 128
STAGE_STRIDE = PEER_SLOTS + 8


def _top16_rows(s, iota_rows, fill, vals_scr, idx_scr, payload=None, pay_scr=None):
    for r in range(PEER_TOPK):
        m = jnp.max(s, axis=0, keepdims=True)
        first = jnp.min(jnp.where(s == m, iota_rows, fill), axis=0, keepdims=True)
        taken = iota_rows == first
        vals_scr[r:r + 1, :] = m
        if payload is None:
            idx_scr[r:r + 1, :] = first.astype(jnp.int32)
        else:
            pay_scr[r:r + 1, :] = jnp.max(jnp.where(taken, payload, -1.0), axis=0,
                                          keepdims=True).astype(jnp.int32)
        s = jnp.where(taken, -jnp.inf, s)


def _peer_route_kernel(h_ref, wq_ref, sk_ref, off_ref, gate_ref,
                       v1_scr, i1_scr, v2_scr, i2_scr, cv_scr, ce_scr, eid_scr, gate_scr, *, T):
    q = jnp.dot(h_ref[...].astype(bf16), wq_ref[...], preferred_element_type=f32).astype(bf16)
    gate_scr[...] = jnp.zeros(gate_scr.shape, f32)
    kiota = lax.broadcasted_iota(jnp.int32, (PEER_NKEYS, T), 0).astype(f32)
    sub8 = lax.broadcasted_iota(jnp.int32, (8, T), 0).astype(f32)
    for p in range(PEER_HEADS):
        for half, (vs, is_) in enumerate(((v1_scr, i1_scr), (v2_scr, i2_scr))):
            g = 2 * p + half
            s = lax.dot_general(sk_ref[g], q[:, g * PEER_HALF:(g + 1) * PEER_HALF],
                                (((1,), (1,)), ((), ())), preferred_element_type=f32)
            _top16_rows(s, kiota, PEER_NKEYS, vs, is_)
        v1 = v1_scr[...]
        v2 = v2_scr[...]
        e1 = (i1_scr[...] * PEER_NKEYS).astype(f32)
        e2 = i2_scr[...].astype(f32)
        cv = [v1[0:1] + v2[0:8], v1[0:1] + v2[8:16]]
        ce = [e1[0:1] + e2[0:8], e1[0:1] + e2[8:16]]
        cf = [sub8, sub8 + 8]
        for a in range(1, 8):
            cv.append(v1[a:a + 1] + v2[0:8])
            ce.append(e1[a:a + 1] + e2[0:8])
            cf.append(sub8 + a * PEER_TOPK)
        cv.append(v1[8:16] + v2[0:1])
        ce.append(e1[8:16] + e2[0:1])
        cf.append((sub8 + 8) * PEER_TOPK)
        cand = jnp.concatenate(cv, axis=0)
        cexp = jnp.concatenate(ce, axis=0)
        cflat = jnp.concatenate(cf, axis=0)
        _top16_rows(cand, cflat, PEER_TOPK * PEER_TOPK, cv_scr, None, payload=cexp, pay_scr=ce_scr)
        top = cv_scr[...]
        ex = jnp.exp(top - top[0:1])
        gate_scr[pl.ds(2 * p * PEER_TOPK + 1, PEER_TOPK, stride=2), :] = ex / jnp.sum(ex, axis=0, keepdims=True)
        eid_scr[p * PEER_TOPK:(p + 1) * PEER_TOPK, :] = ce_scr[...] * ROW_SUBL
    off_ref[...] = eid_scr[...]
    gate_ref[...] = gate_scr[...].T


def peer_route(h, w_query, sub_keys, *, T=128):
    N, D = h.shape
    sk = sub_keys.reshape(PEER_HEADS * 2, PEER_NKEYS, PEER_HALF).astype(bf16)
    kern = functools.partial(_peer_route_kernel, T=T)
    return pl.pallas_call(
        kern,
        grid=(N // T,),
        in_specs=[
            pl.BlockSpec((T, D), lambda i: (i, 0)),
            pl.BlockSpec((D, PEER_HEADS * PEER_QDIM), lambda i: (0, 0)),
            pl.BlockSpec((PEER_HEADS * 2, PEER_NKEYS, PEER_HALF), lambda i: (0, 0, 0)),
        ],
        out_specs=[pl.BlockSpec((PEER_SLOTS, T), lambda i: (0, i)),
                   pl.BlockSpec((T, 2 * PEER_SLOTS), lambda i: (i, 0))],
        out_shape=[jax.ShapeDtypeStruct((PEER_SLOTS, N), jnp.int32),
                   jax.ShapeDtypeStruct((N, 2 * PEER_SLOTS), f32)],
        scratch_shapes=[
            pltpu.VMEM((PEER_TOPK, T), f32), pltpu.VMEM((PEER_TOPK, T), jnp.int32),
            pltpu.VMEM((PEER_TOPK, T), f32), pltpu.VMEM((PEER_TOPK, T), jnp.int32),
            pltpu.VMEM((PEER_TOPK, T), f32), pltpu.VMEM((PEER_TOPK, T), jnp.int32),
            pltpu.VMEM((PEER_SLOTS, T), jnp.int32), pltpu.VMEM((2 * PEER_SLOTS, T), f32),
        ],
        compiler_params=pltpu.CompilerParams(dimension_semantics=("arbitrary",)),
        name="peer_route",
    )(h, w_query.astype(bf16), sk)


def pack_table(tab):
    bits = lax.bitcast_convert_type(tab.astype(bf16), jnp.uint16).astype(jnp.uint32)
    words = (bits[:, :ROW_WORDS] << 16) | bits[:, ROW_WORDS:]
    return lax.bitcast_convert_type(words, jnp.int32).reshape(tab.shape[0] * ROW_SUBL, 128)


GROUP = 8


def _fetch_slots(off_ref, t0, tab_ref, stage, first, last):
    for k in range(first, last):
        offs = off_ref.at[k, pl.ds(t0, GROUP)]
        for i in range(GROUP):
            off = pl.multiple_of(offs[i], ROW_SUBL)
            stage[i, pl.ds(k, ROW_SUBL, stride=STAGE_STRIDE), :] = tab_ref[pl.ds(off, ROW_SUBL), :]


def _staged_rows(stage, i):
    chunks = [stage[i, c * STAGE_STRIDE:c * STAGE_STRIDE + PEER_SLOTS, :] for c in range(ROW_SUBL)]
    return pltpu.bitcast(jnp.concatenate(chunks, axis=1), bf16)


def _pipelined_groups(off_ref, tab_ref, stage_a, stage_b, start, step, finish, T):
    ngroups = T // GROUP
    per_step = PEER_SLOTS // GROUP
    _fetch_slots(off_ref, 0, tab_ref, stage_a, 0, PEER_SLOTS)

    def run(g, stage, g_next, stage_next):
        t_next = pl.multiple_of(g_next * GROUP, GROUP)
        carry = start(g)
        for i in range(GROUP):
            carry = step(i, carry, stage)
            _fetch_slots(off_ref, t_next, tab_ref, stage_next, i * per_step, (i + 1) * per_step)
        finish(g, carry)

    def pair(j, carry):
        g0 = 2 * j
        run(g0, stage_a, g0 + 1, stage_b)
        run(g0 + 1, stage_b, jnp.minimum(g0 + 2, ngroups - 1), stage_a)
        return carry

    lax.fori_loop(0, ngroups // 2, pair, 0)


def _peer_u_kernel(off_ref, h_ref, gate_ref, tab_ref, w_ref, stage_a, stage_b, *, T):
    row16 = lax.broadcasted_iota(jnp.int32, (2 * GROUP, 2 * PEER_SLOTS), 0) & (GROUP - 1)
    odd = (lax.broadcasted_iota(jnp.int32, (GROUP, 2 * PEER_SLOTS), 1) & 1) == 1

    def start(g):
        h8 = h_ref[pl.ds(pl.multiple_of(g * GROUP, GROUP), GROUP), :]
        h16 = jnp.concatenate([h8[:, :ROW_WORDS], h8[:, ROW_WORDS:]], axis=0).astype(bf16)
        return h16, jnp.zeros((2 * GROUP, 2 * PEER_SLOTS), f32)

    def step(i, carry, stage):
        h16, acc = carry
        res = lax.dot_general(h16, _staged_rows(stage, i), (((1,), (1,)), ((), ())),
                              preferred_element_type=f32)
        return h16, jnp.where(row16 == i, res, acc)

    def finish(g, carry):
        t0 = pl.multiple_of(g * GROUP, GROUP)
        acc = carry[1]
        dots = acc[0:GROUP] + pltpu.roll(acc[GROUP:], 1, axis=1)
        gelu = 0.5 * dots * (1.0 + lax.erf(dots * (2.0 ** -0.5)))
        w_ref[pl.ds(t0, GROUP), :] = jnp.where(odd, gelu * gate_ref[pl.ds(t0, GROUP), :], 0.0)

    _pipelined_groups(off_ref, tab_ref, stage_a, stage_b, start, step, finish, T)


def _peer_v_kernel(off_ref, w_ref, x_ref, g2_ref, tab_ref, o_ref, stage_a, stage_b, *, T):
    row16 = lax.broadcasted_iota(jnp.int32, (2 * GROUP, ROW_WORDS), 0) & (GROUP - 1)

    def start(g):
        w8 = w_ref[pl.ds(pl.multiple_of(g * GROUP, GROUP), GROUP), :]
        w16 = jnp.concatenate([w8, pltpu.roll(w8, 2 * PEER_SLOTS - 1, axis=1)], axis=0).astype(bf16)
        return w16, jnp.zeros((2 * GROUP, ROW_WORDS), f32)

    def step(i, carry, stage):
        w16, acc = carry
        res = jnp.dot(w16, _staged_rows(stage, i), preferred_element_type=f32)
        return w16, jnp.where(row16 == i, res, acc)

    def finish(g, carry):
        t0 = pl.multiple_of(g * GROUP, GROUP)
        acc = carry[1]
        y = jnp.concatenate([acc[0:GROUP], acc[GROUP:]], axis=1)
        o_ref[pl.ds(t0, GROUP), :] = x_ref[pl.ds(t0, GROUP), :] + g2_ref[0] * y

    _pipelined_groups(off_ref, tab_ref, stage_a, stage_b, start, step, finish, T)


_TABLE_VMEM_LIMIT = 56 * 1024 * 1024


def _table_spec():
    return pl.BlockSpec((N_EXPERTS * ROW_SUBL, 128), lambda i: (0, 0), pipeline_mode=pl.Buffered(1))


def _stage_scratch():
    return [pltpu.VMEM((GROUP, ROW_SUBL * STAGE_STRIDE, 128), jnp.int32) for _ in range(2)]


def _offset_spec(T):
    return pl.BlockSpec((PEER_SLOTS, T), lambda i: (0, i), memory_space=pltpu.SMEM)


def peer_u(off, h, gate, u_pk, *, T=512):
    N, D = h.shape
    return pl.pallas_call(
        functools.partial(_peer_u_kernel, T=T),
        grid=(N // T,),
        in_specs=[
            _offset_spec(T),
            pl.BlockSpec((T, D), lambda i: (i, 0)),
            pl.BlockSpec((T, 2 * PEER_SLOTS), lambda i: (i, 0)),
            _table_spec(),
        ],
        out_specs=pl.BlockSpec((T, 2 * PEER_SLOTS), lambda i: (i, 0)),
        out_shape=jax.ShapeDtypeStruct((N, 2 * PEER_SLOTS), f32),
        scratch_shapes=_stage_scratch(),
        compiler_params=pltpu.CompilerParams(dimension_semantics=("arbitrary",),
                                             vmem_limit_bytes=_TABLE_VMEM_LIMIT),
        name="peer_u",
    )(off, h, gate, u_pk)


def peer_v(off, w, x, g2, v_pk, *, T=512):
    N, D = x.shape
    per_batch = N // g2.shape[0]
    return pl.pallas_call(
        functools.partial(_peer_v_kernel, T=T),
        grid=(N // T,),
        in_specs=[
            _offset_spec(T),
            pl.BlockSpec((T, 2 * PEER_SLOTS), lambda i: (i, 0)),
            pl.BlockSpec((T, D), lambda i: (i, 0)),
            pl.BlockSpec((1, 1, D), lambda i: (i * T // per_batch, 0, 0)),
            _table_spec(),
        ],
        out_specs=pl.BlockSpec((T, D), lambda i: (i, 0)),
        out_shape=jax.ShapeDtypeStruct((N, D), f32),
        scratch_shapes=_stage_scratch(),
        compiler_params=pltpu.CompilerParams(dimension_semantics=("arbitrary",),
                                             vmem_limit_bytes=_TABLE_VMEM_LIMIT),
        name="peer_v",
    )(off, w, x, g2.reshape(g2.shape[0], 1, D), v_pk)


def kernel(x, c, w_ada, b_ada, w_in, q_gain, k_gain, w_uv, conv_w, a_log, dt_bias,
           gdn_gain, w_out, w_query, sub_keys, u_tab, v_tab):
    B, S, D = x.shape
    xf = x.reshape(B * S, D)
    for l in range(DEPTH):
        sh1, sc1, g1, sh2, sc2, g2 = jnp.split(adaln(c, w_ada[l], b_ada[l]), 6, axis=-1)
        aq, akv, iq, small, qkvz = in_projection(xf, sh1, sc1, w_in[l])
        small = small.reshape(B, S, -1)
        ik, iw, ba, bb = (small[..., lo:hi] for lo, hi in (SMALL_IK, SMALL_IW, SMALL_BA, SMALL_BB))
        ya = dsa_attention(aq.reshape(B, S, -1), akv.reshape(B, S, -1), iq.reshape(B, S, -1), ik, iw,
                           q_gain[l], k_gain[l], w_uv[l])
        yb = gated_deltanet(qkvz.reshape(B, S, -1), ba, bb, conv_w[l], a_log[l], dt_bias[l], gdn_gain[l])
        x1, h2 = out_projection(ya.reshape(B * S, -1), yb.reshape(B * S, -1), xf, g1, sh2, sc2, w_out[l])
        off, gate = peer_route(h2, w_query[l], sub_keys[l])
        w = peer_u(off, h2, gate, pack_table(u_tab[l]))
        xf = peer_v(off, w, x1, g2, pack_table(v_tab[l]))
    return xf.reshape(B, S, D)
```
